```python
import math, functools
import jax, jax.numpy as jnp
from jax import lax
import numpy as np

D_MODEL = 1024
BATCH = 16
SEQ = 256
DEPTH = 4
DEC_BATCH = 2
DEC_SEQ = 4096
PAST_LEN = 512

GRID_W = 64
EPS = 1e-6
ATT_HEADS = 8
ATT_KV_HEADS = 2
HEAD_DIM = 64
ATT_WIDTH = ATT_HEADS * HEAD_DIM
KV_WIDTH = ATT_KV_HEADS * HEAD_DIM
ROPE_FREQS = HEAD_DIM // 4
ROPE_THETA = 10000.0
Q_BLOCK = 128
GLA_HEADS = 4
GLA_DK = 64
GLA_DV = 128
GLA_K_WIDTH = GLA_HEADS * GLA_DK
GLA_V_WIDTH = GLA_HEADS * GLA_DV
GLA_LOW_RANK = 16
GLA_TAU = 16.0
GLA_CHUNK = 64
FOURIER_GROUPS = 4
FOURIER_GROUP_DIM = 128
FOURIER_WIDTH = FOURIER_GROUPS * FOURIER_GROUP_DIM
BRANCH_WIDTH = 512
N_BRANCHES = 3
IN_WIDTHS = (ATT_WIDTH, KV_WIDTH, KV_WIDTH,
             GLA_K_WIDTH, GLA_K_WIDTH, GLA_V_WIDTH, GLA_V_WIDTH, GLA_LOW_RANK, GLA_LOW_RANK,
             FOURIER_WIDTH, N_BRANCHES * D_MODEL)
IN_COLS = sum(IN_WIDTHS)
IN_SPLITS = tuple(int(s) for s in np.cumsum(IN_WIDTHS)[:-1])
D_FF = 2816
N_EXPERTS = 8
TOP_K = 2
D_EXPERT = 3584
N_DENSE = (DEPTH + 1) // 2
N_MOE = DEPTH // 2

kernel_name = 'hybrid_diffusion_gqa_gla_fourier_prefix_step'


def rms_norm(x, g):
    xf = x.astype(jnp.float32)
    y = xf * lax.rsqrt(jnp.mean(xf * xf, axis=-1, keepdims=True) + EPS)
    return (y * g.astype(jnp.float32)).astype(x.dtype)


def axial_rope_tables(n_tokens):
    rows = n_tokens // GRID_W
    row = jnp.broadcast_to(jnp.arange(rows, dtype=jnp.float32)[:, None], (rows, GRID_W)).reshape(-1)
    col = jnp.broadcast_to(jnp.arange(GRID_W, dtype=jnp.float32)[None, :], (rows, GRID_W)).reshape(-1)
    inv_freq = 1.0 / (ROPE_THETA ** (jnp.arange(ROPE_FREQS, dtype=jnp.float32) / ROPE_FREQS))
    ang = jnp.stack([row, col], axis=-1)[:, :, None] * inv_freq
    return jnp.cos(ang), jnp.sin(ang)


def apply_axial_rope(x, cos, sin):
    b, l, h, d = x.shape
    xr = x.reshape(b, l, h, 2, 2, ROPE_FREQS)
    x1, x2 = xr[..., 0, :], xr[..., 1, :]
    c = cos[None, :, None].astype(x.dtype)
    s = sin[None, :, None].astype(x.dtype)
    out = jnp.stack([x1 * c - x2 * s, x2 * c + x1 * s], axis=-2)
    return out.reshape(b, l, h, d)


def blocked_attention(q, k, v):
    b, lq = q.shape[0], q.shape[1]
    nblk = lq // Q_BLOCK
    grp = ATT_HEADS // ATT_KV_HEADS
    qb = q.reshape(b, nblk, Q_BLOCK, ATT_KV_HEADS, grp, HEAD_DIM).transpose(1, 0, 2, 3, 4, 5)
    scale = HEAD_DIM ** -0.5

    def one_block(qblk):
        s = jnp.einsum('bqhgd,bkhd->bhgqk', qblk, k, preferred_element_type=jnp.float32) * scale
        p = jax.nn.softmax(s, axis=-1).astype(v.dtype)
        return jnp.einsum('bhgqk,bkhd->bqhgd', p, v)

    out = lax.map(one_block, qb)
    return out.transpose(1, 0, 2, 3, 4, 5).reshape(b, lq, ATT_WIDTH)


def gla_chunk_scan(q, k, v, log_a, s0):
    b, l, h = q.shape[0], q.shape[1], q.shape[2]
    n = l // GLA_CHUNK

    def to_chunks(t):
        return t.astype(jnp.float32).reshape(b, n, GLA_CHUNK, h, t.shape[-1]).transpose(1, 0, 3, 2, 4)

    causal = jnp.tril(jnp.ones((GLA_CHUNK, GLA_CHUNK), dtype=bool))[:, :, None]

    def step(s, inp):
        qc, kc, vc, ac = inp
        bc = jnp.cumsum(ac, axis=2)
        inter = jnp.einsum('bhtd,bhde->bhte', qc * jnp.exp(bc), s)
        rel = jnp.where(causal, bc[:, :, :, None, :] - bc[:, :, None, :, :], -jnp.inf)
        scores = jnp.einsum('bhtd,bhsd,bhtsd->bhts', qc, kc, jnp.exp(rel))
        intra = jnp.einsum('bhts,bhse->bhte', scores, vc)
        b_end = bc[:, :, -1:, :]
        s_new = jnp.exp(b_end[:, :, 0, :, None]) * s + jnp.einsum('bhsd,bhse->bhde', kc * jnp.exp(b_end - bc), vc)
        return s_new, inter + intra

    s_fin, out = lax.scan(step, s0.astype(jnp.float32),
                          (to_chunks(q), to_chunks(k), to_chunks(v), to_chunks(log_a)))
    out = out.transpose(1, 0, 3, 2, 4).reshape(b, l, h, v.shape[-1])
    return out.astype(v.dtype), s_fin.astype(v.dtype)


def gla_bidirectional(q, k, v, la_f, la_b, s0_f, s0_b):
    o_f, s_f = gla_chunk_scan(q, k, v, la_f, s0_f)
    flip = lambda t: t[:, ::-1]
    o_b, s_b = gla_chunk_scan(flip(q), flip(k), flip(v), flip(la_b), s0_b)
    return o_f + flip(o_b), s_f, s_b


def fourier_mix(u):
    b, l, _ = u.shape
    ug = u.reshape(b, l, FOURIER_GROUPS, FOURIER_GROUP_DIM).astype(jnp.float32)
    f = jnp.fft.fftn(ug, axes=(1, 3), norm='ortho').real
    return f.reshape(b, l, FOURIER_WIDTH).astype(u.dtype)


def swiglu(h, w1, w3, w2):
    return (jax.nn.silu(h @ w1) * (h @ w3)) @ w2


def moe_swiglu(h, router_w, w1, w3, w2):
    b, l, d = h.shape
    t = h.reshape(b * l, d)
    logits = (t @ router_w).astype(jnp.float32)
    top_v, top_i = lax.top_k(logits, TOP_K)
    top_w = jax.nn.softmax(top_v, axis=-1)
    comb = jnp.sum(jax.nn.one_hot(top_i, N_EXPERTS, dtype=jnp.float32) * top_w[..., None], axis=1).astype(h.dtype)
    y = jnp.zeros_like(t)
    for e in range(N_EXPERTS):
        y = y + comb[:, e:e + 1] * swiglu(t, w1[e], w3[e], w2[e])
    return y.reshape(b, l, d)


def token_mixer(h, p, rope, ctx_k, ctx_v, s0_f, s0_b):
    b, l, _ = h.shape
    z = h @ p['w_in']
    (aq, ak, av, gq, gk, gv, gr, lr_f, lr_b, fu, gate_logits) = jnp.split(z, IN_SPLITS, axis=-1)
    q = rms_norm(aq.reshape(b, l, ATT_HEADS, HEAD_DIM), p['q_norm_g'])
    k = rms_norm(ak.reshape(b, l, ATT_KV_HEADS, HEAD_DIM), p['k_norm_g'])
    v = av.reshape(b, l, ATT_KV_HEADS, HEAD_DIM)
    if rope is None:
        keys, vals = k, v
    else:
        cos, sin = rope
        q = apply_axial_rope(q, cos, sin)
        keys = jnp.concatenate([apply_axial_rope(k, cos, sin), ctx_k.astype(k.dtype)], axis=1)
        vals = jnp.concatenate([v, ctx_v.astype(v.dtype)], axis=1)
    att = blocked_attention(q, keys, vals)
    shp_k = (b, l, GLA_HEADS, GLA_DK)
    shp_v = (b, l, GLA_HEADS, GLA_DV)
    la_f = jax.nn.log_sigmoid((lr_f @ p['alpha_up'][0] + p['alpha_b'][0]).astype(jnp.float32)) / GLA_TAU
    la_b = jax.nn.log_sigmoid((lr_b @ p['alpha_up'][1] + p['alpha_b'][1]).astype(jnp.float32)) / GLA_TAU
    o, s_f, s_b = gla_bidirectional(gq.reshape(shp_k) * (GLA_DK ** -0.5), gk.reshape(shp_k), gv.reshape(shp_v),
                                    la_f.reshape(shp_k), la_b.reshape(shp_k), s0_f, s0_b)
    o = rms_norm(o, p['gla_norm_g']) * jax.nn.silu(gr.reshape(shp_v))
    gla_out = o.reshape(b, l, GLA_V_WIDTH)
    four = fourier_mix(fu)
    gates = jax.nn.sigmoid(gate_logits.reshape(b, l, N_BRANCHES, D_MODEL))
    merged = (gates[:, :, 0] * (att @ p['w_branch'][0])
              + gates[:, :, 1] * (gla_out @ p['w_branch'][1])
              + gates[:, :, 2] * (four @ p['w_branch'][2]))
    return merged @ p['w_out'], k, v, s_f, s_b


def trunk_layer(x, mod, p, ffn, rope, ctx_k, ctx_v, s0_f, s0_b):
    shift1, scale1, gate1, shift2, scale2, gate2 = jnp.split(mod, 6, axis=-1)
    h = rms_norm(x, p['norm1_g']) * (1 + scale1) + shift1
    mix, k, v, s_f, s_b = token_mixer(h, p, rope, ctx_k, ctx_v, s0_f, s0_b)
    x = x + gate1 * mix
    h = rms_norm(x, p['norm2_g']) * (1 + scale2) + shift2
    x = x + gate2 * ffn(h)
    return x, k, v, s_f, s_b


def setup_inputs(seed: int = 0) -> dict:
    key = jax.random.key(seed)
    ks = jax.random.split(key, 27)
    D = D_MODEL
    nrm = lambda k, shape, scale: jax.random.normal(k, shape, jnp.float32) * scale
    return {
        'x_prompt': nrm(ks[0], (BATCH, SEQ, D), 1.0),
        'x_sample': nrm(ks[1], (DEC_BATCH, DEC_SEQ, D), 1.0),
        'cache_k': nrm(ks[2], (DEC_BATCH, DEPTH, PAST_LEN, ATT_KV_HEADS, HEAD_DIM), 1.0),
        'cache_v': nrm(ks[3], (DEC_BATCH, DEPTH, PAST_LEN, ATT_KV_HEADS, HEAD_DIM), 1.0),
        'state_gla': nrm(ks[4], (DEC_BATCH, DEPTH, 2, GLA_HEADS, GLA_DK, GLA_DV), 1.0),
        'c': nrm(ks[5], (DEC_BATCH, D), 1.0),
        'c_ctx': nrm(ks[6], (D,), 1.0),
        'ada_w': nrm(ks[7], (DEPTH, D, 6 * D), 0.5 * D ** -0.5),
        'ada_b': nrm(ks[8], (DEPTH, 6 * D), 0.02),
        'norm1_g': 1.0 + nrm(ks[9], (DEPTH, D), 0.02),
        'norm2_g': 1.0 + nrm(ks[10], (DEPTH, D), 0.02),
        'w_in': nrm(ks[11], (DEPTH, D, IN_COLS), D ** -0.5),
        'q_norm_g': 1.0 + nrm(ks[12], (DEPTH, HEAD_DIM), 0.02),
        'k_norm_g': 1.0 + nrm(ks[13], (DEPTH, HEAD_DIM), 0.02),
        'alpha_up': nrm(ks[14], (DEPTH, 2, GLA_LOW_RANK, GLA_K_WIDTH), GLA_LOW_RANK ** -0.5),
        'alpha_b': nrm(ks[15], (DEPTH, 2, GLA_K_WIDTH), 0.1),
        'gla_norm_g': 1.0 + nrm(ks[16], (DEPTH, GLA_DV), 0.02),
        'w_branch': nrm(ks[17], (DEPTH, N_BRANCHES, BRANCH_WIDTH, D), BRANCH_WIDTH ** -0.5),
        'w_out': nrm(ks[18], (DEPTH, D, D), D ** -0.5),
        'ffn_w1': nrm(ks[19], (N_DENSE, D, D_FF), D ** -0.5),
        'ffn_w3': nrm(ks[20], (N_DENSE, D, D_FF), D ** -0.5),
        'ffn_w2': nrm(ks[21], (N_DENSE, D_FF, D), D_FF ** -0.5),
        'router_w': nrm(ks[22], (N_MOE, D, N_EXPERTS), D ** -0.5),
        'moe_w1': nrm(ks[23], (N_MOE, N_EXPERTS, D, D_EXPERT), D ** -0.5),
        'moe_w3': nrm(ks[24], (N_MOE, N_EXPERTS, D, D_EXPERT), D ** -0.5),
        'moe_w2': nrm(ks[25], (N_MOE, N_EXPERTS, D_EXPERT, D), D_EXPERT ** -0.5),
        'final_g': 1.0 + nrm(ks[26], (D,), 0.02),
    }


def reference(x_prompt, x_sample, cache_k, cache_v, state_gla, c, c_ctx, ada_w, ada_b, norm1_g, norm2_g,
              w_in, q_norm_g, k_norm_g, alpha_up, alpha_b, gla_norm_g, w_branch, w_out,
              ffn_w1, ffn_w3, ffn_w2, router_w, moe_w1, moe_w3, moe_w2, final_g):
    ctx_cond = jax.nn.silu(c_ctx)
    lat_cond = jax.nn.silu(c)
    rope = axial_rope_tables(x_sample.shape[1])
    zero_state = jnp.zeros((x_prompt.shape[0], GLA_HEADS, GLA_DK, GLA_DV), x_prompt.dtype)
    xp, xs = x_prompt, x_sample
    new_k, new_v, new_s = [], [], []
    for l in range(DEPTH):
        p = {'norm1_g': norm1_g[l], 'norm2_g': norm2_g[l], 'w_in': w_in[l], 'q_norm_g': q_norm_g[l],
             'k_norm_g': k_norm_g[l], 'alpha_up': alpha_up[l], 'alpha_b': alpha_b[l],
             'gla_norm_g': gla_norm_g[l], 'w_branch': w_branch[l], 'w_out': w_out[l]}
        j = l // 2
        if l % 2 == 0:
            ffn = functools.partial(swiglu, w1=ffn_w1[j], w3=ffn_w3[j], w2=ffn_w2[j])
        else:
            ffn = functools.partial(moe_swiglu, router_w=router_w[j], w1=moe_w1[j], w3=moe_w3[j], w2=moe_w2[j])
        mod_ctx = ctx_cond @ ada_w[l] + ada_b[l]
        mod_lat = (lat_cond @ ada_w[l] + ada_b[l])[:, None, :]
        xp, kc, vc, sf, sb = trunk_layer(xp, mod_ctx, p, ffn, None, None, None, zero_state, zero_state)
        new_k.append(kc)
        new_v.append(vc)
        new_s.append(jnp.stack([sf, sb], axis=1))
        xs, _, _, _, _ = trunk_layer(xs, mod_lat, p, ffn, rope, cache_k[:, l], cache_v[:, l],
                                     state_gla[:, l, 0], state_gla[:, l, 1])
    y_prompt = rms_norm(xp, final_g)
    y_sample = rms_norm(xs, final_g)
    new_cache_k = jnp.stack(new_k, axis=1)
    new_cache_v = jnp.stack(new_v, axis=1)
    new_state_gla = jnp.stack(new_s, axis=1)
    return (y_prompt, y_sample, new_cache_k, new_cache_v, new_state_gla)
```

```python
import functools
import math

import jax
import jax.numpy as jnp
from jax import lax
from jax.experimental import pallas as pl
from jax.experimental.pallas import tpu as pltpu

F32 = jnp.float32
BF16 = jnp.bfloat16

EPS = 1e-6
ATT_HEADS = 8
ATT_KV_HEADS = 2
HEAD_DIM = 64
ATT_WIDTH = ATT_HEADS * HEAD_DIM
KV_WIDTH = ATT_KV_HEADS * HEAD_DIM
ROPE_FREQS = HEAD_DIM // 4
ROPE_THETA = 10000.0
GRID_W = 64
GLA_HEADS = 4
GLA_DK = 64
GLA_DV = 128
GLA_K_WIDTH = GLA_HEADS * GLA_DK
GLA_V_WIDTH = GLA_HEADS * GLA_DV
GLA_LOW_RANK = 16
GLA_TAU = 16.0
FOURIER_GROUPS = 4
FOURIER_GROUP_DIM = 128
FOURIER_WIDTH = FOURIER_GROUPS * FOURIER_GROUP_DIM
N_BRANCHES = 3
N_EXPERTS = 8

LANES = 128
VMEM_LIMIT_BYTES = 56 * 1024 * 1024

COL_AQ = 0
COL_GV = 512
COL_GR = 1024
COL_FU = 1536
COL_GATES = 2048

GLA_CHUNK = 256
GLA_SAFE_LOG = 60.0


def _cparams(sem):
    return pltpu.CompilerParams(dimension_semantics=sem, vmem_limit_bytes=VMEM_LIMIT_BYTES)


def _resident(shape, index_map):
    return pl.BlockSpec(shape, index_map, pipeline_mode=pl.Buffered(1))


def _split2(x):
    hi = x.astype(BF16)
    lo = (x - hi.astype(F32)).astype(BF16)
    return hi, lo


def _split3(x):
    hi = x.astype(BF16)
    r = x - hi.astype(F32)
    mid = r.astype(BF16)
    lo = (r - mid.astype(F32)).astype(BF16)
    return hi, mid, lo


def _dot(a, b):
    return jnp.dot(a, b, preferred_element_type=F32)


def _dot_nt(a, b):
    return lax.dot_general(a, b, (((1,), (1,)), ((), ())), preferred_element_type=F32)


def _dot_tn(a, b):
    return lax.dot_general(a, b, (((0,), (0,)), ((), ())), preferred_element_type=F32)


def _dot3(a, b):
    a_hi, a_lo = _split2(a)
    b_hi, b_lo = _split2(b)
    return _dot(a_hi, b_hi) + _dot(a_lo, b_hi) + _dot(a_hi, b_lo)


def _sigmoid(x):
    return 1.0 / (1.0 + jnp.exp(-x))


def _silu(x):
    return x * _sigmoid(x)


def _in_layout(d_model):
    gates_end = COL_GATES + N_BRANCHES * d_model
    col_gq = gates_end
    col_gk = col_gq + GLA_K_WIDTH
    col_akv = col_gk + GLA_K_WIDTH
    col_lr = col_akv + 2 * KV_WIDTH
    used = col_lr + LANES
    total = -(-used // 1024) * 1024
    return dict(gq=col_gq, gk=col_gk, akv=col_akv, lr=col_lr, total=total)


def _mod_kernel(cond_ref, w_ref, b_ref, o_ref):
    c = _silu(cond_ref[...])
    o_ref[0] = _dot3(c, w_ref[0]) + b_ref[0]


def _modulation(cond, ada_w, ada_b):
    depth, d, n6 = ada_w.shape
    tn = 1536 if n6 % 1536 == 0 else n6
    rows = cond.shape[0]
    return pl.pallas_call(
        _mod_kernel,
        grid=(depth, n6 // tn),
        in_specs=[
            pl.BlockSpec((rows, d), lambda l, j: (0, 0)),
            pl.BlockSpec((1, d, tn), lambda l, j: (l, 0, j)),
            pl.BlockSpec((1, 1, tn), lambda l, j: (l, 0, j)),
        ],
        out_specs=pl.BlockSpec((1, rows, tn), lambda l, j: (l, 0, j)),
        out_shape=jax.ShapeDtypeStruct((depth, rows, n6), F32),
        compiler_params=_cparams(("parallel", "parallel")),
        name="adaln_mod",
    )(cond, ada_w, ada_b.reshape(depth, 1, n6))


def _norm_mod(x, g, shift, scale):
    ms = jnp.mean(x * x, axis=-1, keepdims=True)
    return x * lax.rsqrt(ms + EPS) * g * (1.0 + scale) + shift


def _inproj_kernel(x_ref, mod_ref, g_ref, w_ref, z_ref, h_scr, *, d):
    @pl.when(pl.program_id(1) == 0)
    def _():
        h = _norm_mod(x_ref[...], g_ref[...], mod_ref[0, :, 0:d], mod_ref[0, :, d:2 * d])
        h_scr[...] = h.astype(BF16)

    z_ref[...] = _dot(h_scr[...], w_ref[...])


def _in_projection(x, mod3, g, w, seg_len):
    t, d = x.shape
    n = w.shape[1]
    tm = min(1024, seg_len)
    tn = 1024
    tps = seg_len // tm
    return pl.pallas_call(
        functools.partial(_inproj_kernel, d=d),
        grid=(t // tm, n // tn),
        in_specs=[
            pl.BlockSpec((tm, d), lambda i, j: (i, 0)),
            pl.BlockSpec((1, 1, 6 * d), lambda i, j: (i // tps, 0, 0)),
            pl.BlockSpec((1, d), lambda i, j: (0, 0)),
            pl.BlockSpec((d, tn), lambda i, j: (0, j)),
        ],
        out_specs=pl.BlockSpec((tm, tn), lambda i, j: (i, j)),
        out_shape=jax.ShapeDtypeStruct((t, n), F32),
        scratch_shapes=[pltpu.VMEM((tm, d), BF16)],
        compiler_params=_cparams(("parallel", "arbitrary")),
        name="in_projection",
    )(x, mod3, g.reshape(1, d), w)


def _head_rms(x, g, ones_bd):
    hi, lo = _split2(x * x)
    ss = _dot(hi, ones_bd) + _dot(lo, ones_bd)
    return x * lax.rsqrt(ss * (1.0 / HEAD_DIM) + EPS) * g


def _rope(x, cos, sin_signed):
    width = x.shape[1]
    reps = width // LANES
    if reps > 1:
        cos = jnp.concatenate([cos] * reps, axis=1)
        sin_signed = jnp.concatenate([sin_signed] * reps, axis=1)
    lane = lax.broadcasted_iota(jnp.int32, x.shape, 1)
    first_half = (lane % (2 * ROPE_FREQS)) < ROPE_FREQS
    partner = jnp.where(first_half, pltpu.roll(x, width - ROPE_FREQS, 1), pltpu.roll(x, ROPE_FREQS, 1))
    return x * cos + partner * sin_signed


def _dup_heads(x):
    lane = lax.broadcasted_iota(jnp.int32, x.shape, 1)
    swapped = pltpu.roll(x, HEAD_DIM, 1)
    low = lane < HEAD_DIM
    return jnp.concatenate([jnp.where(low, x, swapped), jnp.where(low, swapped, x)], axis=1)


def _attn_prep_kernel(zq_ref, zkv_ref, cos_ref, sin_ref, qg_ref, kg_ref, bd_ref, q_ref, kout_ref, k2_ref, v2_ref):
    cos = cos_ref[0]
    sin = sin_ref[0]
    bd = bd_ref[...]
    q = _head_rms(zq_ref[...], qg_ref[...], bd)
    q_ref[...] = (_rope(q, cos, sin) * (HEAD_DIM ** -0.5)).astype(BF16)
    kv = zkv_ref[...]
    k = _head_rms(kv[:, :KV_WIDTH], kg_ref[...], bd[:KV_WIDTH, :KV_WIDTH])
    kout_ref[...] = k
    k2_ref[...] = _dup_heads(_rope(k, cos, sin)).astype(BF16)
    v2_ref[...] = _dup_heads(kv[:, KV_WIDTH:]).astype(BF16)


def _attention_prep(z, lay, cos_tab, sin_tab, q_g, k_g, ones_bd, seg_len):
    t = z.shape[0]
    tm = min(512, seg_len)
    tps = seg_len // tm
    akv_blk = lay["akv"] // (2 * KV_WIDTH)
    tab_spec = pl.BlockSpec((1, tm, LANES), lambda i: (jnp.minimum(i // tps, 1), i % tps, 0))
    return pl.pallas_call(
        _attn_prep_kernel,
        grid=(t // tm,),
        in_specs=[
            pl.BlockSpec((tm, ATT_WIDTH), lambda i: (i, COL_AQ // ATT_WIDTH)),
            pl.BlockSpec((tm, 2 * KV_WIDTH), lambda i: (i, akv_blk)),
            tab_spec,
            tab_spec,
            pl.BlockSpec((1, ATT_WIDTH), lambda i: (0, 0)),
            pl.BlockSpec((1, KV_WIDTH), lambda i: (0, 0)),
            pl.BlockSpec((ATT_WIDTH, ATT_WIDTH), lambda i: (0, 0)),
        ],
        out_specs=[
            pl.BlockSpec((tm, ATT_WIDTH), lambda i: (i, 0)),
            pl.BlockSpec((tm, KV_WIDTH), lambda i: (i, 0)),
            pl.BlockSpec((tm, 2 * KV_WIDTH), lambda i: (i, 0)),
            pl.BlockSpec((tm, 2 * KV_WIDTH), lambda i: (i, 0)),
        ],
        out_shape=[
            jax.ShapeDtypeStruct((t, ATT_WIDTH), BF16),
            jax.ShapeDtypeStruct((t, KV_WIDTH), F32),
            jax.ShapeDtypeStruct((t, 2 * KV_WIDTH), BF16),
            jax.ShapeDtypeStruct((t, 2 * KV_WIDTH), BF16),
        ],
        compiler_params=_cparams(("parallel",)),
        name="attention_prep",
    )(z, z, cos_tab, sin_tab, q_g, k_g, ones_bd)


def _attn_kernel(*refs, has_ctx):
    if has_ctx:
        q_ref, k_ref, v_ref, ck_ref, cv_ref, o_ref = refs
    else:
        q_ref, k_ref, v_ref, o_ref = refs
    tq = q_ref.shape[0]
    k = k_ref[...]
    v = v_ref[...]
    low = lax.broadcasted_iota(jnp.int32, (tq, LANES), 1) < HEAD_DIM
    zero = jnp.zeros((tq, LANES), BF16)
    outs = []
    for p in range(2):
        q2 = q_ref[:, p * LANES:(p + 1) * LANES]
        lhs = jnp.concatenate([jnp.where(low, q2, zero), jnp.where(low, zero, q2)], axis=0)
        s = _dot_nt(lhs, k)
        m = jnp.max(s, axis=-1, keepdims=True)
        if has_ctx:
            sc = _dot_nt(lhs, ck_ref[0])
            m = jnp.maximum(m, jnp.max(sc, axis=-1, keepdims=True))
        e = jnp.exp(s - m)
        den = jnp.sum(e, axis=-1, keepdims=True)
        o = _dot(e.astype(BF16), v)
        if has_ctx:
            ec = jnp.exp(sc - m)
            den = den + jnp.sum(ec, axis=-1, keepdims=True)
            o = o + _dot(ec.astype(BF16), cv_ref[0])
        o = o / den
        outs.append(jnp.where(low, o[:tq], o[tq:]))
    o_ref[...] = jnp.concatenate(outs, axis=1).astype(BF16)


def _attention(q, k2, v2, ctx_k2, ctx_v2, *, row0, n_seq, seq_len, tq):
    t = q.shape[0]
    ntq = seq_len // tq
    gw = ATT_WIDTH // ATT_KV_HEADS
    qblk0 = row0 // tq
    kblk0 = row0 // seq_len
    has_ctx = ctx_k2 is not None
    in_specs = [
        pl.BlockSpec((tq, gw), lambda b, g, i: (qblk0 + b * ntq + i, g)),
        pl.BlockSpec((seq_len, LANES), lambda b, g, i: (kblk0 + b, g)),
        pl.BlockSpec((seq_len, LANES), lambda b, g, i: (kblk0 + b, g)),
    ]
    args = [q, k2, v2]
    if has_ctx:
        past = ctx_k2.shape[1]
        in_specs += [pl.BlockSpec((1, past, LANES), lambda b, g, i: (b, 0, g))] * 2
        args += [ctx_k2, ctx_v2]
    return pl.pallas_call(
        functools.partial(_attn_kernel, has_ctx=has_ctx),
        grid=(n_seq, ATT_KV_HEADS, ntq),
        in_specs=in_specs,
        out_specs=pl.BlockSpec((tq, gw), lambda b, g, i: (b * ntq + i, g)),
        out_shape=jax.ShapeDtypeStruct((n_seq * seq_len, ATT_WIDTH), BF16),
        compiler_params=_cparams(("parallel", "parallel", "parallel")),
        name="attention_ctx" if has_ctx else "attention_self",
    )(*args)


def _gla_kernel(q_ref, k_ref, v_ref, lr_ref, aup_ref, ab_ref, s0_ref, o_ref, sfin_ref,
                s_scr, la_scr, vt_scr, ot_scr, *, chunks_per_seq):
    seg = pl.program_id(0)
    direction = pl.program_id(1)
    n = pl.program_id(2)
    c = q_ref.shape[0]
    kw = GLA_K_WIDTH
    fwd = direction == 0

    @pl.when(n % chunks_per_seq == 0)
    def _():
        s_scr[...] = s0_ref[0, 0]

    x = _dot3(lr_ref[...], aup_ref[0]) + ab_ref[0]
    la = (jnp.minimum(x, 0.0) - jnp.log(1.0 + jnp.exp(-jnp.abs(x)))) * (1.0 / GLA_TAU)

    row = lax.broadcasted_iota(jnp.int32, (c, c), 0)
    col = lax.broadcasted_iota(jnp.int32, (c, c), 1)
    tri = jnp.where(fwd, row - col, col - row) >= 0
    tri_b = tri.astype(BF16)
    la_hi, la_mid, la_lo = _split3(la)
    cum = _dot(tri_b, la_hi) + _dot(tri_b, la_mid) + _dot(tri_b, la_lo)
    bend = jnp.sum(la, axis=0, keepdims=True)
    safe = jnp.min(cum) >= -GLA_SAFE_LOG
    head_of_lane = lax.broadcasted_iota(jnp.int32, (1, kw), 1) // GLA_DK

    @pl.when(safe)
    def _():
        q = q_ref[...] * (GLA_DK ** -0.5)
        k = k_ref[...]
        vb = v_ref[...].astype(BF16)
        qt = (q * jnp.exp(cum)).astype(BF16)
        kt = (k * jnp.exp(-cum)).astype(BF16)
        ke = (k * jnp.exp(bend - cum)).astype(BF16)
        s_old = s_scr[...]
        s_b = s_old.astype(BF16)
        s_new = jnp.exp(bend) * s_old
        zero = jnp.zeros_like(qt)
        for h in range(GLA_HEADS):
            hm = head_of_lane == h
            qh = jnp.where(hm, qt, zero)
            a = jnp.where(tri, _dot_nt(qh, kt), 0.0).astype(BF16)
            vh = vb[:, h * GLA_DV:(h + 1) * GLA_DV]
            o_ref[0, :, h * GLA_DV:(h + 1) * GLA_DV] = _dot(a, vh) + _dot_nt(qh, s_b)
            s_new = s_new + _dot_tn(vh, jnp.where(hm, ke, zero))
        s_scr[...] = s_new
        sfin_ref[0, 0, 0] = s_new

    @pl.when(jnp.logical_not(safe))
    def _():
        la_scr[...] = la
        vt_scr[...] = v_ref[...].T
        ot_scr[...] = jnp.zeros_like(ot_scr)
        lane_c = lax.broadcasted_iota(jnp.int32, (1, c), 1)

        def body(i, s):
            t = jnp.where(fwd, i, c - 1 - i)
            a_t = jnp.exp(la_scr[pl.ds(t, 1), :])
            k_t = k_ref[pl.ds(t, 1), :]
            q_t = q_ref[pl.ds(t, 1), :] * (GLA_DK ** -0.5)
            onehot = lane_c == t
            vmat = jnp.zeros((GLA_DV, kw), F32)
            for h in range(GLA_HEADS):
                vcol = jnp.sum(jnp.where(onehot, vt_scr[h * GLA_DV:(h + 1) * GLA_DV, :], 0.0), axis=1, keepdims=True)
                vmat = jnp.where(head_of_lane == h, vcol, vmat)
            s = a_t * s + vmat * k_t
            prod = s * q_t
            for h in range(GLA_HEADS):
                ocol = jnp.sum(jnp.where(head_of_lane == h, prod, 0.0), axis=1, keepdims=True)
                rows = slice(h * GLA_DV, (h + 1) * GLA_DV)
                ot_scr[rows, :] = jnp.where(onehot, ocol, ot_scr[rows, :])
            return s

        s_new = lax.fori_loop(0, c, body, s_scr[...])
        o_ref[0] = ot_scr[...].T
        s_scr[...] = s_new
        sfin_ref[0, 0, 0] = s_new


def _gla(z, lay, aup_pad, ab, s0, *, seg_len, n_seg, prompt_len):
    t = z.shape[0]
    c = GLA_CHUNK
    nc = seg_len // c
    cps_prompt = prompt_len // c
    gq_blk = lay["gq"] // GLA_K_WIDTH
    gk_blk = lay["gk"] // GLA_K_WIDTH
    gv_blk = COL_GV // GLA_V_WIDTH
    lr_blk = lay["lr"] // LANES

    def chunk(seg, d, n):
        return seg * nc + jnp.where(d == 0, n, nc - 1 - n)

    def kernel(*refs):
        cps = jnp.where(pl.program_id(0) == 0, cps_prompt, nc)
        _gla_kernel(*refs, chunks_per_seq=cps)

    return pl.pallas_call(
        kernel,
        grid=(n_seg, 2, nc),
        in_specs=[
            pl.BlockSpec((c, GLA_K_WIDTH), lambda s, d, n: (chunk(s, d, n), gq_blk)),
            pl.BlockSpec((c, GLA_K_WIDTH), lambda s, d, n: (chunk(s, d, n), gk_blk)),
            pl.BlockSpec((c, GLA_V_WIDTH), lambda s, d, n: (chunk(s, d, n), gv_blk)),
            pl.BlockSpec((c, LANES), lambda s, d, n: (chunk(s, d, n), lr_blk)),
            pl.BlockSpec((1, LANES, GLA_K_WIDTH), lambda s, d, n: (d, 0, 0)),
            pl.BlockSpec((1, 1, GLA_K_WIDTH), lambda s, d, n: (d, 0, 0)),
            pl.BlockSpec((1, 1, GLA_DV, GLA_K_WIDTH), lambda s, d, n: (s, d, 0, 0)),
        ],
        out_specs=[
            pl.BlockSpec((1, c, GLA_V_WIDTH), lambda s, d, n: (d, chunk(s, d, n), 0)),
            pl.BlockSpec((1, 1, 1, GLA_DV, GLA_K_WIDTH), lambda s, d, n: (s, d, jnp.where(d == 0, n, nc - 1 - n), 0, 0)),
        ],
        out_shape=[
            jax.ShapeDtypeStruct((2, t, GLA_V_WIDTH), F32),
            jax.ShapeDtypeStruct((n_seg, 2, nc, GLA_DV, GLA_K_WIDTH), F32),
        ],
        scratch_shapes=[
            pltpu.VMEM((GLA_DV, GLA_K_WIDTH), F32),
            pltpu.VMEM((c, GLA_K_WIDTH), F32),
            pltpu.VMEM((GLA_V_WIDTH, c), F32),
            pltpu.VMEM((GLA_V_WIDTH, c), F32),
        ],
        compiler_params=_cparams(("parallel", "parallel", "arbitrary")),
        name="gla_scan",
    )(z, z, z, z, aup_pad, ab, s0)


def _four_chan_kernel(u_ref, w_ref, o_ref):
    o_ref[...] = _dot(u_ref[...].astype(BF16), w_ref[...]).astype(BF16)


def _fourier_channels(z, w_chan):
    t = z.shape[0]
    tm = 512
    return pl.pallas_call(
        _four_chan_kernel,
        grid=(t // tm,),
        in_specs=[
            pl.BlockSpec((tm, FOURIER_WIDTH), lambda i: (i, COL_FU // FOURIER_WIDTH)),
            _resident((FOURIER_WIDTH, 2 * FOURIER_WIDTH), lambda i: (0, 0)),
        ],
        out_specs=pl.BlockSpec((tm, 2 * FOURIER_WIDTH), lambda i: (i, 0)),
        out_shape=jax.ShapeDtypeStruct((t, 2 * FOURIER_WIDTH), BF16),
        compiler_params=_cparams(("parallel",)),
        name="fourier_channels",
    )(z, w_chan)


def _four_pos_kernel(c_ref, s_ref, xc_ref, xs_ref, o_ref):
    o_ref[...] = (_dot(c_ref[...], xc_ref[...]) + _dot(s_ref[...], xs_ref[...])).astype(BF16)


def _fourier_positions(xcs, cos_m, nsin_m, *, row0, n_seq, seq_len):
    tm = min(512, seq_len)
    nt = seq_len // tm
    blk0 = row0 // seq_len
    return pl.pallas_call(
        _four_pos_kernel,
        grid=(n_seq, nt),
        in_specs=[
            pl.BlockSpec((tm, seq_len), lambda b, i: (i, 0)),
            pl.BlockSpec((tm, seq_len), lambda b, i: (i, 0)),
            pl.BlockSpec((seq_len, FOURIER_WIDTH), lambda b, i: (blk0 + b, 0)),
            pl.BlockSpec((seq_len, FOURIER_WIDTH), lambda b, i: (blk0 + b, 1)),
        ],
        out_specs=pl.BlockSpec((tm, FOURIER_WIDTH), lambda b, i: (b * nt + i, 0)),
        out_shape=jax.ShapeDtypeStruct((n_seq * seq_len, FOURIER_WIDTH), BF16),
        compiler_params=_cparams(("parallel", "parallel")),
        name="fourier_positions",
    )(cos_m, nsin_m, xcs, xcs)


def _merge_kernel(att_ref, gof_ref, gob_ref, gr_ref, four_ref, g0_ref, g1_ref, g2_ref, x_ref, mod_ref,
                  gg_ref, wb_ref, wo_ref, o_ref, *, d):
    o = gof_ref[0] + gob_ref[0]
    gg = gg_ref[...]
    parts = []
    for h in range(GLA_HEADS):
        oh = o[:, h * GLA_DV:(h + 1) * GLA_DV]
        ms = jnp.mean(oh * oh, axis=-1, keepdims=True)
        parts.append(oh * lax.rsqrt(ms + EPS) * gg)
    gla = (jnp.concatenate(parts, axis=1) * _silu(gr_ref[...])).astype(BF16)
    merged = (_sigmoid(g0_ref[...]) * _dot(att_ref[...], wb_ref[0])
              + _sigmoid(g1_ref[...]) * _dot(gla, wb_ref[1])
              + _sigmoid(g2_ref[...]) * _dot(four_ref[...], wb_ref[2]))
    mix = _dot(merged.astype(BF16), wo_ref[...])
    o_ref[...] = x_ref[...] + mod_ref[0, :, 2 * d:3 * d] * mix


def _merge(att, gla_o, z, four, x, mod3, gla_g, w_branch, w_out, seg_len):
    t, d = x.shape
    tm = min(512, seg_len)
    tps = seg_len // tm
    bw = att.shape[1]
    gate_blk = COL_GATES // d
    return pl.pallas_call(
        functools.partial(_merge_kernel, d=d),
        grid=(t // tm,),
        in_specs=[
            pl.BlockSpec((tm, bw), lambda i: (i, 0)),
            pl.BlockSpec((1, tm, GLA_V_WIDTH), lambda i: (0, i, 0)),
            pl.BlockSpec((1, tm, GLA_V_WIDTH), lambda i: (1, i, 0)),
            pl.BlockSpec((tm, GLA_V_WIDTH), lambda i: (i, COL_GR // GLA_V_WIDTH)),
            pl.BlockSpec((tm, bw), lambda i: (i, 0)),
            pl.BlockSpec((tm, d), lambda i: (i, gate_blk)),
            pl.BlockSpec((tm, d), lambda i: (i, gate_blk + 1)),
            pl.BlockSpec((tm, d), lambda i: (i, gate_blk + 2)),
            pl.BlockSpec((tm, d), lambda i: (i, 0)),
            pl.BlockSpec((1, 1, 6 * d), lambda i: (i // tps, 0, 0)),
            pl.BlockSpec((1, GLA_DV), lambda i: (0, 0)),
            _resident((N_BRANCHES, bw, d), lambda i: (0, 0, 0)),
            _resident((d, d), lambda i: (0, 0)),
        ],
        out_specs=pl.BlockSpec((tm, d), lambda i: (i, 0)),
        out_shape=jax.ShapeDtypeStruct((t, d), F32),
        compiler_params=_cparams(("parallel",)),
        name="branch_merge",
    )(att, gla_o, gla_o, z, four, z, z, z, x, mod3, gla_g.reshape(1, GLA_DV), w_branch, w_out)


def _ffn_kernel(x_ref, mod_ref, g_ref, w1_ref, w3_ref, w2_ref, o_ref, *, d):
    x = x_ref[...]
    h = _norm_mod(x, g_ref[...], mod_ref[0, :, 3 * d:4 * d], mod_ref[0, :, 4 * d:5 * d]).astype(BF16)
    a = _dot(h, w1_ref[...])
    u = (_silu(a) * _dot(h, w3_ref[...])).astype(BF16)
    o_ref[...] = x + mod_ref[0, :, 5 * d:6 * d] * _dot(u, w2_ref[...])


def _dense_ffn(x, mod3, g, w1, w3, w2, seg_len):
    t, d = x.shape
    dff = w1.shape[1]
    tm = min(512, seg_len)
    tps = seg_len // tm
    return pl.pallas_call(
        functools.partial(_ffn_kernel, d=d),
        grid=(t // tm,),
        in_specs=[
            pl.BlockSpec((tm, d), lambda i: (i, 0)),
            pl.BlockSpec((1, 1, 6 * d), lambda i: (i // tps, 0, 0)),
            pl.BlockSpec((1, d), lambda i: (0, 0)),
            _resident((d, dff), lambda i: (0, 0)),
            _resident((d, dff), lambda i: (0, 0)),
            _resident((dff, d), lambda i: (0, 0)),
        ],
        out_specs=pl.BlockSpec((tm, d), lambda i: (i, 0)),
        out_shape=jax.ShapeDtypeStruct((t, d), F32),
        compiler_params=_cparams(("parallel",)),
        name="dense_ffn",
    )(x, mod3, g.reshape(1, d), w1, w3, w2)


def _router_weights(h, rw):
    logits = _dot3(h, rw)
    lane = lax.broadcasted_iota(jnp.int32, logits.shape, 1).astype(F32)
    neg = jnp.float32(-jnp.inf)
    l1 = jnp.where(lane < N_EXPERTS, logits, neg)
    m1 = jnp.max(l1, axis=-1, keepdims=True)
    i1 = jnp.min(jnp.where(l1 == m1, lane, float(LANES)), axis=-1, keepdims=True)
    l2 = jnp.where(lane == i1, neg, l1)
    m2 = jnp.max(l2, axis=-1, keepdims=True)
    i2 = jnp.min(jnp.where(l2 == m2, lane, float(LANES)), axis=-1, keepdims=True)
    e2 = jnp.exp(m2 - m1)
    w1 = 1.0 / (1.0 + e2)
    return jnp.where(lane == i1, w1, 0.0) + jnp.where(lane == i2, e2 * w1, 0.0)


def _moe_kernel(x_ref, mod_ref, g_ref, rw_ref, w1_ref, w3_ref, w2_ref, o_ref, h_scr, comb_scr, acc_scr, *, d):
    e = pl.program_id(1)
    f = pl.program_id(2)

    @pl.when((e == 0) & (f == 0))
    def _():
        h = _norm_mod(x_ref[...], g_ref[...], mod_ref[0, :, 3 * d:4 * d], mod_ref[0, :, 4 * d:5 * d])
        h_scr[...] = h.astype(BF16)
        comb_scr[...] = _router_weights(h, rw_ref[...])
        acc_scr[...] = jnp.zeros_like(acc_scr)

    h = h_scr[...]
    comb = comb_scr[...]
    lane = lax.broadcasted_iota(jnp.int32, comb.shape, 1)
    ce = jnp.sum(jnp.where(lane == e, comb, 0.0), axis=-1, keepdims=True)
    a = _dot(h, w1_ref[0])
    u = (_silu(a) * _dot(h, w3_ref[0]) * ce).astype(BF16)
    acc_scr[...] += _dot(u, w2_ref[0])

    @pl.when((e == pl.num_programs(1) - 1) & (f == pl.num_programs(2) - 1))
    def _():
        o_ref[...] = x_ref[...] + mod_ref[0, :, 5 * d:6 * d] * acc_scr[...]


def _moe_ffn(x, mod3, g, rw_pad, w1, w3, w2, seg_len):
    t, d = x.shape
    n_e, _, dff = w1.shape
    tm = min(512, seg_len)
    tps = seg_len // tm
    tf = dff // 2 if (dff // 2) % LANES == 0 else dff
    return pl.pallas_call(
        functools.partial(_moe_kernel, d=d),
        grid=(t // tm, n_e, dff // tf),
        in_specs=[
            pl.BlockSpec((tm, d), lambda i, e, f: (i, 0)),
            pl.BlockSpec((1, 1, 6 * d), lambda i, e, f: (i // tps, 0, 0)),
            pl.BlockSpec((1, d), lambda i, e, f: (0, 0)),
            pl.BlockSpec((d, LANES), lambda i, e, f: (0, 0)),
            pl.BlockSpec((1, d, tf), lambda i, e, f: (e, 0, f)),
            pl.BlockSpec((1, d, tf), lambda i, e, f: (e, 0, f)),
            pl.BlockSpec((1, tf, d), lambda i, e, f: (e, f, 0)),
        ],
        out_specs=pl.BlockSpec((tm, d), lambda i, e, f: (i, 0)),
        out_shape=jax.ShapeDtypeStruct((t, d), F32),
        scratch_shapes=[pltpu.VMEM((tm, d), BF16), pltpu.VMEM((tm, LANES), F32), pltpu.VMEM((tm, d), F32)],
        compiler_params=_cparams(("parallel", "arbitrary", "arbitrary")),
        name="moe_ffn",
    )(x, mod3, g.reshape(1, d), rw_pad, w1, w3, w2)


def _final_norm_kernel(x_ref, g_ref, o_ref):
    x = x_ref[...]
    ms = jnp.mean(x * x, axis=-1, keepdims=True)
    o_ref[...] = x * lax.rsqrt(ms + EPS) * g_ref[...]


def _final_norm(x, g):
    t, d = x.shape
    tm = 512
    return pl.pallas_call(
        _final_norm_kernel,
        grid=(t // tm,),
        in_specs=[pl.BlockSpec((tm, d), lambda i: (i, 0)), pl.BlockSpec((1, d), lambda i: (0, 0))],
        out_specs=pl.BlockSpec((tm, d), lambda i: (i, 0)),
        out_shape=jax.ShapeDtypeStruct((t, d), F32),
        compiler_params=_cparams(("parallel",)),
        name="final_norm",
    )(x, g.reshape(1, d))


def _rope_tables(n_tokens):
    pos = jnp.arange(n_tokens, dtype=jnp.int32)
    rowcol = jnp.stack([pos // GRID_W, pos % GRID_W], axis=-1).astype(F32)
    inv_freq = 1.0 / (ROPE_THETA ** (jnp.arange(ROPE_FREQS, dtype=F32) / ROPE_FREQS))
    ang = rowcol[:, :, None] * inv_freq
    cos = jnp.cos(ang)
    sin = jnp.sin(ang)
    cos_h = jnp.stack([cos, cos], axis=2).reshape(n_tokens, HEAD_DIM)
    sin_h = jnp.stack([-sin, sin], axis=2).reshape(n_tokens, HEAD_DIM)
    cos_t = jnp.tile(cos_h, (1, LANES // HEAD_DIM))
    sin_t = jnp.tile(sin_h, (1, LANES // HEAD_DIM))
    return (jnp.stack([jnp.ones_like(cos_t), cos_t]), jnp.stack([jnp.zeros_like(sin_t), sin_t]))


def _dft_matrices(n):
    idx = jnp.arange(n, dtype=jnp.int32)
    ang = ((idx[:, None] * idx[None, :]) % n).astype(F32) * (2.0 * math.pi / n)
    scale = n ** -0.5
    return (jnp.cos(ang) * scale).astype(BF16), (jnp.sin(ang) * -scale).astype(BF16)


def _channel_dft_weight():
    n = FOURIER_GROUP_DIM
    idx = jnp.arange(n, dtype=jnp.int32)
    ang = ((idx[:, None] * idx[None, :]) % n).astype(F32) * (2.0 * math.pi / n)
    eye = jnp.eye(FOURIER_GROUPS, dtype=F32)
    cos_bd = jnp.kron(eye, jnp.cos(ang) * n ** -0.5)
    sin_bd = jnp.kron(eye, jnp.sin(ang) * n ** -0.5)
    return jnp.concatenate([cos_bd, sin_bd], axis=1).astype(BF16)


def _repack_w_in(w_in, lay):
    depth, d, _ = w_in.shape
    widths = (ATT_WIDTH, KV_WIDTH, KV_WIDTH, GLA_K_WIDTH, GLA_K_WIDTH, GLA_V_WIDTH, GLA_V_WIDTH,
              GLA_LOW_RANK, GLA_LOW_RANK, FOURIER_WIDTH, N_BRANCHES * d)
    offs = [0]
    for w in widths:
        offs.append(offs[-1] + w)
    aq, ak, av, gq, gk, gv, gr, lrf, lrb, fu, gates = [w_in[:, :, offs[i]:offs[i + 1]] for i in range(len(widths))]
    used = lay["lr"] + 2 * GLA_LOW_RANK
    pad = jnp.zeros((depth, d, lay["total"] - used), w_in.dtype)
    return jnp.concatenate([aq, gv, gr, fu, gates, gq, gk, ak, av, lrf, lrb, pad], axis=-1).astype(BF16)


def kernel(x_prompt, x_sample, cache_k, cache_v, state_gla, c, c_ctx, ada_w, ada_b, norm1_g, norm2_g, w_in, q_norm_g, k_norm_g, alpha_up, alpha_b, gla_norm_g, w_branch, w_out, ffn_w1, ffn_w3, ffn_w2, router_w, moe_w1, moe_w3, moe_w2, final_g):
    batch, seq, d = x_prompt.shape
    dec_batch, dec_seq, _ = x_sample.shape
    depth = w_in.shape[0]
    past = cache_k.shape[2]
    seg_len = batch * seq
    assert seg_len == dec_seq, "context tokens and each latent request must fill equal segments"
    assert seq % GLA_CHUNK == 0 and dec_seq % GLA_CHUNK == 0
    assert COL_GATES % d == 0
    n_seg = 1 + dec_batch
    t = n_seg * seg_len
    lay = _in_layout(d)

    w_in_p = _repack_w_in(w_in, lay)
    w_branch_b = w_branch.astype(BF16)
    w_out_b = w_out.astype(BF16)
    ffn_w1_b, ffn_w3_b, ffn_w2_b = ffn_w1.astype(BF16), ffn_w3.astype(BF16), ffn_w2.astype(BF16)
    moe_w1_b, moe_w3_b, moe_w2_b = moe_w1.astype(BF16), moe_w3.astype(BF16), moe_w2.astype(BF16)
    router_pad = jnp.pad(router_w, ((0, 0), (0, 0), (0, LANES - N_EXPERTS)))
    cos_tab, sin_tab = _rope_tables(dec_seq)
    dft_cos_s, dft_nsin_s = _dft_matrices(dec_seq)
    dft_cos_p, dft_nsin_p = _dft_matrices(seq)
    w_chan = _channel_dft_weight()
    ones_bd = jnp.kron(jnp.eye(ATT_HEADS, dtype=F32), jnp.ones((HEAD_DIM, HEAD_DIM), F32)).astype(BF16)
    q_g = jnp.tile(q_norm_g, (1, ATT_HEADS)).reshape(depth, 1, ATT_WIDTH)
    k_g = jnp.tile(k_norm_g, (1, ATT_KV_HEADS)).reshape(depth, 1, KV_WIDTH)
    aup_pad = jnp.zeros((depth, 2, LANES, GLA_K_WIDTH), F32)
    aup_pad = aup_pad.at[:, 0, :GLA_LOW_RANK].set(alpha_up[:, 0])
    aup_pad = aup_pad.at[:, 1, GLA_LOW_RANK:2 * GLA_LOW_RANK].set(alpha_up[:, 1])
    ab = alpha_b.reshape(depth, 2, 1, GLA_K_WIDTH)
    s0_lat = state_gla.transpose(0, 1, 2, 5, 3, 4).reshape(dec_batch, depth, 2, GLA_DV, GLA_K_WIDTH)
    s0_all = jnp.concatenate([jnp.zeros((1,) + s0_lat.shape[1:], F32), s0_lat], axis=0)

    def dup_cache(x):
        xb = x.astype(BF16)[:, :, :, None, :]
        return jnp.broadcast_to(xb, (dec_batch, past, ATT_KV_HEADS, 2, HEAD_DIM)).reshape(dec_batch, past, 2 * KV_WIDTH)

    cond = jnp.zeros((8, d), F32).at[0].set(c_ctx).at[1:1 + dec_batch].set(c)
    mod_all = _modulation(cond, ada_w, ada_b)

    x = jnp.concatenate([x_prompt.reshape(seg_len, d), x_sample.reshape(dec_batch * dec_seq, d)], axis=0)
    new_k, new_v, new_s = [], [], []
    for l in range(depth):
        mod3 = mod_all[l, :n_seg].reshape(n_seg, 1, 6 * d)
        z = _in_projection(x, mod3, norm1_g[l], w_in_p[l], seg_len)
        qn, k_out, k2, v2 = _attention_prep(z, lay, cos_tab, sin_tab, q_g[l], k_g[l], ones_bd, seg_len)
        att_p = _attention(qn, k2, v2, None, None, row0=0, n_seq=batch, seq_len=seq, tq=seq)
        att_s = _attention(qn, k2, v2, dup_cache(cache_k[:, l]), dup_cache(cache_v[:, l]),
                           row0=seg_len, n_seq=dec_batch, seq_len=dec_seq, tq=min(128, dec_seq))
        att = jnp.concatenate([att_p, att_s], axis=0)
        gla_o, s_fin = _gla(z, lay, aup_pad[l], ab[l], s0_all[:, l], seg_len=seg_len, n_seg=n_seg, prompt_len=seq)
        xcs = _fourier_channels(z, w_chan)
        four_p = _fourier_positions(xcs, dft_cos_p, dft_nsin_p, row0=0, n_seq=batch, seq_len=seq)
        four_s = _fourier_positions(xcs, dft_cos_s, dft_nsin_s, row0=seg_len, n_seq=dec_batch, seq_len=dec_seq)
        four = jnp.concatenate([four_p, four_s], axis=0)
        x = _merge(att, gla_o, z, four, x, mod3, gla_norm_g[l], w_branch_b[l], w_out_b[l], seg_len)
        j = l // 2
        if l % 2 == 0:
            x = _dense_ffn(x, mod3, norm2_g[l], ffn_w1_b[j], ffn_w3_b[j], ffn_w2_b[j], seg_len)
        else:
            x = _moe_ffn(x, mod3, norm2_g[l], router_pad[j], moe_w1_b[j], moe_w3_b[j], moe_w2_b[j], seg_len)
        new_k.append(k_out[:seg_len].reshape(batch, seq, ATT_KV_HEADS, HEAD_DIM))
        v_col = lay["akv"] + KV_WIDTH
        new_v.append(z[:seg_len, v_col:v_col + KV_WIDTH].reshape(batch, seq, ATT_KV_HEADS, HEAD_DIM))
        cps = seq // GLA_CHUNK
        sf = s_fin[0, 0, cps - 1::cps]
        sb = s_fin[0, 1, 0::cps]
        st = jnp.stack([sf, sb], axis=1).reshape(batch, 2, GLA_DV, GLA_HEADS, GLA_DK)
        new_s.append(st.transpose(0, 1, 3, 4, 2))
    y = _final_norm(x, final_g)
    y_prompt = y[:seg_len].reshape(batch, seq, d)
    y_sample = y[seg_len:].reshape(dec_batch, dec_seq, d)
    return (y_prompt, y_sample, jnp.stack(new_k, axis=1), jnp.stack(new_v, axis=1), jnp.stack(new_s, axis=1))
```

```python
import functools
import math

import jax
import jax.numpy as jnp
from jax import lax
from jax.experimental import pallas as pl
from jax.experimental.pallas import tpu as pltpu

F32 = jnp.float32
BF16 = jnp.bfloat16

EPS = 1e-6
ATT_HEADS = 8
ATT_KV_HEADS = 2
HEAD_DIM = 64
ATT_WIDTH = ATT_HEADS * HEAD_DIM
KV_WIDTH = ATT_KV_HEADS * HEAD_DIM
ROPE_FREQS = HEAD_DIM // 4
ROPE_THETA = 10000.0
GRID_W = 64
GLA_HEADS = 4
GLA_DK = 64
GLA_DV = 128
GLA_K_WIDTH = GLA_HEADS * GLA_DK
GLA_V_WIDTH = GLA_HEADS * GLA_DV
GLA_LOW_RANK = 16
GLA_TAU = 16.0
FOURIER_GROUPS = 4
FOURIER_GROUP_DIM = 128
FOURIER_WIDTH = FOURIER_GROUPS * FOURIER_GROUP_DIM
N_BRANCHES = 3
N_EXPERTS = 8

LANES = 128
VMEM_LIMIT_BYTES = 56 * 1024 * 1024

COL_AQ = 0
COL_GV = 512
COL_GR = 1024
COL_FU = 1536
COL_GATES = 2048

GLA_CHUNK = 256
GLA_SAFE_LOG = 60.0


def _cparams(sem):
    return pltpu.CompilerParams(dimension_semantics=sem, vmem_limit_bytes=VMEM_LIMIT_BYTES)


def _resident(shape, index_map):
    return pl.BlockSpec(shape, index_map, pipeline_mode=pl.Buffered(1))


def _split2(x):
    hi = x.astype(BF16)
    lo = (x - hi.astype(F32)).astype(BF16)
    return hi, lo


def _split3(x):
    hi = x.astype(BF16)
    r = x - hi.astype(F32)
    mid = r.astype(BF16)
    lo = (r - mid.astype(F32)).astype(BF16)
    return hi, mid, lo


def _dot(a, b):
    return jnp.dot(a, b, preferred_element_type=F32)


def _dot_nt(a, b):
    return lax.dot_general(a, b, (((1,), (1,)), ((), ())), preferred_element_type=F32)


def _dot_tn(a, b):
    return lax.dot_general(a, b, (((0,), (0,)), ((), ())), preferred_element_type=F32)


def _dot3(a, b):
    a_hi, a_lo = _split2(a)
    b_hi, b_lo = _split2(b)
    return _dot(a_hi, b_hi) + _dot(a_lo, b_hi) + _dot(a_hi, b_lo)


def _sigmoid(x):
    return 1.0 / (1.0 + jnp.exp(-x))


def _silu(x):
    return x * _sigmoid(x)


def _in_layout(d_model):
    gates_end = COL_GATES + N_BRANCHES * d_model
    col_gq = gates_end
    col_gk = col_gq + GLA_K_WIDTH
    col_akv = col_gk + GLA_K_WIDTH
    col_lr = col_akv + 2 * KV_WIDTH
    used = col_lr + LANES
    total = -(-used // 1024) * 1024
    return dict(gq=col_gq, gk=col_gk, akv=col_akv, lr=col_lr, total=total)


def _mod_kernel(cond_ref, w_ref, b_ref, o_ref):
    c = _silu(cond_ref[...])
    o_ref[0] = _dot3(c, w_ref[0]) + b_ref[0]


def _modulation(cond, ada_w, ada_b):
    depth, d, n6 = ada_w.shape
    tn = 1536 if n6 % 1536 == 0 else n6
    rows = cond.shape[0]
    return pl.pallas_call(
        _mod_kernel,
        grid=(depth, n6 // tn),
        in_specs=[
            pl.BlockSpec((rows, d), lambda l, j: (0, 0)),
            pl.BlockSpec((1, d, tn), lambda l, j: (l, 0, j)),
            pl.BlockSpec((1, 1, tn), lambda l, j: (l, 0, j)),
        ],
        out_specs=pl.BlockSpec((1, rows, tn), lambda l, j: (l, 0, j)),
        out_shape=jax.ShapeDtypeStruct((depth, rows, n6), F32),
        compiler_params=_cparams(("parallel", "parallel")),
        name="adaln_mod",
    )(cond, ada_w, ada_b.reshape(depth, 1, n6))


def _norm_mod(x, g, shift, scale):
    ms = jnp.mean(x * x, axis=-1, keepdims=True)
    return x * lax.rsqrt(ms + EPS) * g * (1.0 + scale) + shift


def _inproj_kernel(x_ref, mod_ref, g_ref, w_ref, z_ref, h_scr, *, d):
    @pl.when(pl.program_id(1) == 0)
    def _():
        h = _norm_mod(x_ref[...], g_ref[...], mod_ref[0, :, 0:d], mod_ref[0, :, d:2 * d])
        h_scr[...] = h.astype(BF16)

    z_ref[...] = _dot(h_scr[...], w_ref[...])


def _in_projection(x, mod3, g, w, seg_len):
    t, d = x.shape
    n = w.shape[1]
    tm = min(1024, seg_len)
    tn = 1024
    tps = seg_len // tm
    return pl.pallas_call(
        functools.partial(_inproj_kernel, d=d),
        grid=(t // tm, n // tn),
        in_specs=[
            pl.BlockSpec((tm, d), lambda i, j: (i, 0)),
            pl.BlockSpec((1, 1, 6 * d), lambda i, j: (i // tps, 0, 0)),
            pl.BlockSpec((1, d), lambda i, j: (0, 0)),
            pl.BlockSpec((d, tn), lambda i, j: (0, j)),
        ],
        out_specs=pl.BlockSpec((tm, tn), lambda i, j: (i, j)),
        out_shape=jax.ShapeDtypeStruct((t, n), F32),
        scratch_shapes=[pltpu.VMEM((tm, d), BF16)],
        compiler_params=_cparams(("parallel", "arbitrary")),
        name="in_projection",
    )(x, mod3, g.reshape(1, d), w)


def _head_rms(x, g, ones_bd):
    hi, lo = _split2(x * x)
    ss = _dot(hi, ones_bd) + _dot(lo, ones_bd)
    return x * lax.rsqrt(ss * (1.0 / HEAD_DIM) + EPS) * g


def _rope(x, cos, sin_signed):
    width = x.shape[1]
    reps = width // LANES
    if reps > 1:
        cos = jnp.concatenate([cos] * reps, axis=1)
        sin_signed = jnp.concatenate([sin_signed] * reps, axis=1)
    lane = lax.broadcasted_iota(jnp.int32, x.shape, 1)
    first_half = (lane % (2 * ROPE_FREQS)) < ROPE_FREQS
    partner = jnp.where(first_half, pltpu.roll(x, width - ROPE_FREQS, 1), pltpu.roll(x, ROPE_FREQS, 1))
    return x * cos + partner * sin_signed


def _dup_heads(x):
    lane = lax.broadcasted_iota(jnp.int32, x.shape, 1)
    swapped = pltpu.roll(x, HEAD_DIM, 1)
    low = lane < HEAD_DIM
    return jnp.concatenate([jnp.where(low, x, swapped), jnp.where(low, swapped, x)], axis=1)


def _attn_prep_kernel(zq_ref, zkv_ref, cos_ref, sin_ref, qg_ref, kg_ref, bd_ref, q_ref, kout_ref, k2_ref, v2_ref):
    cos = cos_ref[0]
    sin = sin_ref[0]
    bd = bd_ref[...]
    q = _head_rms(zq_ref[...], qg_ref[...], bd)
    q_ref[...] = (_rope(q, cos, sin) * (HEAD_DIM ** -0.5)).astype(BF16)
    kv = zkv_ref[...]
    k = _head_rms(kv[:, :KV_WIDTH], kg_ref[...], bd[:KV_WIDTH, :KV_WIDTH])
    kout_ref[...] = k
    k2_ref[...] = _dup_heads(_rope(k, cos, sin)).astype(BF16)
    v2_ref[...] = _dup_heads(kv[:, KV_WIDTH:]).astype(BF16)


def _attention_prep(z, lay, cos_tab, sin_tab, q_g, k_g, ones_bd, seg_len):
    t = z.shape[0]
    tm = min(512, seg_len)
    tps = seg_len // tm
    akv_blk = lay["akv"] // (2 * KV_WIDTH)
    tab_spec = pl.BlockSpec((1, tm, LANES), lambda i: (jnp.minimum(i // tps, 1), i % tps, 0))
    return pl.pallas_call(
        _attn_prep_kernel,
        grid=(t // tm,),
        in_specs=[
            pl.BlockSpec((tm, ATT_WIDTH), lambda i: (i, COL_AQ // ATT_WIDTH)),
            pl.BlockSpec((tm, 2 * KV_WIDTH), lambda i: (i, akv_blk)),
            tab_spec,
            tab_spec,
            pl.BlockSpec((1, ATT_WIDTH), lambda i: (0, 0)),
            pl.BlockSpec((1, KV_WIDTH), lambda i: (0, 0)),
            pl.BlockSpec((ATT_WIDTH, ATT_WIDTH), lambda i: (0, 0)),
        ],
        out_specs=[
            pl.BlockSpec((tm, ATT_WIDTH), lambda i: (i, 0)),
            pl.BlockSpec((tm, KV_WIDTH), lambda i: (i, 0)),
            pl.BlockSpec((tm, 2 * KV_WIDTH), lambda i: (i, 0)),
            pl.BlockSpec((tm, 2 * KV_WIDTH), lambda i: (i, 0)),
        ],
        out_shape=[
            jax.ShapeDtypeStruct((t, ATT_WIDTH), BF16),
            jax.ShapeDtypeStruct((t, KV_WIDTH), F32),
            jax.ShapeDtypeStruct((t, 2 * KV_WIDTH), BF16),
            jax.ShapeDtypeStruct((t, 2 * KV_WIDTH), BF16),
        ],
        compiler_params=_cparams(("parallel",)),
        name="attention_prep",
    )(z, z, cos_tab, sin_tab, q_g, k_g, ones_bd)


def _attn_kernel(*refs, has_ctx):
    if has_ctx:
        q_ref, k_ref, v_ref, ck_ref, cv_ref, o_ref = refs
    else:
        q_ref, k_ref, v_ref, o_ref = refs
    tq = q_ref.shape[0]
    k = k_ref[...]
    v = v_ref[...]
    low = lax.broadcasted_iota(jnp.int32, (tq, LANES), 1) < HEAD_DIM
    zero = jnp.zeros((tq, LANES), BF16)
    outs = []
    for p in range(2):
        q2 = q_ref[:, p * LANES:(p + 1) * LANES]
        lhs = jnp.concatenate([jnp.where(low, q2, zero), jnp.where(low, zero, q2)], axis=0)
        s = _dot_nt(lhs, k)
        m = jnp.max(s, axis=-1, keepdims=True)
        if has_ctx:
            sc = _dot_nt(lhs, ck_ref[0])
            m = jnp.maximum(m, jnp.max(sc, axis=-1, keepdims=True))
        e = jnp.exp(s - m)
        den = jnp.sum(e, axis=-1, keepdims=True)
        o = _dot(e.astype(BF16), v)
        if has_ctx:
            ec = jnp.exp(sc - m)
            den = den + jnp.sum(ec, axis=-1, keepdims=True)
            o = o + _dot(ec.astype(BF16), cv_ref[0])
        o = o / den
        outs.append(jnp.where(low, o[:tq], o[tq:]))
    o_ref[...] = jnp.concatenate(outs, axis=1).astype(BF16)


def _attention(q, k2, v2, ctx_k2, ctx_v2, *, row0, n_seq, seq_len, tq):
    t = q.shape[0]
    ntq = seq_len // tq
    gw = ATT_WIDTH // ATT_KV_HEADS
    qblk0 = row0 // tq
    kblk0 = row0 // seq_len
    has_ctx = ctx_k2 is not None
    in_specs = [
        pl.BlockSpec((tq, gw), lambda b, g, i: (qblk0 + b * ntq + i, g)),
        pl.BlockSpec((seq_len, LANES), lambda b, g, i: (kblk0 + b, g)),
        pl.BlockSpec((seq_len, LANES), lambda b, g, i: (kblk0 + b, g)),
    ]
    args = [q, k2, v2]
    if has_ctx:
        past = ctx_k2.shape[1]
        in_specs += [pl.BlockSpec((1, past, LANES), lambda b, g, i: (b, 0, g))] * 2
        args += [ctx_k2, ctx_v2]
    return pl.pallas_call(
        functools.partial(_attn_kernel, has_ctx=has_ctx),
        grid=(n_seq, ATT_KV_HEADS, ntq),
        in_specs=in_specs,
        out_specs=pl.BlockSpec((tq, gw), lambda b, g, i: (b * ntq + i, g)),
        out_shape=jax.ShapeDtypeStruct((n_seq * seq_len, ATT_WIDTH), BF16),
        compiler_params=_cparams(("parallel", "parallel", "parallel")),
        name="attention_ctx" if has_ctx else "attention_self",
    )(*args)


def _gla_kernel(q_ref, k_ref, v_ref, lr_ref, aup_ref, ab_ref, s0_ref, o_ref, sfin_ref,
                s_scr, la_scr, vt_scr, ot_scr, *, chunks_per_seq):
    seg = pl.program_id(0)
    direction = pl.program_id(1)
    n = pl.program_id(2)
    c = q_ref.shape[0]
    kw = GLA_K_WIDTH
    fwd = direction == 0

    @pl.when(n % chunks_per_seq == 0)
    def _():
        s_scr[...] = s0_ref[0, 0]

    x = _dot3(lr_ref[...], aup_ref[0]) + ab_ref[0]
    la = (jnp.minimum(x, 0.0) - jnp.log(1.0 + jnp.exp(-jnp.abs(x)))) * (1.0 / GLA_TAU)

    row = lax.broadcasted_iota(jnp.int32, (c, c), 0)
    col = lax.broadcasted_iota(jnp.int32, (c, c), 1)
    tri = jnp.where(fwd, row - col, col - row) >= 0
    tri_b = tri.astype(BF16)
    la_hi, la_mid, la_lo = _split3(la)
    cum = _dot(tri_b, la_hi) + _dot(tri_b, la_mid) + _dot(tri_b, la_lo)
    bend = jnp.sum(la, axis=0, keepdims=True)
    safe = jnp.min(cum) >= -GLA_SAFE_LOG
    head_of_lane = lax.broadcasted_iota(jnp.int32, (1, kw), 1) // GLA_DK

    @pl.when(safe)
    def _():
        q = q_ref[...] * (GLA_DK ** -0.5)
        k = k_ref[...]
        vb = v_ref[...].astype(BF16)
        qt = (q * jnp.exp(cum)).astype(BF16)
        kt = (k * jnp.exp(-cum)).astype(BF16)
        ke = (k * jnp.exp(bend - cum)).astype(BF16)
        s_old = s_scr[...]
        s_b = s_old.astype(BF16)
        s_new = jnp.exp(bend) * s_old
        zero = jnp.zeros_like(qt)
        for h in range(GLA_HEADS):
            hm = head_of_lane == h
            qh = jnp.where(hm, qt, zero)
            a = jnp.where(tri, _dot_nt(qh, kt), 0.0).astype(BF16)
            vh = vb[:, h * GLA_DV:(h + 1) * GLA_DV]
            o_ref[0, :, h * GLA_DV:(h + 1) * GLA_DV] = _dot(a, vh) + _dot_nt(qh, s_b)
            s_new = s_new + _dot_tn(vh, jnp.where(hm, ke, zero))
        s_scr[...] = s_new
        sfin_ref[0, 0, 0] = s_new

    @pl.when(jnp.logical_not(safe))
    def _():
        la_scr[...] = la
        vt_scr[...] = v_ref[...].T
        ot_scr[...] = jnp.zeros_like(ot_scr)
        lane_c = lax.broadcasted_iota(jnp.int32, (1, c), 1)

        def body(i, s):
            t = jnp.where(fwd, i, c - 1 - i)
            a_t = jnp.exp(la_scr[pl.ds(t, 1), :])
            k_t = k_ref[pl.ds(t, 1), :]
            q_t = q_ref[pl.ds(t, 1), :] * (GLA_DK ** -0.5)
            onehot = lane_c == t
            vmat = jnp.zeros((GLA_DV, kw), F32)
            for h in range(GLA_HEADS):
                vcol = jnp.sum(jnp.where(onehot, vt_scr[h * GLA_DV:(h + 1) * GLA_DV, :], 0.0), axis=1, keepdims=True)
                vmat = jnp.where(head_of_lane == h, vcol, vmat)
            s = a_t * s + vmat * k_t
            prod = s * q_t
            for h in range(GLA_HEADS):
                ocol = jnp.sum(jnp.where(head_of_lane == h, prod, 0.0), axis=1, keepdims=True)
                rows = slice(h * GLA_DV, (h + 1) * GLA_DV)
                ot_scr[rows, :] = jnp.where(onehot, ocol, ot_scr[rows, :])
            return s

        s_new = lax.fori_loop(0, c, body, s_scr[...])
        o_ref[0] = ot_scr[...].T
        s_scr[...] = s_new
        sfin_ref[0, 0, 0] = s_new


def _gla(z, lay, aup_pad, ab, s0, *, seg_len, n_seg, prompt_len):
    t = z.shape[0]
    c = GLA_CHUNK
    nc = seg_len // c
    cps_prompt = prompt_len // c
    gq_blk = lay["gq"] // GLA_K_WIDTH
    gk_blk = lay["gk"] // GLA_K_WIDTH
    gv_blk = COL_GV // GLA_V_WIDTH
    lr_blk = lay["lr"] // LANES

    def chunk(seg, d, n):
        return seg * nc + jnp.where(d == 0, n, nc - 1 - n)

    def kernel(*refs):
        cps = jnp.where(pl.program_id(0) == 0, cps_prompt, nc)
        _gla_kernel(*refs, chunks_per_seq=cps)

    return pl.pallas_call(
        kernel,
        grid=(n_seg, 2, nc),
        in_specs=[
            pl.BlockSpec((c, GLA_K_WIDTH), lambda s, d, n: (chunk(s, d, n), gq_blk)),
            pl.BlockSpec((c, GLA_K_WIDTH), lambda s, d, n: (chunk(s, d, n), gk_blk)),
            pl.BlockSpec((c, GLA_V_WIDTH), lambda s, d, n: (chunk(s, d, n), gv_blk)),
            pl.BlockSpec((c, LANES), lambda s, d, n: (chunk(s, d, n), lr_blk)),
            pl.BlockSpec((1, LANES, GLA_K_WIDTH), lambda s, d, n: (d, 0, 0)),
            pl.BlockSpec((1, 1, GLA_K_WIDTH), lambda s, d, n: (d, 0, 0)),
            pl.BlockSpec((1, 1, GLA_DV, GLA_K_WIDTH), lambda s, d, n: (s, d, 0, 0)),
        ],
        out_specs=[
            pl.BlockSpec((1, c, GLA_V_WIDTH), lambda s, d, n: (d, chunk(s, d, n), 0)),
            pl.BlockSpec((1, 1, 1, GLA_DV, GLA_K_WIDTH), lambda s, d, n: (s, d, jnp.where(d == 0, n, nc - 1 - n), 0, 0)),
        ],
        out_shape=[
            jax.ShapeDtypeStruct((2, t, GLA_V_WIDTH), F32),
            jax.ShapeDtypeStruct((n_seg, 2, nc, GLA_DV, GLA_K_WIDTH), F32),
        ],
        scratch_shapes=[
            pltpu.VMEM((GLA_DV, GLA_K_WIDTH), F32),
            pltpu.VMEM((c, GLA_K_WIDTH), F32),
            pltpu.VMEM((GLA_V_WIDTH, c), F32),
            pltpu.VMEM((GLA_V_WIDTH, c), F32),
        ],
        compiler_params=_cparams(("parallel", "parallel", "arbitrary")),
        name="gla_scan",
    )(z, z, z, z, aup_pad, ab, s0)


def _four_chan_kernel(u_ref, w_ref, o_ref):
    o_ref[...] = _dot(u_ref[...].astype(BF16), w_ref[...]).astype(BF16)


def _fourier_channels(z, w_chan):
    t = z.shape[0]
    tm = 512
    return pl.pallas_call(
        _four_chan_kernel,
        grid=(t // tm,),
        in_specs=[
            pl.BlockSpec((tm, FOURIER_WIDTH), lambda i: (i, COL_FU // FOURIER_WIDTH)),
            _resident((FOURIER_WIDTH, 2 * FOURIER_WIDTH), lambda i: (0, 0)),
        ],
        out_specs=pl.BlockSpec((tm, 2 * FOURIER_WIDTH), lambda i: (i, 0)),
        out_shape=jax.ShapeDtypeStruct((t, 2 * FOURIER_WIDTH), BF16),
        compiler_params=_cparams(("parallel",)),
        name="fourier_channels",
    )(z, w_chan)


def _four_pos_kernel(c_ref, s_ref, xc_ref, xs_ref, o_ref):
    o_ref[...] = (_dot(c_ref[...], xc_ref[...]) + _dot(s_ref[...], xs_ref[...])).astype(BF16)


def _fourier_positions(xcs, cos_m, nsin_m, *, row0, n_seq, seq_len):
    tm = min(512, seq_len)
    nt = seq_len // tm
    blk0 = row0 // seq_len
    return pl.pallas_call(
        _four_pos_kernel,
        grid=(n_seq, nt),
        in_specs=[
            pl.BlockSpec((tm, seq_len), lambda b, i: (i, 0)),
            pl.BlockSpec((tm, seq_len), lambda b, i: (i, 0)),
            pl.BlockSpec((seq_len, FOURIER_WIDTH), lambda b, i: (blk0 + b, 0)),
            pl.BlockSpec((seq_len, FOURIER_WIDTH), lambda b, i: (blk0 + b, 1)),
        ],
        out_specs=pl.BlockSpec((tm, FOURIER_WIDTH), lambda b, i: (b * nt + i, 0)),
        out_shape=jax.ShapeDtypeStruct((n_seq * seq_len, FOURIER_WIDTH), BF16),
        compiler_params=_cparams(("parallel", "parallel")),
        name="fourier_positions",
    )(cos_m, nsin_m, xcs, xcs)


def _merge_kernel(att_ref, gof_ref, gob_ref, gr_ref, four_ref, g0_ref, g1_ref, g2_ref, x_ref, mod_ref,
                  gg_ref, wb_ref, wo_ref, o_ref, *, d):
    o = gof_ref[0] + gob_ref[0]
    gg = gg_ref[...]
    parts = []
    for h in range(GLA_HEADS):
        oh = o[:, h * GLA_DV:(h + 1) * GLA_DV]
        ms = jnp.mean(oh * oh, axis=-1, keepdims=True)
        parts.append(oh * lax.rsqrt(ms + EPS) * gg)
    gla = (jnp.concatenate(parts, axis=1) * _silu(gr_ref[...])).astype(BF16)
    merged = (_sigmoid(g0_ref[...]) * _dot(att_ref[...], wb_ref[0])
              + _sigmoid(g1_ref[...]) * _dot(gla, wb_ref[1])
              + _sigmoid(g2_ref[...]) * _dot(four_ref[...], wb_ref[2]))
    mix = _dot(merged.astype(BF16), wo_ref[...])
    o_ref[...] = x_ref[...] + mod_ref[0, :, 2 * d:3 * d] * mix


def _merge(att, gla_o, z, four, x, mod3, gla_g, w_branch, w_out, seg_len):
    t, d = x.shape
    tm = min(512, seg_len)
    tps = seg_len // tm
    bw = att.shape[1]
    gate_blk = COL_GATES // d
    return pl.pallas_call(
        functools.partial(_merge_kernel, d=d),
        grid=(t // tm,),
        in_specs=[
            pl.BlockSpec((tm, bw), lambda i: (i, 0)),
            pl.BlockSpec((1, tm, GLA_V_WIDTH), lambda i: (0, i, 0)),
            pl.BlockSpec((1, tm, GLA_V_WIDTH), lambda i: (1, i, 0)),
            pl.BlockSpec((tm, GLA_V_WIDTH), lambda i: (i, COL_GR // GLA_V_WIDTH)),
            pl.BlockSpec((tm, bw), lambda i: (i, 0)),
            pl.BlockSpec((tm, d), lambda i: (i, gate_blk)),
            pl.BlockSpec((tm, d), lambda i: (i, gate_blk + 1)),
            pl.BlockSpec((tm, d), lambda i: (i, gate_blk + 2)),
            pl.BlockSpec((tm, d), lambda i: (i, 0)),
            pl.BlockSpec((1, 1, 6 * d), lambda i: (i // tps, 0, 0)),
            pl.BlockSpec((1, GLA_DV), lambda i: (0, 0)),
            _resident((N_BRANCHES, bw, d), lambda i: (0, 0, 0)),
            _resident((d, d), lambda i: (0, 0)),
        ],
        out_specs=pl.BlockSpec((tm, d), lambda i: (i, 0)),
        out_shape=jax.ShapeDtypeStruct((t, d), F32),
        compiler_params=_cparams(("parallel",)),
        name="branch_merge",
    )(att, gla_o, gla_o, z, four, z, z, z, x, mod3, gla_g.reshape(1, GLA_DV), w_branch, w_out)


def _ffn_kernel(x_ref, mod_ref, g_ref, w1_ref, w3_ref, w2_ref, o_ref, *, d):
    x = x_ref[...]
    h = _norm_mod(x, g_ref[...], mod_ref[0, :, 3 * d:4 * d], mod_ref[0, :, 4 * d:5 * d]).astype(BF16)
    a = _dot(h, w1_ref[...])
    u = (_silu(a) * _dot(h, w3_ref[...])).astype(BF16)
    o_ref[...] = x + mod_ref[0, :, 5 * d:6 * d] * _dot(u, w2_ref[...])


def _dense_ffn(x, mod3, g, w1, w3, w2, seg_len):
    t, d = x.shape
    dff = w1.shape[1]
    tm = min(512, seg_len)
    tps = seg_len // tm
    return pl.pallas_call(
        functools.partial(_ffn_kernel, d=d),
        grid=(t // tm,),
        in_specs=[
            pl.BlockSpec((tm, d), lambda i: (i, 0)),
            pl.BlockSpec((1, 1, 6 * d), lambda i: (i // tps, 0, 0)),
            pl.BlockSpec((1, d), lambda i: (0, 0)),
            _resident((d, dff), lambda i: (0, 0)),
            _resident((d, dff), lambda i: (0, 0)),
            _resident((dff, d), lambda i: (0, 0)),
        ],
        out_specs=pl.BlockSpec((tm, d), lambda i: (i, 0)),
        out_shape=jax.ShapeDtypeStruct((t, d), F32),
        compiler_params=_cparams(("parallel",)),
        name="dense_ffn",
    )(x, mod3, g.reshape(1, d), w1, w3, w2)


MOE_TILE = 512
ROUTE_E1, ROUTE_E2, ROUTE_R1, ROUTE_R2, ROUTE_W1, ROUTE_W2 = range(6)


def _route_kernel(x_ref, mod_ref, g_ref, rw_ref, h_ref, route_ref, count_ref, carry_scr, *, d):
    @pl.when(pl.program_id(0) == 0)
    def _():
        carry_scr[...] = jnp.zeros_like(carry_scr)

    h = _norm_mod(x_ref[...], g_ref[...], mod_ref[0, :, 3 * d:4 * d], mod_ref[0, :, 4 * d:5 * d])
    h_ref[...] = h
    logits = _dot3(h, rw_ref[...])
    tm = logits.shape[0]
    lane = lax.broadcasted_iota(jnp.int32, logits.shape, 1).astype(F32)
    neg = jnp.float32(-jnp.inf)
    l1 = jnp.where(lane < N_EXPERTS, logits, neg)
    m1 = jnp.max(l1, axis=-1, keepdims=True)
    i1 = jnp.min(jnp.where(l1 == m1, lane, float(LANES)), axis=-1, keepdims=True)
    l2 = jnp.where(lane == i1, neg, l1)
    m2 = jnp.max(l2, axis=-1, keepdims=True)
    i2 = jnp.min(jnp.where(l2 == m2, lane, float(LANES)), axis=-1, keepdims=True)
    e2 = jnp.exp(m2 - m1)
    w1 = 1.0 / (1.0 + e2)
    w2 = e2 * w1
    onehot = jnp.where((lane == i1) | (lane == i2), 1.0, 0.0)
    row = lax.broadcasted_iota(jnp.int32, (tm, tm), 0)
    col = lax.broadcasted_iota(jnp.int32, (tm, tm), 1)
    before = jnp.where(row > col, 1.0, 0.0).astype(BF16)
    rank = _dot(before, onehot.astype(BF16)) + carry_scr[...]
    r1 = jnp.sum(jnp.where(lane == i1, rank, 0.0), axis=-1, keepdims=True)
    r2 = jnp.sum(jnp.where(lane == i2, rank, 0.0), axis=-1, keepdims=True)
    rec = jnp.zeros_like(logits)
    for idx, val in ((ROUTE_E1, i1), (ROUTE_E2, i2), (ROUTE_R1, r1), (ROUTE_R2, r2), (ROUTE_W1, w1), (ROUTE_W2, w2)):
        rec = jnp.where(lane == idx, val, rec)
    route_ref[...] = rec
    total = carry_scr[...] + jnp.sum(onehot, axis=0, keepdims=True)
    carry_scr[...] = total
    count_ref[...] = total


def _route(x, mod3, g, rw_pad, seg_len):
    t, d = x.shape
    tm = min(512, seg_len)
    tps = seg_len // tm
    return pl.pallas_call(
        functools.partial(_route_kernel, d=d),
        grid=(t // tm,),
        in_specs=[
            pl.BlockSpec((tm, d), lambda i: (i, 0)),
            pl.BlockSpec((1, 1, 6 * d), lambda i: (i // tps, 0, 0)),
            pl.BlockSpec((1, d), lambda i: (0, 0)),
            pl.BlockSpec((d, LANES), lambda i: (0, 0)),
        ],
        out_specs=[
            pl.BlockSpec((tm, d), lambda i: (i, 0)),
            pl.BlockSpec((tm, LANES), lambda i: (i, 0)),
            pl.BlockSpec((1, LANES), lambda i: (0, 0)),
        ],
        out_shape=[
            jax.ShapeDtypeStruct((t, d), F32),
            jax.ShapeDtypeStruct((t, LANES), F32),
            jax.ShapeDtypeStruct((1, LANES), F32),
        ],
        scratch_shapes=[pltpu.VMEM((1, LANES), F32)],
        compiler_params=_cparams(("arbitrary",)),
        name="moe_route",
    )(x, mod3, g.reshape(1, d), rw_pad)


def _row_copy(src_hbm, src_row, dst_hbm, dst_row, sem):
    return pltpu.make_async_copy(src_hbm.at[pl.ds(src_row, 1)], dst_hbm.at[pl.ds(dst_row, 1)], sem)


def _dispatch_kernel(d1_ref, d2_ref, h_hbm, xs_in_hbm, xs_hbm, sem, *, rows_per_step):
    del xs_in_hbm
    base = pl.program_id(0) * rows_per_step

    def start(i, carry):
        tok = base + i
        _row_copy(h_hbm, tok, xs_hbm, d1_ref[tok], sem).start()
        _row_copy(h_hbm, tok, xs_hbm, d2_ref[tok], sem).start()
        return carry

    def wait(i, carry):
        _row_copy(h_hbm, 0, xs_hbm, 0, sem).wait()
        _row_copy(h_hbm, 0, xs_hbm, 0, sem).wait()
        return carry

    lax.fori_loop(0, rows_per_step, start, 0)
    lax.fori_loop(0, rows_per_step, wait, 0)


def _dispatch(h, dest1, dest2, n_slots):
    t, d = h.shape
    rows = 512
    xs0 = jnp.zeros((n_slots, d), h.dtype)
    return pl.pallas_call(
        functools.partial(_dispatch_kernel, rows_per_step=rows),
        grid_spec=pltpu.PrefetchScalarGridSpec(
            num_scalar_prefetch=2,
            grid=(t // rows,),
            in_specs=[pl.BlockSpec(memory_space=pl.ANY), pl.BlockSpec(memory_space=pl.ANY)],
            out_specs=pl.BlockSpec(memory_space=pl.ANY),
            scratch_shapes=[pltpu.SemaphoreType.DMA],
        ),
        out_shape=jax.ShapeDtypeStruct((n_slots, d), h.dtype),
        input_output_aliases={3: 0},
        compiler_params=_cparams(("arbitrary",)),
        name="moe_dispatch",
    )(dest1, dest2, h, xs0)


def _expert_changed(te_ref, j):
    prev = te_ref[jnp.maximum(j - 1, 0)]
    return (j == 0) | (te_ref[j] != prev)


def _expert_up_kernel(te_ref, nv_ref, xs_ref, w1_ref, w3_ref, u_ref, w1b_scr, w3b_scr):
    j = pl.program_id(1)

    @pl.when(_expert_changed(te_ref, j))
    def _():
        w1b_scr[...] = w1_ref[0].astype(BF16)
        w3b_scr[...] = w3_ref[0].astype(BF16)

    @pl.when(j < nv_ref[0])
    def _():
        xs = xs_ref[...].astype(BF16)
        a = _dot(xs, w1b_scr[...])
        u_ref[...] = (_silu(a) * _dot(xs, w3b_scr[...])).astype(BF16)

    @pl.when(j >= nv_ref[0])
    def _():
        u_ref[...] = jnp.zeros_like(u_ref)


def _expert_up(xs, tile_expert, n_valid, w1, w3, layer):
    s, d = xs.shape
    dff = w1.shape[3]
    tf = dff // 4 if (dff // 4) % LANES == 0 else dff
    nt = s // MOE_TILE

    def rows(f, j, te, nv):
        return (jnp.minimum(j, nv[0] - 1), 0)

    return pl.pallas_call(
        _expert_up_kernel,
        grid_spec=pltpu.PrefetchScalarGridSpec(
            num_scalar_prefetch=2,
            grid=(dff // tf, nt),
            in_specs=[
                pl.BlockSpec((MOE_TILE, d), rows),
                pl.BlockSpec((None, 1, d, tf), lambda f, j, te, nv: (layer, te[j], 0, f)),
                pl.BlockSpec((None, 1, d, tf), lambda f, j, te, nv: (layer, te[j], 0, f)),
            ],
            out_specs=pl.BlockSpec((MOE_TILE, tf), lambda f, j, te, nv: (j, f)),
            scratch_shapes=[pltpu.VMEM((d, tf), BF16), pltpu.VMEM((d, tf), BF16)],
        ),
        out_shape=jax.ShapeDtypeStruct((s, dff), BF16),
        compiler_params=_cparams(("arbitrary", "arbitrary")),
        name="moe_expert_up",
    )(tile_expert, n_valid, xs, w1, w3)


def _expert_down_kernel(te_ref, nv_ref, u_ref, w2_ref, y_ref, w2b_scr):
    j = pl.program_id(1)

    @pl.when(_expert_changed(te_ref, j))
    def _():
        w2b_scr[...] = w2_ref[0].astype(BF16)

    @pl.when(j < nv_ref[0])
    def _():
        y_ref[...] = _dot(u_ref[...], w2b_scr[...])

    @pl.when(j >= nv_ref[0])
    def _():
        y_ref[...] = jnp.zeros_like(y_ref)


def _expert_down(u, tile_expert, n_valid, w2, layer):
    s, dff = u.shape
    d = w2.shape[3]
    tn = d // 2
    nt = s // MOE_TILE
    return pl.pallas_call(
        _expert_down_kernel,
        grid_spec=pltpu.PrefetchScalarGridSpec(
            num_scalar_prefetch=2,
            grid=(d // tn, nt),
            in_specs=[
                pl.BlockSpec((MOE_TILE, dff), lambda n, j, te, nv: (jnp.minimum(j, nv[0] - 1), 0)),
                pl.BlockSpec((None, 1, dff, tn), lambda n, j, te, nv: (layer, te[j], 0, n)),
            ],
            out_specs=pl.BlockSpec((MOE_TILE, tn), lambda n, j, te, nv: (j, n)),
            scratch_shapes=[pltpu.VMEM((dff, tn), BF16)],
        ),
        out_shape=jax.ShapeDtypeStruct((s, d), F32),
        compiler_params=_cparams(("arbitrary", "arbitrary")),
        name="moe_expert_down",
    )(tile_expert, n_valid, u, w2)


def _combine_kernel(d1_ref, d2_ref, x_ref, mod_ref, route_ref, ys_hbm, o_ref, buf1, buf2, sem, *, d):
    tc = x_ref.shape[0]
    base = pl.program_id(0) * tc

    def start(i, carry):
        _row_copy(ys_hbm, d1_ref[base + i], buf1, i, sem).start()
        _row_copy(ys_hbm, d2_ref[base + i], buf2, i, sem).start()
        return carry

    def wait(i, carry):
        _row_copy(ys_hbm, 0, buf1, 0, sem).wait()
        _row_copy(ys_hbm, 0, buf2, 0, sem).wait()
        return carry

    lax.fori_loop(0, tc, start, 0)
    lax.fori_loop(0, tc, wait, 0)
    route = route_ref[...]
    lane = lax.broadcasted_iota(jnp.int32, route.shape, 1)
    w1 = jnp.sum(jnp.where(lane == ROUTE_W1, route, 0.0), axis=-1, keepdims=True)
    w2 = jnp.sum(jnp.where(lane == ROUTE_W2, route, 0.0), axis=-1, keepdims=True)
    o_ref[...] = x_ref[...] + mod_ref[0, :, 5 * d:6 * d] * (w1 * buf1[...] + w2 * buf2[...])


def _combine(x, mod3, route, ys, dest1, dest2, seg_len):
    t, d = x.shape
    tc = 256
    tps = seg_len // tc
    return pl.pallas_call(
        functools.partial(_combine_kernel, d=d),
        grid_spec=pltpu.PrefetchScalarGridSpec(
            num_scalar_prefetch=2,
            grid=(t // tc,),
            in_specs=[
                pl.BlockSpec((tc, d), lambda i, a, b: (i, 0)),
                pl.BlockSpec((1, 1, 6 * d), lambda i, a, b: (i // tps, 0, 0)),
                pl.BlockSpec((tc, LANES), lambda i, a, b: (i, 0)),
                pl.BlockSpec(memory_space=pl.ANY),
            ],
            out_specs=pl.BlockSpec((tc, d), lambda i, a, b: (i, 0)),
            scratch_shapes=[pltpu.VMEM((tc, d), F32), pltpu.VMEM((tc, d), F32), pltpu.SemaphoreType.DMA],
        ),
        out_shape=jax.ShapeDtypeStruct((t, d), F32),
        compiler_params=_cparams(("arbitrary",)),
        name="moe_combine",
    )(dest1, dest2, x, mod3, route, ys)


def _moe_ffn(x, mod3, g, rw_pad, w1, w3, w2, layer, seg_len):
    t, d = x.shape
    h, route, counts = _route(x, mod3, g, rw_pad, seg_len)
    cnt = counts[0, :N_EXPERTS].astype(jnp.int32)
    padded = (cnt + MOE_TILE - 1) // MOE_TILE * MOE_TILE
    ends = jnp.cumsum(padded)
    offs = ends - padded
    e1 = route[:, ROUTE_E1].astype(jnp.int32)
    e2 = route[:, ROUTE_E2].astype(jnp.int32)
    dest1 = offs[e1] + route[:, ROUTE_R1].astype(jnp.int32)
    dest2 = offs[e2] + route[:, ROUTE_R2].astype(jnp.int32)
    n_tiles = (2 * t) // MOE_TILE + N_EXPERTS
    tile_start = jnp.arange(n_tiles, dtype=jnp.int32) * MOE_TILE
    tile_expert = jnp.minimum(jnp.searchsorted(ends, tile_start, side="right"), N_EXPERTS - 1).astype(jnp.int32)
    n_valid = (ends[-1:] // MOE_TILE).astype(jnp.int32)
    xs = _dispatch(h, dest1, dest2, n_tiles * MOE_TILE)
    u = _expert_up(xs, tile_expert, n_valid, w1, w3, layer)
    ys = _expert_down(u, tile_expert, n_valid, w2, layer)
    return _combine(x, mod3, route, ys, dest1, dest2, seg_len)


def _final_norm_kernel(x_ref, g_ref, o_ref):
    x = x_ref[...]
    ms = jnp.mean(x * x, axis=-1, keepdims=True)
    o_ref[...] = x * lax.rsqrt(ms + EPS) * g_ref[...]


def _final_norm(x, g):
    t, d = x.shape
    tm = 512
    return pl.pallas_call(
        _final_norm_kernel,
        grid=(t // tm,),
        in_specs=[pl.BlockSpec((tm, d), lambda i: (i, 0)), pl.BlockSpec((1, d), lambda i: (0, 0))],
        out_specs=pl.BlockSpec((tm, d), lambda i: (i, 0)),
        out_shape=jax.ShapeDtypeStruct((t, d), F32),
        compiler_params=_cparams(("parallel",)),
        name="final_norm",
    )(x, g.reshape(1, d))


def _rope_tables(n_tokens):
    pos = jnp.arange(n_tokens, dtype=jnp.int32)
    rowcol = jnp.stack([pos // GRID_W, pos % GRID_W], axis=-1).astype(F32)
    inv_freq = 1.0 / (ROPE_THETA ** (jnp.arange(ROPE_FREQS, dtype=F32) / ROPE_FREQS))
    ang = rowcol[:, :, None] * inv_freq
    cos = jnp.cos(ang)
    sin = jnp.sin(ang)
    cos_h = jnp.stack([cos, cos], axis=2).reshape(n_tokens, HEAD_DIM)
    sin_h = jnp.stack([-sin, sin], axis=2).reshape(n_tokens, HEAD_DIM)
    cos_t = jnp.tile(cos_h, (1, LANES // HEAD_DIM))
    sin_t = jnp.tile(sin_h, (1, LANES // HEAD_DIM))
    return (jnp.stack([jnp.ones_like(cos_t), cos_t]), jnp.stack([jnp.zeros_like(sin_t), sin_t]))


def _dft_matrices(n):
    idx = jnp.arange(n, dtype=jnp.int32)
    ang = ((idx[:, None] * idx[None, :]) % n).astype(F32) * (2.0 * math.pi / n)
    scale = n ** -0.5
    return (jnp.cos(ang) * scale).astype(BF16), (jnp.sin(ang) * -scale).astype(BF16)


def _channel_dft_weight():
    n = FOURIER_GROUP_DIM
    idx = jnp.arange(n, dtype=jnp.int32)
    ang = ((idx[:, None] * idx[None, :]) % n).astype(F32) * (2.0 * math.pi / n)
    eye = jnp.eye(FOURIER_GROUPS, dtype=F32)
    cos_bd = jnp.kron(eye, jnp.cos(ang) * n ** -0.5)
    sin_bd = jnp.kron(eye, jnp.sin(ang) * n ** -0.5)
    return jnp.concatenate([cos_bd, sin_bd], axis=1).astype(BF16)


def _repack_w_in(w_in, lay):
    depth, d, _ = w_in.shape
    widths = (ATT_WIDTH, KV_WIDTH, KV_WIDTH, GLA_K_WIDTH, GLA_K_WIDTH, GLA_V_WIDTH, GLA_V_WIDTH,
              GLA_LOW_RANK, GLA_LOW_RANK, FOURIER_WIDTH, N_BRANCHES * d)
    offs = [0]
    for w in widths:
        offs.append(offs[-1] + w)
    aq, ak, av, gq, gk, gv, gr, lrf, lrb, fu, gates = [w_in[:, :, offs[i]:offs[i + 1]] for i in range(len(widths))]
    used = lay["lr"] + 2 * GLA_LOW_RANK
    pad = jnp.zeros((depth, d, lay["total"] - used), w_in.dtype)
    return jnp.concatenate([aq, gv, gr, fu, gates, gq, gk, ak, av, lrf, lrb, pad], axis=-1).astype(BF16)


def kernel(x_prompt, x_sample, cache_k, cache_v, state_gla, c, c_ctx, ada_w, ada_b, norm1_g, norm2_g, w_in, q_norm_g, k_norm_g, alpha_up, alpha_b, gla_norm_g, w_branch, w_out, ffn_w1, ffn_w3, ffn_w2, router_w, moe_w1, moe_w3, moe_w2, final_g):
    batch, seq, d = x_prompt.shape
    dec_batch, dec_seq, _ = x_sample.shape
    depth = w_in.shape[0]
    past = cache_k.shape[2]
    seg_len = batch * seq
    assert seg_len == dec_seq, "context tokens and each latent request must fill equal segments"
    assert seq % GLA_CHUNK == 0 and dec_seq % GLA_CHUNK == 0
    assert COL_GATES % d == 0
    n_seg = 1 + dec_batch
    t = n_seg * seg_len
    lay = _in_layout(d)

    w_in_p = _repack_w_in(w_in, lay)
    w_branch_b = w_branch.astype(BF16)
    w_out_b = w_out.astype(BF16)
    ffn_w1_b, ffn_w3_b, ffn_w2_b = ffn_w1.astype(BF16), ffn_w3.astype(BF16), ffn_w2.astype(BF16)
    router_pad = jnp.pad(router_w, ((0, 0), (0, 0), (0, LANES - N_EXPERTS)))
    cos_tab, sin_tab = _rope_tables(dec_seq)
    dft_cos_s, dft_nsin_s = _dft_matrices(dec_seq)
    dft_cos_p, dft_nsin_p = _dft_matrices(seq)
    w_chan = _channel_dft_weight()
    ones_bd = jnp.kron(jnp.eye(ATT_HEADS, dtype=F32), jnp.ones((HEAD_DIM, HEAD_DIM), F32)).astype(BF16)
    q_g = jnp.tile(q_norm_g, (1, ATT_HEADS)).reshape(depth, 1, ATT_WIDTH)
    k_g = jnp.tile(k_norm_g, (1, ATT_KV_HEADS)).reshape(depth, 1, KV_WIDTH)
    aup_pad = jnp.zeros((depth, 2, LANES, GLA_K_WIDTH), F32)
    aup_pad = aup_pad.at[:, 0, :GLA_LOW_RANK].set(alpha_up[:, 0])
    aup_pad = aup_pad.at[:, 1, GLA_LOW_RANK:2 * GLA_LOW_RANK].set(alpha_up[:, 1])
    ab = alpha_b.reshape(depth, 2, 1, GLA_K_WIDTH)
    s0_lat = state_gla.transpose(0, 1, 2, 5, 3, 4).reshape(dec_batch, depth, 2, GLA_DV, GLA_K_WIDTH)
    s0_all = jnp.concatenate([jnp.zeros((1,) + s0_lat.shape[1:], F32), s0_lat], axis=0)

    def dup_cache(x):
        xb = x.astype(BF16)[:, :, :, None, :]
        return jnp.broadcast_to(xb, (dec_batch, past, ATT_KV_HEADS, 2, HEAD_DIM)).reshape(dec_batch, past, 2 * KV_WIDTH)

    cond = jnp.zeros((8, d), F32).at[0].set(c_ctx).at[1:1 + dec_batch].set(c)
    mod_all = _modulation(cond, ada_w, ada_b)

    x = jnp.concatenate([x_prompt.reshape(seg_len, d), x_sample.reshape(dec_batch * dec_seq, d)], axis=0)
    new_k, new_v, new_s = [], [], []
    for l in range(depth):
        mod3 = mod_all[l, :n_seg].reshape(n_seg, 1, 6 * d)
        z = _in_projection(x, mod3, norm1_g[l], w_in_p[l], seg_len)
        qn, k_out, k2, v2 = _attention_prep(z, lay, cos_tab, sin_tab, q_g[l], k_g[l], ones_bd, seg_len)
        att_p = _attention(qn, k2, v2, None, None, row0=0, n_seq=batch, seq_len=seq, tq=seq)
        att_s = _attention(qn, k2, v2, dup_cache(cache_k[:, l]), dup_cache(cache_v[:, l]),
                           row0=seg_len, n_seq=dec_batch, seq_len=dec_seq, tq=min(128, dec_seq))
        att = jnp.concatenate([att_p, att_s], axis=0)
        gla_o, s_fin = _gla(z, lay, aup_pad[l], ab[l], s0_all[:, l], seg_len=seg_len, n_seg=n_seg, prompt_len=seq)
        xcs = _fourier_channels(z, w_chan)
        four_p = _fourier_positions(xcs, dft_cos_p, dft_nsin_p, row0=0, n_seq=batch, seq_len=seq)
        four_s = _fourier_positions(xcs, dft_cos_s, dft_nsin_s, row0=seg_len, n_seq=dec_batch, seq_len=dec_seq)
        four = jnp.concatenate([four_p, four_s], axis=0)
        x = _merge(att, gla_o, z, four, x, mod3, gla_norm_g[l], w_branch_b[l], w_out_b[l], seg_len)
        j = l // 2
        if l % 2 == 0:
            x = _dense_ffn(x, mod3, norm2_g[l], ffn_w1_b[j], ffn_w3_b[j], ffn_w2_b[j], seg_len)
        else:
            x = _moe_ffn(x, mod3, norm2_g[l], router_pad[j], moe_w1, moe_w3, moe_w2, j, seg_len)
        new_k.append(k_out[:seg_len].reshape(batch, seq, ATT_KV_HEADS, HEAD_DIM))
        v_col = lay["akv"] + KV_WIDTH
        new_v.append(z[:seg_len, v_col:v_col + KV_WIDTH].reshape(batch, seq, ATT_KV_HEADS, HEAD_DIM))
        cps = seq // GLA_CHUNK
        sf = s_fin[0, 0, cps - 1::cps]
        sb = s_fin[0, 1, 0::cps]
        st = jnp.stack([sf, sb], axis=1).reshape(batch, 2, GLA_DV, GLA_HEADS, GLA_DK)
        new_s.append(st.transpose(0, 1, 3, 4, 2))
    y = _final_norm(x, final_g)
    y_prompt = y[:seg_len].reshape(batch, seq, d)
    y_sample = y[seg_len:].reshape(dec_batch, dec_seq, d)
    return (y_prompt, y_sample, jnp.stack(new_k, axis=1), jnp.stack(new_v, axis=1), jnp.stack(new_s, axis=1))
```

```python
import functools
import math

import jax
import jax.numpy as jnp
from jax import lax
from jax.experimental import pallas as pl
from jax.experimental.pallas import tpu as pltpu

F32 = jnp.float32
BF16 = jnp.bfloat16

EPS = 1e-6
ATT_HEADS = 8
ATT_KV_HEADS = 2
HEAD_DIM = 64
ATT_WIDTH = ATT_HEADS * HEAD_DIM
KV_WIDTH = ATT_KV_HEADS * HEAD_DIM
ROPE_FREQS = HEAD_DIM // 4
ROPE_THETA = 10000.0
GRID_W = 64
GLA_HEADS = 4
GLA_DK = 64
GLA_DV = 128
GLA_K_WIDTH = GLA_HEADS * GLA_DK
GLA_V_WIDTH = GLA_HEADS * GLA_DV
GLA_LOW_RANK = 16
GLA_TAU = 16.0
FOURIER_GROUPS = 4
FOURIER_GROUP_DIM = 128
FOURIER_WIDTH = FOURIER_GROUPS * FOURIER_GROUP_DIM
N_BRANCHES = 3
N_EXPERTS = 8

LANES = 128
VMEM_LIMIT_BYTES = 56 * 1024 * 1024

COL_AQ = 0
COL_GV = 512
COL_GR = 1024
COL_FU = 1536
COL_GATES = 2048

GLA_CHUNK = 256
GLA_SAFE_LOG = 60.0


def _cparams(sem):
    return pltpu.CompilerParams(dimension_semantics=sem, vmem_limit_bytes=VMEM_LIMIT_BYTES)


def _resident(shape, index_map):
    return pl.BlockSpec(shape, index_map, pipeline_mode=pl.Buffered(1))


def _split2(x):
    hi = x.astype(BF16)
    lo = (x - hi.astype(F32)).astype(BF16)
    return hi, lo


def _split3(x):
    hi = x.astype(BF16)
    r = x - hi.astype(F32)
    mid = r.astype(BF16)
    lo = (r - mid.astype(F32)).astype(BF16)
    return hi, mid, lo


def _dot(a, b):
    return jnp.dot(a, b, preferred_element_type=F32)


def _dot_nt(a, b):
    return lax.dot_general(a, b, (((1,), (1,)), ((), ())), preferred_element_type=F32)


def _dot_tn(a, b):
    return lax.dot_general(a, b, (((0,), (0,)), ((), ())), preferred_element_type=F32)


def _dot3(a, b):
    a_hi, a_lo = _split2(a)
    b_hi, b_lo = _split2(b)
    return _dot(a_hi, b_hi) + _dot(a_lo, b_hi) + _dot(a_hi, b_lo)


def _sigmoid(x):
    return 1.0 / (1.0 + jnp.exp(-x))


def _silu(x):
    return x * _sigmoid(x)


def _in_layout(d_model):
    gates_end = COL_GATES + N_BRANCHES * d_model
    col_gq = gates_end
    col_gk = col_gq + GLA_K_WIDTH
    col_akv = col_gk + GLA_K_WIDTH
    col_lr = col_akv + 2 * KV_WIDTH
    used = col_lr + LANES
    total = -(-used // 1024) * 1024
    return dict(gq=col_gq, gk=col_gk, akv=col_akv, lr=col_lr, total=total)


def _mod_kernel(cond_ref, w_ref, b_ref, o_ref):
    c = _silu(cond_ref[...])
    o_ref[0] = _dot3(c, w_ref[0]) + b_ref[0]


def _modulation(cond, ada_w, ada_b):
    depth, d, n6 = ada_w.shape
    tn = 1536 if n6 % 1536 == 0 else n6
    rows = cond.shape[0]
    return pl.pallas_call(
        _mod_kernel,
        grid=(depth, n6 // tn),
        in_specs=[
            pl.BlockSpec((rows, d), lambda l, j: (0, 0)),
            pl.BlockSpec((1, d, tn), lambda l, j: (l, 0, j)),
            pl.BlockSpec((1, 1, tn), lambda l, j: (l, 0, j)),
        ],
        out_specs=pl.BlockSpec((1, rows, tn), lambda l, j: (l, 0, j)),
        out_shape=jax.ShapeDtypeStruct((depth, rows, n6), F32),
        compiler_params=_cparams(("parallel", "parallel")),
        name="adaln_mod",
    )(cond, ada_w, ada_b.reshape(depth, 1, n6))


def _norm_mod(x, g, shift, scale):
    ms = jnp.mean(x * x, axis=-1, keepdims=True)
    return x * lax.rsqrt(ms + EPS) * g * (1.0 + scale) + shift


def _inproj_kernel(x_ref, mod_ref, g_ref, w_ref, z_ref, h_scr, *, d):
    @pl.when(pl.program_id(1) == 0)
    def _():
        h = _norm_mod(x_ref[...], g_ref[...], mod_ref[0, :, 0:d], mod_ref[0, :, d:2 * d])
        h_scr[...] = h.astype(BF16)

    z_ref[...] = _dot(h_scr[...], w_ref[...])


def _in_projection(x, mod3, g, w, seg_len):
    t, d = x.shape
    n = w.shape[1]
    tm = min(1024, seg_len)
    tn = 1024
    tps = seg_len // tm
    return pl.pallas_call(
        functools.partial(_inproj_kernel, d=d),
        grid=(t // tm, n // tn),
        in_specs=[
            pl.BlockSpec((tm, d), lambda i, j: (i, 0)),
            pl.BlockSpec((1, 1, 6 * d), lambda i, j: (i // tps, 0, 0)),
            pl.BlockSpec((1, d), lambda i, j: (0, 0)),
            pl.BlockSpec((d, tn), lambda i, j: (0, j)),
        ],
        out_specs=pl.BlockSpec((tm, tn), lambda i, j: (i, j)),
        out_shape=jax.ShapeDtypeStruct((t, n), F32),
        scratch_shapes=[pltpu.VMEM((tm, d), BF16)],
        compiler_params=_cparams(("parallel", "arbitrary")),
        name="in_projection",
    )(x, mod3, g.reshape(1, d), w)


def _head_rms(x, g, ones_bd):
    hi, lo = _split2(x * x)
    ss = _dot(hi, ones_bd) + _dot(lo, ones_bd)
    return x * lax.rsqrt(ss * (1.0 / HEAD_DIM) + EPS) * g


def _rope(x, cos, sin_signed):
    width = x.shape[1]
    reps = width // LANES
    if reps > 1:
        cos = jnp.concatenate([cos] * reps, axis=1)
        sin_signed = jnp.concatenate([sin_signed] * reps, axis=1)
    lane = lax.broadcasted_iota(jnp.int32, x.shape, 1)
    first_half = (lane % (2 * ROPE_FREQS)) < ROPE_FREQS
    partner = jnp.where(first_half, pltpu.roll(x, width - ROPE_FREQS, 1), pltpu.roll(x, ROPE_FREQS, 1))
    return x * cos + partner * sin_signed


ATT_KEY_CHUNK = 512
ATT_QUERY_TILE = 256
VT_ROWS = 2 * HEAD_DIM
VT_USED = HEAD_DIM + 16


def _dup_heads(x):
    lane = lax.broadcasted_iota(jnp.int32, x.shape, 1)
    swapped = pltpu.roll(x, HEAD_DIM, 1)
    low = lane < HEAD_DIM
    return jnp.concatenate([jnp.where(low, x, swapped), jnp.where(low, swapped, x)], axis=1)


def _attn_prep_kernel(zq_ref, zkv_ref, cos_ref, sin_ref, qg_ref, kg_ref, bd_ref, q_ref, kout_ref, k2_ref, vt_ref):
    cos = cos_ref[0]
    sin = sin_ref[0]
    bd = bd_ref[...]
    q = _head_rms(zq_ref[...], qg_ref[...], bd)
    q_ref[...] = (_rope(q, cos, sin) * (HEAD_DIM ** -0.5)).astype(BF16)
    kv = zkv_ref[...]
    k = _head_rms(kv[:, :KV_WIDTH], kg_ref[...], bd[:KV_WIDTH, :KV_WIDTH])
    kout_ref[...] = k
    k2_ref[...] = _dup_heads(_rope(k, cos, sin)).astype(BF16)
    vt = kv[:, KV_WIDTH:].T
    ones = jnp.ones((HEAD_DIM, vt.shape[1]), F32)
    vt_ref[...] = jnp.concatenate([vt[:HEAD_DIM], ones, vt[HEAD_DIM:], ones], axis=0).astype(BF16)


def _attention_prep(z, lay, cos_tab, sin_tab, q_g, k_g, ones_bd, seg_len):
    t = z.shape[0]
    tm = min(512, seg_len)
    tps = seg_len // tm
    akv_blk = lay["akv"] // (2 * KV_WIDTH)
    tab_spec = pl.BlockSpec((1, tm, LANES), lambda i: (jnp.minimum(i // tps, 1), i % tps, 0))
    return pl.pallas_call(
        _attn_prep_kernel,
        grid=(t // tm,),
        in_specs=[
            pl.BlockSpec((tm, ATT_WIDTH), lambda i: (i, COL_AQ // ATT_WIDTH)),
            pl.BlockSpec((tm, 2 * KV_WIDTH), lambda i: (i, akv_blk)),
            tab_spec,
            tab_spec,
            pl.BlockSpec((1, ATT_WIDTH), lambda i: (0, 0)),
            pl.BlockSpec((1, KV_WIDTH), lambda i: (0, 0)),
            pl.BlockSpec((ATT_WIDTH, ATT_WIDTH), lambda i: (0, 0)),
        ],
        out_specs=[
            pl.BlockSpec((tm, ATT_WIDTH), lambda i: (i, 0)),
            pl.BlockSpec((tm, KV_WIDTH), lambda i: (i, 0)),
            pl.BlockSpec((tm, 2 * KV_WIDTH), lambda i: (i, 0)),
            pl.BlockSpec((ATT_KV_HEADS * VT_ROWS, tm), lambda i: (0, i)),
        ],
        out_shape=[
            jax.ShapeDtypeStruct((t, ATT_WIDTH), BF16),
            jax.ShapeDtypeStruct((t, KV_WIDTH), F32),
            jax.ShapeDtypeStruct((t, 2 * KV_WIDTH), BF16),
            jax.ShapeDtypeStruct((ATT_KV_HEADS * VT_ROWS, t), BF16),
        ],
        compiler_params=_cparams(("parallel",)),
        name="attention_prep",
    )(z, z, cos_tab, sin_tab, q_g, k_g, ones_bd)


def _attn_kernel(*refs, has_ctx):
    if has_ctx:
        q_ref, k_ref, v_ref, ck_ref, cv_ref, o_ref = refs
    else:
        q_ref, k_ref, v_ref, o_ref = refs
    tq = q_ref.shape[0]
    sources = [(k_ref, v_ref)] + ([(ck_ref.at[0], cv_ref.at[0])] if has_ctx else [])
    chunks = []
    for kr, vr in sources:
        n = min(ATT_KEY_CHUNK, kr.shape[0])
        chunks += [(kr, vr, c * n, n) for c in range(kr.shape[0] // n)]
    low = lax.broadcasted_iota(jnp.int32, (tq, LANES), 1) < HEAD_DIM
    zero = jnp.zeros((tq, LANES), BF16)
    n_heads = ATT_HEADS // ATT_KV_HEADS
    ws = []
    for h in range(n_heads):
        q2 = q_ref[:, (h // 2) * LANES:(h // 2 + 1) * LANES]
        ws.append(jnp.where(low, q2, zero) if h % 2 == 0 else jnp.where(low, zero, q2))
    w = jnp.concatenate(ws, axis=0)
    maxes, parts = [], []
    for kr, vr, r0, n in chunks:
        st = _dot_nt(kr[r0:r0 + n, :], w)
        mc = jnp.max(st, axis=0, keepdims=True)
        p = jnp.exp(st - mc).astype(BF16)
        maxes.append(mc)
        parts.append(_dot(vr[0:VT_USED, r0:r0 + n], p))
    m = functools.reduce(jnp.maximum, maxes)
    acc = sum(jnp.exp(mc - m) * part for mc, part in zip(maxes, parts))
    o_t = acc[:HEAD_DIM] / acc[HEAD_DIM:HEAD_DIM + 1]
    o_heads = jnp.concatenate([o_t[:, h * tq:(h + 1) * tq] for h in range(n_heads)], axis=0)
    o_ref[...] = o_heads.T.astype(BF16)


def _attention(q, k2, v2, ctx_k2, ctx_v2, *, row0, n_seq, seq_len, tq):
    t = q.shape[0]
    ntq = seq_len // tq
    gw = ATT_WIDTH // ATT_KV_HEADS
    qblk0 = row0 // tq
    kblk0 = row0 // seq_len
    has_ctx = ctx_k2 is not None
    in_specs = [
        pl.BlockSpec((tq, gw), lambda b, g, i: (qblk0 + b * ntq + i, g)),
        pl.BlockSpec((seq_len, LANES), lambda b, g, i: (kblk0 + b, g)),
        pl.BlockSpec((VT_ROWS, seq_len), lambda b, g, i: (g, kblk0 + b)),
    ]
    args = [q, k2, v2]
    if has_ctx:
        past = ctx_k2.shape[1]
        in_specs += [pl.BlockSpec((1, past, LANES), lambda b, g, i: (b, 0, g)),
                     pl.BlockSpec((1, VT_ROWS, past), lambda b, g, i: (b, g, 0))]
        args += [ctx_k2, ctx_v2]
    return pl.pallas_call(
        functools.partial(_attn_kernel, has_ctx=has_ctx),
        grid=(n_seq, ATT_KV_HEADS, ntq),
        in_specs=in_specs,
        out_specs=pl.BlockSpec((tq, gw), lambda b, g, i: (b * ntq + i, g)),
        out_shape=jax.ShapeDtypeStruct((n_seq * seq_len, ATT_WIDTH), BF16),
        compiler_params=_cparams(("parallel", "parallel", "parallel")),
        name="attention_ctx" if has_ctx else "attention_self",
    )(*args)


def _gla_kernel(q_ref, k_ref, v_ref, lr_ref, aup_ref, ab_ref, s0_ref, o_ref, sfin_ref,
                s_scr, la_scr, vt_scr, ot_scr, *, chunks_per_seq):
    seg = pl.program_id(0)
    direction = pl.program_id(1)
    n = pl.program_id(2)
    c = q_ref.shape[0]
    kw = GLA_K_WIDTH
    fwd = direction == 0

    @pl.when(n % chunks_per_seq == 0)
    def _():
        s_scr[...] = s0_ref[0, 0]

    x = _dot3(lr_ref[...], aup_ref[0]) + ab_ref[0]
    la = (jnp.minimum(x, 0.0) - jnp.log(1.0 + jnp.exp(-jnp.abs(x)))) * (1.0 / GLA_TAU)

    row = lax.broadcasted_iota(jnp.int32, (c, c), 0)
    col = lax.broadcasted_iota(jnp.int32, (c, c), 1)
    tri = jnp.where(fwd, row - col, col - row) >= 0
    tri_b = tri.astype(BF16)
    la_hi, la_mid, la_lo = _split3(la)
    cum = _dot(tri_b, la_hi) + _dot(tri_b, la_mid) + _dot(tri_b, la_lo)
    bend = jnp.sum(la, axis=0, keepdims=True)
    safe = jnp.min(cum) >= -GLA_SAFE_LOG
    head_of_lane = lax.broadcasted_iota(jnp.int32, (1, kw), 1) // GLA_DK

    @pl.when(safe)
    def _():
        q = q_ref[...] * (GLA_DK ** -0.5)
        k = k_ref[...]
        vb = v_ref[...].astype(BF16)
        qt = (q * jnp.exp(cum)).astype(BF16)
        kt = (k * jnp.exp(-cum)).astype(BF16)
        ke = (k * jnp.exp(bend - cum)).astype(BF16)
        s_old = s_scr[...]
        s_b = s_old.astype(BF16)
        s_new = jnp.exp(bend) * s_old
        zero = jnp.zeros_like(qt)
        for h in range(GLA_HEADS):
            hm = head_of_lane == h
            qh = jnp.where(hm, qt, zero)
            a = jnp.where(tri, _dot_nt(qh, kt), 0.0).astype(BF16)
            vh = vb[:, h * GLA_DV:(h + 1) * GLA_DV]
            o_ref[0, :, h * GLA_DV:(h + 1) * GLA_DV] = _dot(a, vh) + _dot_nt(qh, s_b)
            s_new = s_new + _dot_tn(vh, jnp.where(hm, ke, zero))
        s_scr[...] = s_new
        sfin_ref[0, 0, 0] = s_new

    @pl.when(jnp.logical_not(safe))
    def _():
        la_scr[...] = la
        vt_scr[...] = v_ref[...].T
        ot_scr[...] = jnp.zeros_like(ot_scr)
        lane_c = lax.broadcasted_iota(jnp.int32, (1, c), 1)

        def body(i, s):
            t = jnp.where(fwd, i, c - 1 - i)
            a_t = jnp.exp(la_scr[pl.ds(t, 1), :])
            k_t = k_ref[pl.ds(t, 1), :]
            q_t = q_ref[pl.ds(t, 1), :] * (GLA_DK ** -0.5)
            onehot = lane_c == t
            vmat = jnp.zeros((GLA_DV, kw), F32)
            for h in range(GLA_HEADS):
                vcol = jnp.sum(jnp.where(onehot, vt_scr[h * GLA_DV:(h + 1) * GLA_DV, :], 0.0), axis=1, keepdims=True)
                vmat = jnp.where(head_of_lane == h, vcol, vmat)
            s = a_t * s + vmat * k_t
            prod = s * q_t
            for h in range(GLA_HEADS):
                ocol = jnp.sum(jnp.where(head_of_lane == h, prod, 0.0), axis=1, keepdims=True)
                rows = slice(h * GLA_DV, (h + 1) * GLA_DV)
                ot_scr[rows, :] = jnp.where(onehot, ocol, ot_scr[rows, :])
            return s

        s_new = lax.fori_loop(0, c, body, s_scr[...])
        o_ref[0] = ot_scr[...].T
        s_scr[...] = s_new
        sfin_ref[0, 0, 0] = s_new


def _gla(z, lay, aup_pad, ab, s0, *, seg_len, n_seg, prompt_len):
    t = z.shape[0]
    c = GLA_CHUNK
    nc = seg_len // c
    cps_prompt = prompt_len // c
    gq_blk = lay["gq"] // GLA_K_WIDTH
    gk_blk = lay["gk"] // GLA_K_WIDTH
    gv_blk = COL_GV // GLA_V_WIDTH
    lr_blk = lay["lr"] // LANES

    def chunk(seg, d, n):
        return seg * nc + jnp.where(d == 0, n, nc - 1 - n)

    def kernel(*refs):
        cps = jnp.where(pl.program_id(0) == 0, cps_prompt, nc)
        _gla_kernel(*refs, chunks_per_seq=cps)

    return pl.pallas_call(
        kernel,
        grid=(n_seg, 2, nc),
        in_specs=[
            pl.BlockSpec((c, GLA_K_WIDTH), lambda s, d, n: (chunk(s, d, n), gq_blk)),
            pl.BlockSpec((c, GLA_K_WIDTH), lambda s, d, n: (chunk(s, d, n), gk_blk)),
            pl.BlockSpec((c, GLA_V_WIDTH), lambda s, d, n: (chunk(s, d, n), gv_blk)),
            pl.BlockSpec((c, LANES), lambda s, d, n: (chunk(s, d, n), lr_blk)),
            pl.BlockSpec((1, LANES, GLA_K_WIDTH), lambda s, d, n: (d, 0, 0)),
            pl.BlockSpec((1, 1, GLA_K_WIDTH), lambda s, d, n: (d, 0, 0)),
            pl.BlockSpec((1, 1, GLA_DV, GLA_K_WIDTH), lambda s, d, n: (s, d, 0, 0)),
        ],
        out_specs=[
            pl.BlockSpec((1, c, GLA_V_WIDTH), lambda s, d, n: (d, chunk(s, d, n), 0)),
            pl.BlockSpec((1, 1, 1, GLA_DV, GLA_K_WIDTH), lambda s, d, n: (s, d, jnp.where(d == 0, n, nc - 1 - n), 0, 0)),
        ],
        out_shape=[
            jax.ShapeDtypeStruct((2, t, GLA_V_WIDTH), F32),
            jax.ShapeDtypeStruct((n_seg, 2, nc, GLA_DV, GLA_K_WIDTH), F32),
        ],
        scratch_shapes=[
            pltpu.VMEM((GLA_DV, GLA_K_WIDTH), F32),
            pltpu.VMEM((c, GLA_K_WIDTH), F32),
            pltpu.VMEM((GLA_V_WIDTH, c), F32),
            pltpu.VMEM((GLA_V_WIDTH, c), F32),
        ],
        compiler_params=_cparams(("parallel", "parallel", "arbitrary")),
        name="gla_scan",
    )(z, z, z, z, aup_pad, ab, s0)


def _four_chan_kernel(u_ref, w_ref, o_ref):
    o_ref[...] = _dot(u_ref[...].astype(BF16), w_ref[...]).astype(BF16)


def _fourier_channels(z, w_chan):
    t = z.shape[0]
    tm = 512
    return pl.pallas_call(
        _four_chan_kernel,
        grid=(t // tm,),
        in_specs=[
            pl.BlockSpec((tm, FOURIER_WIDTH), lambda i: (i, COL_FU // FOURIER_WIDTH)),
            _resident((FOURIER_WIDTH, 2 * FOURIER_WIDTH), lambda i: (0, 0)),
        ],
        out_specs=pl.BlockSpec((tm, 2 * FOURIER_WIDTH), lambda i: (i, 0)),
        out_shape=jax.ShapeDtypeStruct((t, 2 * FOURIER_WIDTH), BF16),
        compiler_params=_cparams(("parallel",)),
        name="fourier_channels",
    )(z, w_chan)


def _four_pos_kernel(c_ref, s_ref, xc_ref, xs_ref, o_ref):
    o_ref[...] = (_dot(c_ref[...], xc_ref[...]) + _dot(s_ref[...], xs_ref[...])).astype(BF16)


def _fourier_positions(xcs, cos_m, nsin_m, *, row0, n_seq, seq_len):
    tm = min(512, seq_len)
    nt = seq_len // tm
    blk0 = row0 // seq_len
    return pl.pallas_call(
        _four_pos_kernel,
        grid=(n_seq, nt),
        in_specs=[
            pl.BlockSpec((tm, seq_len), lambda b, i: (i, 0)),
            pl.BlockSpec((tm, seq_len), lambda b, i: (i, 0)),
            pl.BlockSpec((seq_len, FOURIER_WIDTH), lambda b, i: (blk0 + b, 0)),
            pl.BlockSpec((seq_len, FOURIER_WIDTH), lambda b, i: (blk0 + b, 1)),
        ],
        out_specs=pl.BlockSpec((tm, FOURIER_WIDTH), lambda b, i: (b * nt + i, 0)),
        out_shape=jax.ShapeDtypeStruct((n_seq * seq_len, FOURIER_WIDTH), BF16),
        compiler_params=_cparams(("parallel", "parallel")),
        name="fourier_positions",
    )(cos_m, nsin_m, xcs, xcs)


def _merge_kernel(att_ref, gof_ref, gob_ref, gr_ref, four_ref, g0_ref, g1_ref, g2_ref, x_ref, mod_ref,
                  gg_ref, wb_ref, wo_ref, o_ref, *, d):
    o = gof_ref[0] + gob_ref[0]
    gg = gg_ref[...]
    parts = []
    for h in range(GLA_HEADS):
        oh = o[:, h * GLA_DV:(h + 1) * GLA_DV]
        ms = jnp.mean(oh * oh, axis=-1, keepdims=True)
        parts.append(oh * lax.rsqrt(ms + EPS) * gg)
    gla = (jnp.concatenate(parts, axis=1) * _silu(gr_ref[...])).astype(BF16)
    merged = (_sigmoid(g0_ref[...]) * _dot(att_ref[...], wb_ref[0])
              + _sigmoid(g1_ref[...]) * _dot(gla, wb_ref[1])
              + _sigmoid(g2_ref[...]) * _dot(four_ref[...], wb_ref[2]))
    mix = _dot(merged.astype(BF16), wo_ref[...])
    o_ref[...] = x_ref[...] + mod_ref[0, :, 2 * d:3 * d] * mix


def _merge(att, gla_o, z, four, x, mod3, gla_g, w_branch, w_out, seg_len):
    t, d = x.shape
    tm = min(512, seg_len)
    tps = seg_len // tm
    bw = att.shape[1]
    gate_blk = COL_GATES // d
    return pl.pallas_call(
        functools.partial(_merge_kernel, d=d),
        grid=(t // tm,),
        in_specs=[
            pl.BlockSpec((tm, bw), lambda i: (i, 0)),
            pl.BlockSpec((1, tm, GLA_V_WIDTH), lambda i: (0, i, 0)),
            pl.BlockSpec((1, tm, GLA_V_WIDTH), lambda i: (1, i, 0)),
            pl.BlockSpec((tm, GLA_V_WIDTH), lambda i: (i, COL_GR // GLA_V_WIDTH)),
            pl.BlockSpec((tm, bw), lambda i: (i, 0)),
            pl.BlockSpec((tm, d), lambda i: (i, gate_blk)),
            pl.BlockSpec((tm, d), lambda i: (i, gate_blk + 1)),
            pl.BlockSpec((tm, d), lambda i: (i, gate_blk + 2)),
            pl.BlockSpec((tm, d), lambda i: (i, 0)),
            pl.BlockSpec((1, 1, 6 * d), lambda i: (i // tps, 0, 0)),
            pl.BlockSpec((1, GLA_DV), lambda i: (0, 0)),
            _resident((N_BRANCHES, bw, d), lambda i: (0, 0, 0)),
            _resident((d, d), lambda i: (0, 0)),
        ],
        out_specs=pl.BlockSpec((tm, d), lambda i: (i, 0)),
        out_shape=jax.ShapeDtypeStruct((t, d), F32),
        compiler_params=_cparams(("parallel",)),
        name="branch_merge",
    )(att, gla_o, gla_o, z, four, z, z, z, x, mod3, gla_g.reshape(1, GLA_DV), w_branch, w_out)


def _ffn_kernel(x_ref, mod_ref, g_ref, w1_ref, w3_ref, w2_ref, o_ref, *, d):
    x = x_ref[...]
    h = _norm_mod(x, g_ref[...], mod_ref[0, :, 3 * d:4 * d], mod_ref[0, :, 4 * d:5 * d]).astype(BF16)
    a = _dot(h, w1_ref[...])
    u = (_silu(a) * _dot(h, w3_ref[...])).astype(BF16)
    o_ref[...] = x + mod_ref[0, :, 5 * d:6 * d] * _dot(u, w2_ref[...])


def _dense_ffn(x, mod3, g, w1, w3, w2, seg_len):
    t, d = x.shape
    dff = w1.shape[1]
    tm = min(512, seg_len)
    tps = seg_len // tm
    return pl.pallas_call(
        functools.partial(_ffn_kernel, d=d),
        grid=(t // tm,),
        in_specs=[
            pl.BlockSpec((tm, d), lambda i: (i, 0)),
            pl.BlockSpec((1, 1, 6 * d), lambda i: (i // tps, 0, 0)),
            pl.BlockSpec((1, d), lambda i: (0, 0)),
            _resident((d, dff), lambda i: (0, 0)),
            _resident((d, dff), lambda i: (0, 0)),
            _resident((dff, d), lambda i: (0, 0)),
        ],
        out_specs=pl.BlockSpec((tm, d), lambda i: (i, 0)),
        out_shape=jax.ShapeDtypeStruct((t, d), F32),
        compiler_params=_cparams(("parallel",)),
        name="dense_ffn",
    )(x, mod3, g.reshape(1, d), w1, w3, w2)


MOE_TILE = 512
ROUTE_E1, ROUTE_E2, ROUTE_R1, ROUTE_R2, ROUTE_W1, ROUTE_W2 = range(6)


def _route_kernel(x_ref, mod_ref, g_ref, rw_ref, h_ref, route_ref, count_ref, carry_scr, *, d):
    @pl.when(pl.program_id(0) == 0)
    def _():
        carry_scr[...] = jnp.zeros_like(carry_scr)

    h = _norm_mod(x_ref[...], g_ref[...], mod_ref[0, :, 3 * d:4 * d], mod_ref[0, :, 4 * d:5 * d])
    h_ref[...] = h
    logits = _dot3(h, rw_ref[...])
    tm = logits.shape[0]
    lane = lax.broadcasted_iota(jnp.int32, logits.shape, 1).astype(F32)
    neg = jnp.float32(-jnp.inf)
    l1 = jnp.where(lane < N_EXPERTS, logits, neg)
    m1 = jnp.max(l1, axis=-1, keepdims=True)
    i1 = jnp.min(jnp.where(l1 == m1, lane, float(LANES)), axis=-1, keepdims=True)
    l2 = jnp.where(lane == i1, neg, l1)
    m2 = jnp.max(l2, axis=-1, keepdims=True)
    i2 = jnp.min(jnp.where(l2 == m2, lane, float(LANES)), axis=-1, keepdims=True)
    e2 = jnp.exp(m2 - m1)
    w1 = 1.0 / (1.0 + e2)
    w2 = e2 * w1
    onehot = jnp.where((lane == i1) | (lane == i2), 1.0, 0.0)
    row = lax.broadcasted_iota(jnp.int32, (tm, tm), 0)
    col = lax.broadcasted_iota(jnp.int32, (tm, tm), 1)
    before = jnp.where(row > col, 1.0, 0.0).astype(BF16)
    rank = _dot(before, onehot.astype(BF16)) + carry_scr[...]
    r1 = jnp.sum(jnp.where(lane == i1, rank, 0.0), axis=-1, keepdims=True)
    r2 = jnp.sum(jnp.where(lane == i2, rank, 0.0), axis=-1, keepdims=True)
    rec = jnp.zeros_like(logits)
    for idx, val in ((ROUTE_E1, i1), (ROUTE_E2, i2), (ROUTE_R1, r1), (ROUTE_R2, r2), (ROUTE_W1, w1), (ROUTE_W2, w2)):
        rec = jnp.where(lane == idx, val, rec)
    route_ref[...] = rec
    total = carry_scr[...] + jnp.sum(onehot, axis=0, keepdims=True)
    carry_scr[...] = total
    count_ref[...] = total


def _route(x, mod3, g, rw_pad, seg_len):
    t, d = x.shape
    tm = min(512, seg_len)
    tps = seg_len // tm
    return pl.pallas_call(
        functools.partial(_route_kernel, d=d),
        grid=(t // tm,),
        in_specs=[
            pl.BlockSpec((tm, d), lambda i: (i, 0)),
            pl.BlockSpec((1, 1, 6 * d), lambda i: (i // tps, 0, 0)),
            pl.BlockSpec((1, d), lambda i: (0, 0)),
            pl.BlockSpec((d, LANES), lambda i: (0, 0)),
        ],
        out_specs=[
            pl.BlockSpec((tm, d), lambda i: (i, 0)),
            pl.BlockSpec((tm, LANES), lambda i: (i, 0)),
            pl.BlockSpec((1, LANES), lambda i: (0, 0)),
        ],
        out_shape=[
            jax.ShapeDtypeStruct((t, d), F32),
            jax.ShapeDtypeStruct((t, LANES), F32),
            jax.ShapeDtypeStruct((1, LANES), F32),
        ],
        scratch_shapes=[pltpu.VMEM((1, LANES), F32)],
        compiler_params=_cparams(("arbitrary",)),
        name="moe_route",
    )(x, mod3, g.reshape(1, d), rw_pad)


DMA_UNROLL = 8


def _start_row_gather(src_hbm, row_of, dst, sem, n):
    def body(i, carry):
        pltpu.make_async_copy(src_hbm.at[pl.ds(row_of(i), 1)], dst.at[pl.ds(i, 1)], sem).start()
        return carry

    lax.fori_loop(0, n, body, 0, unroll=DMA_UNROLL)


def _wait_row_gather(src_hbm, dst, sem, n):
    def body(i, carry):
        pltpu.make_async_copy(src_hbm.at[pl.ds(0, 1)], dst.at[pl.ds(0, 1)], sem).wait()
        return carry

    lax.fori_loop(0, n, body, 0, unroll=DMA_UNROLL)


def _gather_kernel(p1_ref, p2_ref, cs_ref, h_hbm, o_ref, tok_smem, buf, sem):
    j = pl.program_id(0)
    tm = o_ref.shape[0]
    n_tok = p1_ref.shape[0]

    def issue(tile):
        cs = cs_ref[tile]
        slot = tile % 2
        _start_row_gather(h_hbm, lambda i: tok_smem[jnp.minimum(cs + i, 2 * n_tok - 1)], buf.at[slot], sem.at[slot], tm)

    @pl.when(j == 0)
    def _():
        def inv(t, carry):
            tok_smem[p1_ref[t]] = t
            tok_smem[p2_ref[t]] = t
            return carry

        lax.fori_loop(0, n_tok, inv, 0, unroll=DMA_UNROLL)
        issue(0)

    @pl.when(j + 1 < pl.num_programs(0))
    def _():
        issue(j + 1)

    slot = j % 2
    _wait_row_gather(h_hbm, buf.at[slot], sem.at[slot], tm)
    o_ref[...] = buf[slot].astype(BF16)


def _gather_sorted(h, pos1, pos2, tile_cs):
    t, d = h.shape
    nt = tile_cs.shape[0]
    return pl.pallas_call(
        _gather_kernel,
        grid_spec=pltpu.PrefetchScalarGridSpec(
            num_scalar_prefetch=3,
            grid=(nt,),
            in_specs=[pl.BlockSpec(memory_space=pl.ANY)],
            out_specs=pl.BlockSpec((MOE_TILE, d), lambda j, a, b, c: (j, 0)),
            scratch_shapes=[
                pltpu.SMEM((2 * t,), jnp.int32),
                pltpu.VMEM((2, MOE_TILE, d), F32),
                pltpu.SemaphoreType.DMA((2,)),
            ],
        ),
        out_shape=jax.ShapeDtypeStruct((nt * MOE_TILE, d), BF16),
        compiler_params=_cparams(("arbitrary",)),
        name="moe_gather",
    )(pos1, pos2, tile_cs, h)


def _expert_changed(te_ref, j):
    prev = te_ref[jnp.maximum(j - 1, 0)]
    return (j == 0) | (te_ref[j] != prev)


def _expert_up_kernel(te_ref, nv_ref, xs_ref, w1_ref, w3_ref, u_ref, w1b_scr, w3b_scr):
    j = pl.program_id(1)

    @pl.when(_expert_changed(te_ref, j))
    def _():
        w1b_scr[...] = w1_ref[0].astype(BF16)
        w3b_scr[...] = w3_ref[0].astype(BF16)

    @pl.when(j < nv_ref[0])
    def _():
        xs = xs_ref[...]
        a = _dot(xs, w1b_scr[...])
        u_ref[...] = (_silu(a) * _dot(xs, w3b_scr[...])).astype(BF16)

    @pl.when(j >= nv_ref[0])
    def _():
        u_ref[...] = jnp.zeros_like(u_ref)


def _expert_up(xs, tile_expert, n_valid, w1, w3, layer):
    s, d = xs.shape
    dff = w1.shape[3]
    tf = dff // 4 if (dff // 4) % LANES == 0 else dff
    nt = s // MOE_TILE

    def rows(f, j, te, nv):
        return (jnp.minimum(j, nv[0] - 1), 0)

    return pl.pallas_call(
        _expert_up_kernel,
        grid_spec=pltpu.PrefetchScalarGridSpec(
            num_scalar_prefetch=2,
            grid=(dff // tf, nt),
            in_specs=[
                pl.BlockSpec((MOE_TILE, d), rows),
                pl.BlockSpec((None, 1, d, tf), lambda f, j, te, nv: (layer, te[j], 0, f)),
                pl.BlockSpec((None, 1, d, tf), lambda f, j, te, nv: (layer, te[j], 0, f)),
            ],
            out_specs=pl.BlockSpec((MOE_TILE, tf), lambda f, j, te, nv: (j, f)),
            scratch_shapes=[pltpu.VMEM((d, tf), BF16), pltpu.VMEM((d, tf), BF16)],
        ),
        out_shape=jax.ShapeDtypeStruct((s, dff), BF16),
        compiler_params=_cparams(("arbitrary", "arbitrary")),
        name="moe_expert_up",
    )(tile_expert, n_valid, xs, w1, w3)


def _expert_down_kernel(te_ref, nv_ref, u_ref, w2_ref, y_ref, w2b_scr):
    j = pl.program_id(1)

    @pl.when(_expert_changed(te_ref, j))
    def _():
        w2b_scr[...] = w2_ref[0].astype(BF16)

    @pl.when(j < nv_ref[0])
    def _():
        y_ref[...] = _dot(u_ref[...], w2b_scr[...])

    @pl.when(j >= nv_ref[0])
    def _():
        y_ref[...] = jnp.zeros_like(y_ref)


def _expert_down(u, tile_expert, n_valid, w2, layer):
    s, dff = u.shape
    d = w2.shape[3]
    tn = d // 2
    nt = s // MOE_TILE
    return pl.pallas_call(
        _expert_down_kernel,
        grid_spec=pltpu.PrefetchScalarGridSpec(
            num_scalar_prefetch=2,
            grid=(d // tn, nt),
            in_specs=[
                pl.BlockSpec((MOE_TILE, dff), lambda n, j, te, nv: (jnp.minimum(j, nv[0] - 1), 0)),
                pl.BlockSpec((None, 1, dff, tn), lambda n, j, te, nv: (layer, te[j], 0, n)),
            ],
            out_specs=pl.BlockSpec((MOE_TILE, tn), lambda n, j, te, nv: (j, n)),
            scratch_shapes=[pltpu.VMEM((dff, tn), BF16)],
        ),
        out_shape=jax.ShapeDtypeStruct((s, d), F32),
        compiler_params=_cparams(("arbitrary", "arbitrary")),
        name="moe_expert_down",
    )(tile_expert, n_valid, u, w2)


def _combine_kernel(d1_ref, d2_ref, x_ref, mod_ref, route_ref, ys_hbm, o_ref, buf1, buf2, sem, *, d):
    tc = x_ref.shape[0]
    j = pl.program_id(0)

    def issue(tile):
        slot = tile % 2
        base = tile * tc
        _start_row_gather(ys_hbm, lambda i: d1_ref[base + i], buf1.at[slot], sem.at[0, slot], tc)
        _start_row_gather(ys_hbm, lambda i: d2_ref[base + i], buf2.at[slot], sem.at[1, slot], tc)

    @pl.when(j == 0)
    def _():
        issue(0)

    @pl.when(j + 1 < pl.num_programs(0))
    def _():
        issue(j + 1)

    slot = j % 2
    _wait_row_gather(ys_hbm, buf1.at[slot], sem.at[0, slot], tc)
    _wait_row_gather(ys_hbm, buf2.at[slot], sem.at[1, slot], tc)
    route = route_ref[...]
    lane = lax.broadcasted_iota(jnp.int32, route.shape, 1)
    w1 = jnp.sum(jnp.where(lane == ROUTE_W1, route, 0.0), axis=-1, keepdims=True)
    w2 = jnp.sum(jnp.where(lane == ROUTE_W2, route, 0.0), axis=-1, keepdims=True)
    o_ref[...] = x_ref[...] + mod_ref[0, :, 5 * d:6 * d] * (w1 * buf1[slot] + w2 * buf2[slot])


def _combine(x, mod3, route, ys, dest1, dest2, seg_len):
    t, d = x.shape
    tc = 256
    tps = seg_len // tc
    return pl.pallas_call(
        functools.partial(_combine_kernel, d=d),
        grid_spec=pltpu.PrefetchScalarGridSpec(
            num_scalar_prefetch=2,
            grid=(t // tc,),
            in_specs=[
                pl.BlockSpec((tc, d), lambda i, a, b: (i, 0)),
                pl.BlockSpec((1, 1, 6 * d), lambda i, a, b: (i // tps, 0, 0)),
                pl.BlockSpec((tc, LANES), lambda i, a, b: (i, 0)),
                pl.BlockSpec(memory_space=pl.ANY),
            ],
            out_specs=pl.BlockSpec((tc, d), lambda i, a, b: (i, 0)),
            scratch_shapes=[pltpu.VMEM((2, tc, d), F32), pltpu.VMEM((2, tc, d), F32), pltpu.SemaphoreType.DMA((2, 2))],
        ),
        out_shape=jax.ShapeDtypeStruct((t, d), F32),
        compiler_params=_cparams(("arbitrary",)),
        name="moe_combine",
    )(dest1, dest2, x, mod3, route, ys)


def _moe_ffn(x, mod3, g, rw_pad, w1, w3, w2, layer, seg_len):
    t, d = x.shape
    h, route, counts = _route(x, mod3, g, rw_pad, seg_len)
    cnt = counts[0, :N_EXPERTS].astype(jnp.int32)
    padded = (cnt + MOE_TILE - 1) // MOE_TILE * MOE_TILE
    ends = jnp.cumsum(padded)
    offs = ends - padded
    starts = jnp.cumsum(cnt) - cnt
    e1 = route[:, ROUTE_E1].astype(jnp.int32)
    e2 = route[:, ROUTE_E2].astype(jnp.int32)
    r1 = route[:, ROUTE_R1].astype(jnp.int32)
    r2 = route[:, ROUTE_R2].astype(jnp.int32)
    pos1, pos2 = starts[e1] + r1, starts[e2] + r2
    dest1, dest2 = offs[e1] + r1, offs[e2] + r2
    n_tiles = (2 * t) // MOE_TILE + N_EXPERTS
    tile_start = jnp.arange(n_tiles, dtype=jnp.int32) * MOE_TILE
    tile_expert = jnp.minimum(jnp.sum((tile_start[:, None] >= ends[None, :]).astype(jnp.int32), axis=1), N_EXPERTS - 1)
    tile_cs = starts[tile_expert] + (tile_start - offs[tile_expert])
    n_valid = (ends[-1:] // MOE_TILE).astype(jnp.int32)
    xs = _gather_sorted(h, pos1, pos2, tile_cs)
    u = _expert_up(xs, tile_expert, n_valid, w1, w3, layer)
    ys = _expert_down(u, tile_expert, n_valid, w2, layer)
    return _combine(x, mod3, route, ys, dest1, dest2, seg_len)


def _final_norm_kernel(x_ref, g_ref, o_ref):
    x = x_ref[...]
    ms = jnp.mean(x * x, axis=-1, keepdims=True)
    o_ref[...] = x * lax.rsqrt(ms + EPS) * g_ref[...]


def _final_norm(x, g):
    t, d = x.shape
    tm = 512
    return pl.pallas_call(
        _final_norm_kernel,
        grid=(t // tm,),
        in_specs=[pl.BlockSpec((tm, d), lambda i: (i, 0)), pl.BlockSpec((1, d), lambda i: (0, 0))],
        out_specs=pl.BlockSpec((tm, d), lambda i: (i, 0)),
        out_shape=jax.ShapeDtypeStruct((t, d), F32),
        compiler_params=_cparams(("parallel",)),
        name="final_norm",
    )(x, g.reshape(1, d))


def _rope_tables(n_tokens):
    pos = jnp.arange(n_tokens, dtype=jnp.int32)
    rowcol = jnp.stack([pos // GRID_W, pos % GRID_W], axis=-1).astype(F32)
    inv_freq = 1.0 / (ROPE_THETA ** (jnp.arange(ROPE_FREQS, dtype=F32) / ROPE_FREQS))
    ang = rowcol[:, :, None] * inv_freq
    cos = jnp.cos(ang)
    sin = jnp.sin(ang)
    cos_h = jnp.stack([cos, cos], axis=2).reshape(n_tokens, HEAD_DIM)
    sin_h = jnp.stack([-sin, sin], axis=2).reshape(n_tokens, HEAD_DIM)
    cos_t = jnp.tile(cos_h, (1, LANES // HEAD_DIM))
    sin_t = jnp.tile(sin_h, (1, LANES // HEAD_DIM))
    return (jnp.stack([jnp.ones_like(cos_t), cos_t]), jnp.stack([jnp.zeros_like(sin_t), sin_t]))


def _dft_matrices(n):
    n1 = 1 << (int(math.log2(n)) // 2)
    assert n % n1 == 0
    n2 = n // n1
    m = jnp.arange(n, dtype=jnp.int32)[None, :]
    ang_a = ((jnp.arange(n1, dtype=jnp.int32)[:, None] * m) % n1).astype(F32) * (2.0 * math.pi / n1)
    ang_b = ((jnp.arange(n2, dtype=jnp.int32)[:, None] * m) % n).astype(F32) * (2.0 * math.pi / n)
    ca, sa = jnp.cos(ang_a)[:, None, :], jnp.sin(ang_a)[:, None, :]
    cb, sb = jnp.cos(ang_b)[None, :, :], jnp.sin(ang_b)[None, :, :]
    scale = n ** -0.5
    cos_m = ((ca * cb - sa * sb) * scale).reshape(n, n)
    nsin_m = ((sa * cb + ca * sb) * -scale).reshape(n, n)
    return cos_m.astype(BF16), nsin_m.astype(BF16)


def _channel_dft_weight():
    n = FOURIER_GROUP_DIM
    idx = jnp.arange(n, dtype=jnp.int32)
    ang = ((idx[:, None] * idx[None, :]) % n).astype(F32) * (2.0 * math.pi / n)
    eye = jnp.eye(FOURIER_GROUPS, dtype=F32)
    cos_bd = jnp.kron(eye, jnp.cos(ang) * n ** -0.5)
    sin_bd = jnp.kron(eye, jnp.sin(ang) * n ** -0.5)
    return jnp.concatenate([cos_bd, sin_bd], axis=1).astype(BF16)


def _repack_w_in(w_in, lay):
    depth, d, _ = w_in.shape
    widths = (ATT_WIDTH, KV_WIDTH, KV_WIDTH, GLA_K_WIDTH, GLA_K_WIDTH, GLA_V_WIDTH, GLA_V_WIDTH,
              GLA_LOW_RANK, GLA_LOW_RANK, FOURIER_WIDTH, N_BRANCHES * d)
    offs = [0]
    for w in widths:
        offs.append(offs[-1] + w)
    aq, ak, av, gq, gk, gv, gr, lrf, lrb, fu, gates = [w_in[:, :, offs[i]:offs[i + 1]] for i in range(len(widths))]
    used = lay["lr"] + 2 * GLA_LOW_RANK
    pad = jnp.zeros((depth, d, lay["total"] - used), w_in.dtype)
    return jnp.concatenate([aq, gv, gr, fu, gates, gq, gk, ak, av, lrf, lrb, pad], axis=-1).astype(BF16)


def kernel(x_prompt, x_sample, cache_k, cache_v, state_gla, c, c_ctx, ada_w, ada_b, norm1_g, norm2_g, w_in, q_norm_g, k_norm_g, alpha_up, alpha_b, gla_norm_g, w_branch, w_out, ffn_w1, ffn_w3, ffn_w2, router_w, moe_w1, moe_w3, moe_w2, final_g):
    batch, seq, d = x_prompt.shape
    dec_batch, dec_seq, _ = x_sample.shape
    depth = w_in.shape[0]
    past = cache_k.shape[2]
    seg_len = batch * seq
    assert seg_len == dec_seq, "context tokens and each latent request must fill equal segments"
    assert seq % GLA_CHUNK == 0 and dec_seq % GLA_CHUNK == 0
    assert COL_GATES % d == 0
    n_seg = 1 + dec_batch
    t = n_seg * seg_len
    lay = _in_layout(d)

    w_in_p = _repack_w_in(w_in, lay)
    w_branch_b = w_branch.astype(BF16)
    w_out_b = w_out.astype(BF16)
    ffn_w1_b, ffn_w3_b, ffn_w2_b = ffn_w1.astype(BF16), ffn_w3.astype(BF16), ffn_w2.astype(BF16)
    router_pad = jnp.pad(router_w, ((0, 0), (0, 0), (0, LANES - N_EXPERTS)))
    cos_tab, sin_tab = _rope_tables(dec_seq)
    dft_cos_s, dft_nsin_s = _dft_matrices(dec_seq)
    dft_cos_p, dft_nsin_p = _dft_matrices(seq)
    w_chan = _channel_dft_weight()
    ones_bd = jnp.kron(jnp.eye(ATT_HEADS, dtype=F32), jnp.ones((HEAD_DIM, HEAD_DIM), F32)).astype(BF16)
    q_g = jnp.tile(q_norm_g, (1, ATT_HEADS)).reshape(depth, 1, ATT_WIDTH)
    k_g = jnp.tile(k_norm_g, (1, ATT_KV_HEADS)).reshape(depth, 1, KV_WIDTH)
    aup_pad = jnp.zeros((depth, 2, LANES, GLA_K_WIDTH), F32)
    aup_pad = aup_pad.at[:, 0, :GLA_LOW_RANK].set(alpha_up[:, 0])
    aup_pad = aup_pad.at[:, 1, GLA_LOW_RANK:2 * GLA_LOW_RANK].set(alpha_up[:, 1])
    ab = alpha_b.reshape(depth, 2, 1, GLA_K_WIDTH)
    s0_lat = state_gla.transpose(0, 1, 2, 5, 3, 4).reshape(dec_batch, depth, 2, GLA_DV, GLA_K_WIDTH)
    s0_all = jnp.concatenate([jnp.zeros((1,) + s0_lat.shape[1:], F32), s0_lat], axis=0)

    def ctx_keys(x):
        xb = x.astype(BF16)[:, :, :, None, :]
        return jnp.broadcast_to(xb, (dec_batch, past, ATT_KV_HEADS, 2, HEAD_DIM)).reshape(dec_batch, past, 2 * KV_WIDTH)

    def ctx_values_t(x):
        xt = x.astype(BF16).transpose(0, 2, 3, 1)
        return jnp.concatenate([xt, jnp.ones_like(xt)], axis=2).reshape(dec_batch, ATT_KV_HEADS * VT_ROWS, past)

    cond = jnp.zeros((8, d), F32).at[0].set(c_ctx).at[1:1 + dec_batch].set(c)
    mod_all = _modulation(cond, ada_w, ada_b)

    x = jnp.concatenate([x_prompt.reshape(seg_len, d), x_sample.reshape(dec_batch * dec_seq, d)], axis=0)
    new_k, new_v, new_s = [], [], []
    for l in range(depth):
        mod3 = mod_all[l, :n_seg].reshape(n_seg, 1, 6 * d)
        z = _in_projection(x, mod3, norm1_g[l], w_in_p[l], seg_len)
        qn, k_out, k2, vt = _attention_prep(z, lay, cos_tab, sin_tab, q_g[l], k_g[l], ones_bd, seg_len)
        att_p = _attention(qn, k2, vt, None, None, row0=0, n_seq=batch, seq_len=seq, tq=min(ATT_QUERY_TILE, seq))
        att_s = _attention(qn, k2, vt, ctx_keys(cache_k[:, l]), ctx_values_t(cache_v[:, l]),
                           row0=seg_len, n_seq=dec_batch, seq_len=dec_seq, tq=min(ATT_QUERY_TILE, dec_seq))
        att = jnp.concatenate([att_p, att_s], axis=0)
        gla_o, s_fin = _gla(z, lay, aup_pad[l], ab[l], s0_all[:, l], seg_len=seg_len, n_seg=n_seg, prompt_len=seq)
        xcs = _fourier_channels(z, w_chan)
        four_p = _fourier_positions(xcs, dft_cos_p, dft_nsin_p, row0=0, n_seq=batch, seq_len=seq)
        four_s = _fourier_positions(xcs, dft_cos_s, dft_nsin_s, row0=seg_len, n_seq=dec_batch, seq_len=dec_seq)
        four = jnp.concatenate([four_p, four_s], axis=0)
        x = _merge(att, gla_o, z, four, x, mod3, gla_norm_g[l], w_branch_b[l], w_out_b[l], seg_len)
        j = l // 2
        if l % 2 == 0:
            x = _dense_ffn(x, mod3, norm2_g[l], ffn_w1_b[j], ffn_w3_b[j], ffn_w2_b[j], seg_len)
        else:
            x = _moe_ffn(x, mod3, norm2_g[l], router_pad[j], moe_w1, moe_w3, moe_w2, j, seg_len)
        new_k.append(k_out[:seg_len].reshape(batch, seq, ATT_KV_HEADS, HEAD_DIM))
        v_col = lay["akv"] + KV_WIDTH
        new_v.append(z[:seg_len, v_col:v_col + KV_WIDTH].reshape(batch, seq, ATT_KV_HEADS, HEAD_DIM))
        cps = seq // GLA_CHUNK
        sf = s_fin[0, 0, cps - 1::cps]
        sb = s_fin[0, 1, 0::cps]
        st = jnp.stack([sf, sb], axis=1).reshape(batch, 2, GLA_DV, GLA_HEADS, GLA_DK)
        new_s.append(st.transpose(0, 1, 3, 4, 2))
    y = _final_norm(x, final_g)
    y_prompt = y[:seg_len].reshape(batch, seq, d)
    y_sample = y[seg_len:].reshape(dec_batch, dec_seq, d)
    return (y_prompt, y_sample, jnp.stack(new_k, axis=1), jnp.stack(new_v, axis=1), jnp.stack(new_s, axis=1))
```

```python
import functools
import math

import jax
import jax.numpy as jnp
from jax import lax
from jax.experimental import pallas as pl
from jax.experimental.pallas import tpu as pltpu

F32 = jnp.float32
BF16 = jnp.bfloat16

EPS = 1e-6
ATT_HEADS = 8
ATT_KV_HEADS = 2
HEAD_DIM = 64
ATT_WIDTH = ATT_HEADS * HEAD_DIM
KV_WIDTH = ATT_KV_HEADS * HEAD_DIM
ROPE_FREQS = HEAD_DIM // 4
ROPE_THETA = 10000.0
GRID_W = 64
GLA_HEADS = 4
GLA_DK = 64
GLA_DV = 128
GLA_K_WIDTH = GLA_HEADS * GLA_DK
GLA_V_WIDTH = GLA_HEADS * GLA_DV
GLA_LOW_RANK = 16
GLA_TAU = 16.0
FOURIER_GROUPS = 4
FOURIER_GROUP_DIM = 128
FOURIER_WIDTH = FOURIER_GROUPS * FOURIER_GROUP_DIM
N_BRANCHES = 3
N_EXPERTS = 8

LANES = 128
MXU_WIDTH = 256
VMEM_LIMIT_BYTES = 56 * 1024 * 1024

COL_AQ = 0
COL_GV = 512
COL_GR = 1024
COL_FU = 1536
COL_GATES = 2048

GLA_CHUNK = 256
GLA_SAFE_LOG = 60.0


def _cparams(sem):
    return pltpu.CompilerParams(dimension_semantics=sem, vmem_limit_bytes=VMEM_LIMIT_BYTES)


def _resident(shape, index_map):
    return pl.BlockSpec(shape, index_map, pipeline_mode=pl.Buffered(1))


def _split2(x):
    hi = x.astype(BF16)
    lo = (x - hi.astype(F32)).astype(BF16)
    return hi, lo


def _split3(x):
    hi = x.astype(BF16)
    r = x - hi.astype(F32)
    mid = r.astype(BF16)
    lo = (r - mid.astype(F32)).astype(BF16)
    return hi, mid, lo


def _dot(a, b):
    return jnp.dot(a, b, preferred_element_type=F32)


def _dot_nt(a, b):
    return lax.dot_general(a, b, (((1,), (1,)), ((), ())), preferred_element_type=F32)


def _dot_tn(a, b):
    return lax.dot_general(a, b, (((0,), (0,)), ((), ())), preferred_element_type=F32)


def _dot3(a, b):
    a_hi, a_lo = _split2(a)
    b_hi, b_lo = _split2(b)
    return _dot(a_hi, b_hi) + _dot(a_lo, b_hi) + _dot(a_hi, b_lo)


def _sigmoid(x):
    return 1.0 / (1.0 + jnp.exp(-x))


def _silu(x):
    return x * _sigmoid(x)


def _in_layout(d_model):
    gates_end = COL_GATES + N_BRANCHES * d_model
    col_gq = gates_end
    col_gk = col_gq + GLA_K_WIDTH
    col_akv = col_gk + GLA_K_WIDTH
    col_lr = col_akv + 2 * KV_WIDTH
    used = col_lr + LANES
    total = -(-used // 1024) * 1024
    return dict(gq=col_gq, gk=col_gk, akv=col_akv, lr=col_lr, total=total)


def _mod_kernel(cond_ref, w_ref, b_ref, o_ref):
    c = _silu(cond_ref[...])
    o_ref[0] = _dot3(c, w_ref[0]) + b_ref[0]


def _modulation(cond, ada_w, ada_b):
    depth, d, n6 = ada_w.shape
    tn = 1536 if n6 % 1536 == 0 else n6
    rows = cond.shape[0]
    return pl.pallas_call(
        _mod_kernel,
        grid=(depth, n6 // tn),
        in_specs=[
            pl.BlockSpec((rows, d), lambda l, j: (0, 0)),
            pl.BlockSpec((1, d, tn), lambda l, j: (l, 0, j)),
            pl.BlockSpec((1, 1, tn), lambda l, j: (l, 0, j)),
        ],
        out_specs=pl.BlockSpec((1, rows, tn), lambda l, j: (l, 0, j)),
        out_shape=jax.ShapeDtypeStruct((depth, rows, n6), F32),
        compiler_params=_cparams(("parallel", "parallel")),
        name="adaln_mod",
    )(cond, ada_w, ada_b.reshape(depth, 1, n6))


def _norm_mod(x, g, shift, scale):
    ms = jnp.mean(x * x, axis=-1, keepdims=True)
    return x * lax.rsqrt(ms + EPS) * g * (1.0 + scale) + shift


def _inproj_kernel(x_ref, mod_ref, g_ref, w_ref, z_ref, h_scr, *, d):
    @pl.when(pl.program_id(1) == 0)
    def _():
        h = _norm_mod(x_ref[...], g_ref[...], mod_ref[0, :, 0:d], mod_ref[0, :, d:2 * d])
        h_scr[...] = h.astype(BF16)

    z_ref[...] = _dot(h_scr[...], w_ref[...]).astype(z_ref.dtype)


def _in_projection(x, mod3, g, w, seg_len):
    t, d = x.shape
    n = w.shape[1]
    tm = min(1024, seg_len)
    tn = 1024
    tps = seg_len // tm
    return pl.pallas_call(
        functools.partial(_inproj_kernel, d=d),
        grid=(t // tm, n // tn),
        in_specs=[
            pl.BlockSpec((tm, d), lambda i, j: (i, 0)),
            pl.BlockSpec((1, 1, 6 * d), lambda i, j: (i // tps, 0, 0)),
            pl.BlockSpec((1, d), lambda i, j: (0, 0)),
            pl.BlockSpec((d, tn), lambda i, j: (0, j)),
        ],
        out_specs=pl.BlockSpec((tm, tn), lambda i, j: (i, j)),
        out_shape=jax.ShapeDtypeStruct((t, n), BF16),
        scratch_shapes=[pltpu.VMEM((tm, d), BF16)],
        compiler_params=_cparams(("parallel", "arbitrary")),
        name="in_projection",
    )(x, mod3, g.reshape(1, d), w)


def _head_rms(x, g, ones_bd):
    hi, lo = _split2(x * x)
    ss = _dot(hi, ones_bd) + _dot(lo, ones_bd)
    return x * lax.rsqrt(ss * (1.0 / HEAD_DIM) + EPS) * g


def _rope(x, cos, sin_signed):
    width = x.shape[1]
    reps = width // LANES
    if reps > 1:
        cos = jnp.concatenate([cos] * reps, axis=1)
        sin_signed = jnp.concatenate([sin_signed] * reps, axis=1)
    lane = lax.broadcasted_iota(jnp.int32, x.shape, 1)
    first_half = (lane % (2 * ROPE_FREQS)) < ROPE_FREQS
    partner = jnp.where(first_half, pltpu.roll(x, width - ROPE_FREQS, 1), pltpu.roll(x, ROPE_FREQS, 1))
    return x * cos + partner * sin_signed


ATT_KEY_CHUNK = 512
ATT_QUERY_TILE = 256
VT_ROWS = 2 * HEAD_DIM
VT_USED = HEAD_DIM + 16


def _dup_heads(x):
    lane = lax.broadcasted_iota(jnp.int32, x.shape, 1)
    swapped = pltpu.roll(x, HEAD_DIM, 1)
    low = lane < HEAD_DIM
    return jnp.concatenate([jnp.where(low, x, swapped), jnp.where(low, swapped, x)], axis=1)


def _attn_prep_kernel(zq_ref, zkv_ref, cos_ref, sin_ref, qg_ref, kg_ref, bd_ref, q_ref, kout_ref, k2_ref, vt_ref):
    cos = cos_ref[0]
    sin = sin_ref[0]
    bd = bd_ref[...]
    q = _head_rms(zq_ref[...].astype(F32), qg_ref[...], bd)
    q_ref[...] = (_rope(q, cos, sin) * (HEAD_DIM ** -0.5)).astype(BF16)
    kv = zkv_ref[...].astype(F32)
    k = _head_rms(kv[:, :KV_WIDTH], kg_ref[...], bd[:KV_WIDTH, :KV_WIDTH])
    kout_ref[...] = k
    k2_ref[...] = _dup_heads(_rope(k, cos, sin)).astype(BF16)
    vt = kv[:, KV_WIDTH:].T
    ones = jnp.ones((HEAD_DIM, vt.shape[1]), F32)
    vt_ref[...] = jnp.concatenate([vt[:HEAD_DIM], ones, vt[HEAD_DIM:], ones], axis=0).astype(BF16)


def _attention_prep(z, lay, cos_tab, sin_tab, q_g, k_g, ones_bd, seg_len):
    t = z.shape[0]
    tm = min(512, seg_len)
    tps = seg_len // tm
    akv_blk = lay["akv"] // (2 * KV_WIDTH)
    tab_spec = pl.BlockSpec((1, tm, LANES), lambda i: (jnp.minimum(i // tps, 1), i % tps, 0))
    return pl.pallas_call(
        _attn_prep_kernel,
        grid=(t // tm,),
        in_specs=[
            pl.BlockSpec((tm, ATT_WIDTH), lambda i: (i, COL_AQ // ATT_WIDTH)),
            pl.BlockSpec((tm, 2 * KV_WIDTH), lambda i: (i, akv_blk)),
            tab_spec,
            tab_spec,
            pl.BlockSpec((1, ATT_WIDTH), lambda i: (0, 0)),
            pl.BlockSpec((1, KV_WIDTH), lambda i: (0, 0)),
            pl.BlockSpec((ATT_WIDTH, ATT_WIDTH), lambda i: (0, 0)),
        ],
        out_specs=[
            pl.BlockSpec((tm, ATT_WIDTH), lambda i: (i, 0)),
            pl.BlockSpec((tm, KV_WIDTH), lambda i: (i, 0)),
            pl.BlockSpec((tm, 2 * KV_WIDTH), lambda i: (i, 0)),
            pl.BlockSpec((ATT_KV_HEADS * VT_ROWS, tm), lambda i: (0, i)),
        ],
        out_shape=[
            jax.ShapeDtypeStruct((t, ATT_WIDTH), BF16),
            jax.ShapeDtypeStruct((t, KV_WIDTH), F32),
            jax.ShapeDtypeStruct((t, 2 * KV_WIDTH), BF16),
            jax.ShapeDtypeStruct((ATT_KV_HEADS * VT_ROWS, t), BF16),
        ],
        compiler_params=_cparams(("parallel",)),
        name="attention_prep",
    )(z, z, cos_tab, sin_tab, q_g, k_g, ones_bd)


def _attn_kernel(*refs, has_ctx):
    if has_ctx:
        q_ref, k_ref, v_ref, ck_ref, cv_ref, o_ref = refs
    else:
        q_ref, k_ref, v_ref, o_ref = refs
    tq = q_ref.shape[0]
    sources = [(k_ref, v_ref)] + ([(ck_ref.at[0], cv_ref.at[0])] if has_ctx else [])
    chunks = []
    for kr, vr in sources:
        n = min(ATT_KEY_CHUNK, kr.shape[0])
        chunks += [(kr, vr, c * n, n) for c in range(kr.shape[0] // n)]
    low = lax.broadcasted_iota(jnp.int32, (tq, LANES), 1) < HEAD_DIM
    zero = jnp.zeros((tq, LANES), BF16)
    n_heads = ATT_HEADS // ATT_KV_HEADS
    ws = []
    for h in range(n_heads):
        q2 = q_ref[:, (h // 2) * LANES:(h // 2 + 1) * LANES]
        ws.append(jnp.where(low, q2, zero) if h % 2 == 0 else jnp.where(low, zero, q2))
    w = jnp.concatenate(ws, axis=0)
    maxes, parts = [], []
    scores = lambda c: _dot_nt(c[0][c[2]:c[2] + c[3], :], w)
    st_next = scores(chunks[0])
    for idx, (kr, vr, r0, n) in enumerate(chunks):
        st = st_next
        if idx + 1 < len(chunks):
            st_next = scores(chunks[idx + 1])
        mc = jnp.max(st, axis=0, keepdims=True)
        p = jnp.exp(st - mc).astype(BF16)
        maxes.append(mc)
        parts.append(_dot(vr[0:VT_USED, r0:r0 + n], p))
    m = functools.reduce(jnp.maximum, maxes)
    acc = sum(jnp.exp(mc - m) * part for mc, part in zip(maxes, parts))
    o_t = acc[:HEAD_DIM] / acc[HEAD_DIM:HEAD_DIM + 1]
    o_heads = jnp.concatenate([o_t[:, h * tq:(h + 1) * tq] for h in range(n_heads)], axis=0)
    o_ref[...] = o_heads.T.astype(BF16)


def _attention(q, k2, v2, ctx_k2, ctx_v2, *, row0, n_seq, seq_len, tq):
    ntq = seq_len // tq
    gw = ATT_WIDTH // ATT_KV_HEADS
    qblk0 = row0 // tq
    kblk0 = row0 // seq_len
    has_ctx = ctx_k2 is not None
    in_specs = [
        pl.BlockSpec((tq, gw), lambda b, g, i: (qblk0 + b * ntq + i, g)),
        pl.BlockSpec((seq_len, LANES), lambda b, g, i: (kblk0 + b, g)),
        pl.BlockSpec((VT_ROWS, seq_len), lambda b, g, i: (g, kblk0 + b)),
    ]
    args = [q, k2, v2]
    if has_ctx:
        past = ctx_k2.shape[1]
        in_specs += [pl.BlockSpec((1, past, LANES), lambda b, g, i: (b, 0, g)),
                     pl.BlockSpec((1, VT_ROWS, past), lambda b, g, i: (b, g, 0))]
        args += [ctx_k2, ctx_v2]
    return pl.pallas_call(
        functools.partial(_attn_kernel, has_ctx=has_ctx),
        grid=(n_seq, ATT_KV_HEADS, ntq),
        in_specs=in_specs,
        out_specs=pl.BlockSpec((tq, gw), lambda b, g, i: (b * ntq + i, g)),
        out_shape=jax.ShapeDtypeStruct((n_seq * seq_len, ATT_WIDTH), BF16),
        compiler_params=_cparams(("parallel", "parallel", "parallel")),
        name="attention_ctx" if has_ctx else "attention_self",
    )(*args)


def _gla_kernel(q_ref, k_ref, v_ref, lr_ref, aup_ref, ab_ref, s0_ref, o_ref, sfin_ref,
                s_scr, la_scr, qf_scr, kf_scr, vt_scr, ot_scr, *, chunks_per_seq):
    seg = pl.program_id(0)
    direction = pl.program_id(1)
    n = pl.program_id(2)
    c = q_ref.shape[0]
    kw = GLA_K_WIDTH
    fwd = direction == 0

    @pl.when(n % chunks_per_seq == 0)
    def _():
        s_scr[...] = s0_ref[0, 0]

    aup_hi, aup_lo = _split2(aup_ref[0])
    lr = lr_ref[...]
    x = _dot(lr, aup_hi) + _dot(lr, aup_lo) + ab_ref[0]
    la = (jnp.minimum(x, 0.0) - jnp.log(1.0 + jnp.exp(-jnp.abs(x)))) * (1.0 / GLA_TAU)

    row = lax.broadcasted_iota(jnp.int32, (c, c), 0)
    col = lax.broadcasted_iota(jnp.int32, (c, c), 1)
    tri = jnp.where(fwd, row - col, col - row) >= 0
    tri_b = tri.astype(BF16)
    la_hi, la_mid, la_lo = _split3(la)
    cum = _dot(tri_b, la_hi) + _dot(tri_b, la_mid) + _dot(tri_b, la_lo)
    bend = jnp.sum(la, axis=0, keepdims=True)
    safe = jnp.min(cum) >= -GLA_SAFE_LOG
    head_of_lane = lax.broadcasted_iota(jnp.int32, (1, kw), 1) // GLA_DK

    @pl.when(safe)
    def _():
        q = q_ref[...].astype(F32) * (GLA_DK ** -0.5)
        k = k_ref[...].astype(F32)
        vb = v_ref[...]
        qt = (q * jnp.exp(cum)).astype(BF16)
        kt = (k * jnp.exp(-cum)).astype(BF16)
        ke = (k * jnp.exp(bend - cum)).astype(BF16)
        s_old = s_scr[...]
        s_b = s_old.astype(BF16)
        s_new = jnp.exp(bend) * s_old
        zero = jnp.zeros_like(qt)
        for h in range(GLA_HEADS):
            hm = head_of_lane == h
            qh = jnp.where(hm, qt, zero)
            a = jnp.where(tri, _dot_nt(qh, kt), 0.0).astype(BF16)
            vh = vb[:, h * GLA_DV:(h + 1) * GLA_DV]
            o_ref[0, :, h * GLA_DV:(h + 1) * GLA_DV] = _dot(a, vh) + _dot_nt(qh, s_b)
            s_new = s_new + _dot_tn(vh, jnp.where(hm, ke, zero))
        s_scr[...] = s_new
        sfin_ref[0, 0, 0] = s_new

    @pl.when(jnp.logical_not(safe))
    def _():
        la_scr[...] = la
        qf_scr[...] = q_ref[...].astype(F32) * (GLA_DK ** -0.5)
        kf_scr[...] = k_ref[...].astype(F32)
        vt_scr[...] = v_ref[...].astype(F32).T
        ot_scr[...] = jnp.zeros_like(ot_scr)
        lane_c = lax.broadcasted_iota(jnp.int32, (1, c), 1)

        def body(i, s):
            t = jnp.where(fwd, i, c - 1 - i)
            a_t = jnp.exp(la_scr[pl.ds(t, 1), :])
            k_t = kf_scr[pl.ds(t, 1), :]
            q_t = qf_scr[pl.ds(t, 1), :]
            onehot = lane_c == t
            vmat = jnp.zeros((GLA_DV, kw), F32)
            for h in range(GLA_HEADS):
                vcol = jnp.sum(jnp.where(onehot, vt_scr[h * GLA_DV:(h + 1) * GLA_DV, :], 0.0), axis=1, keepdims=True)
                vmat = jnp.where(head_of_lane == h, vcol, vmat)
            s = a_t * s + vmat * k_t
            prod = s * q_t
            for h in range(GLA_HEADS):
                ocol = jnp.sum(jnp.where(head_of_lane == h, prod, 0.0), axis=1, keepdims=True)
                rows = slice(h * GLA_DV, (h + 1) * GLA_DV)
                ot_scr[rows, :] = jnp.where(onehot, ocol, ot_scr[rows, :])
            return s

        s_new = lax.fori_loop(0, c, body, s_scr[...])
        o_ref[0] = ot_scr[...].T
        s_scr[...] = s_new
        sfin_ref[0, 0, 0] = s_new


def _gla(z, lay, aup_pad, ab, s0, *, seg_len, n_seg, prompt_len):
    t = z.shape[0]
    c = GLA_CHUNK
    nc = seg_len // c
    cps_prompt = prompt_len // c
    gq_blk = lay["gq"] // GLA_K_WIDTH
    gk_blk = lay["gk"] // GLA_K_WIDTH
    gv_blk = COL_GV // GLA_V_WIDTH
    lr_blk = lay["lr"] // LANES

    def chunk(seg, d, n):
        return seg * nc + jnp.where(d == 0, n, nc - 1 - n)

    def kernel(*refs):
        cps = jnp.where(pl.program_id(0) == 0, cps_prompt, nc)
        _gla_kernel(*refs, chunks_per_seq=cps)

    return pl.pallas_call(
        kernel,
        grid=(n_seg, 2, nc),
        in_specs=[
            pl.BlockSpec((c, GLA_K_WIDTH), lambda s, d, n: (chunk(s, d, n), gq_blk)),
            pl.BlockSpec((c, GLA_K_WIDTH), lambda s, d, n: (chunk(s, d, n), gk_blk)),
            pl.BlockSpec((c, GLA_V_WIDTH), lambda s, d, n: (chunk(s, d, n), gv_blk)),
            pl.BlockSpec((c, LANES), lambda s, d, n: (chunk(s, d, n), lr_blk)),
            pl.BlockSpec((1, LANES, GLA_K_WIDTH), lambda s, d, n: (d, 0, 0)),
            pl.BlockSpec((1, 1, GLA_K_WIDTH), lambda s, d, n: (d, 0, 0)),
            pl.BlockSpec((1, 1, GLA_DV, GLA_K_WIDTH), lambda s, d, n: (s, d, 0, 0)),
        ],
        out_specs=[
            pl.BlockSpec((1, c, GLA_V_WIDTH), lambda s, d, n: (d, chunk(s, d, n), 0)),
            pl.BlockSpec((1, 1, 1, GLA_DV, GLA_K_WIDTH), lambda s, d, n: (s, d, jnp.where(d == 0, n, nc - 1 - n), 0, 0)),
        ],
        out_shape=[
            jax.ShapeDtypeStruct((2, t, GLA_V_WIDTH), F32),
            jax.ShapeDtypeStruct((n_seg, 2, nc, GLA_DV, GLA_K_WIDTH), F32),
        ],
        scratch_shapes=[
            pltpu.VMEM((GLA_DV, GLA_K_WIDTH), F32),
            pltpu.VMEM((c, GLA_K_WIDTH), F32),
            pltpu.VMEM((c, GLA_K_WIDTH), F32),
            pltpu.VMEM((c, GLA_K_WIDTH), F32),
            pltpu.VMEM((GLA_V_WIDTH, c), F32),
            pltpu.VMEM((GLA_V_WIDTH, c), F32),
        ],
        compiler_params=_cparams(("parallel", "parallel", "arbitrary")),
        name="gla_scan",
    )(z, z, z, z, aup_pad, ab, s0)


def _four_chan_kernel(u_ref, w_ref, o_ref):
    o_ref[...] = _dot(u_ref[...], w_ref[...]).astype(BF16)


def _fourier_channels(z, w_chan):
    t = z.shape[0]
    tm = 512
    return pl.pallas_call(
        _four_chan_kernel,
        grid=(t // tm,),
        in_specs=[
            pl.BlockSpec((tm, FOURIER_WIDTH), lambda i: (i, COL_FU // FOURIER_WIDTH)),
            _resident((FOURIER_WIDTH, 2 * FOURIER_WIDTH), lambda i: (0, 0)),
        ],
        out_specs=pl.BlockSpec((tm, 2 * FOURIER_WIDTH), lambda i: (i, 0)),
        out_shape=jax.ShapeDtypeStruct((t, 2 * FOURIER_WIDTH), BF16),
        compiler_params=_cparams(("parallel",)),
        name="fourier_channels",
    )(z, w_chan)


def _four_pos_kernel(c_ref, s_ref, xc_ref, xs_ref, o_ref):
    o_ref[...] = (_dot(c_ref[...], xc_ref[...]) + _dot(s_ref[...], xs_ref[...])).astype(BF16)


def _fourier_positions(xcs, cos_m, nsin_m, *, row0, n_seq, seq_len):
    tm = min(512, seq_len)
    nt = seq_len // tm
    blk0 = row0 // seq_len
    return pl.pallas_call(
        _four_pos_kernel,
        grid=(n_seq, nt),
        in_specs=[
            pl.BlockSpec((tm, seq_len), lambda b, i: (i, 0)),
            pl.BlockSpec((tm, seq_len), lambda b, i: (i, 0)),
            pl.BlockSpec((seq_len, FOURIER_WIDTH), lambda b, i: (blk0 + b, 0)),
            pl.BlockSpec((seq_len, FOURIER_WIDTH), lambda b, i: (blk0 + b, 1)),
        ],
        out_specs=pl.BlockSpec((tm, FOURIER_WIDTH), lambda b, i: (b * nt + i, 0)),
        out_shape=jax.ShapeDtypeStruct((n_seq * seq_len, FOURIER_WIDTH), BF16),
        compiler_params=_cparams(("parallel", "parallel")),
        name="fourier_positions",
    )(cos_m, nsin_m, xcs, xcs)


def _merge_kernel(attp_ref, atts_ref, gof_ref, gob_ref, gr_ref, fourp_ref, fours_ref, g0_ref, g1_ref, g2_ref, x_ref,
                  mod_ref, gg_ref, wb_ref, wo_ref, o_ref, *, d, prompt_tiles):
    in_prompt = pl.program_id(0) < prompt_tiles
    att = jnp.where(in_prompt, attp_ref[...], atts_ref[...])
    four = jnp.where(in_prompt, fourp_ref[...], fours_ref[...])
    o = gof_ref[0] + gob_ref[0]
    gg = gg_ref[...]
    parts = []
    for h in range(GLA_HEADS):
        oh = o[:, h * GLA_DV:(h + 1) * GLA_DV]
        ms = jnp.mean(oh * oh, axis=-1, keepdims=True)
        parts.append(oh * lax.rsqrt(ms + EPS) * gg)
    gla = (jnp.concatenate(parts, axis=1) * _silu(gr_ref[...].astype(F32))).astype(BF16)
    merged = (_sigmoid(g0_ref[...].astype(F32)) * _dot(att, wb_ref[0])
              + _sigmoid(g1_ref[...].astype(F32)) * _dot(gla, wb_ref[1])
              + _sigmoid(g2_ref[...].astype(F32)) * _dot(four, wb_ref[2]))
    mix = _dot(merged.astype(BF16), wo_ref[...])
    o_ref[...] = x_ref[...] + mod_ref[0, :, 2 * d:3 * d] * mix


def _merge(att_p, att_s, gla_o, z, four_p, four_s, x, mod3, gla_g, w_branch, w_out, seg_len):
    t, d = x.shape
    tm = min(512, seg_len)
    tps = seg_len // tm
    bw = att_p.shape[1]
    gate_blk = COL_GATES // d
    prompt_tiles = att_p.shape[0] // tm
    p_spec = pl.BlockSpec((tm, bw), lambda i: (jnp.minimum(i, prompt_tiles - 1), 0))
    s_spec = pl.BlockSpec((tm, bw), lambda i: (jnp.maximum(i - prompt_tiles, 0), 0))
    return pl.pallas_call(
        functools.partial(_merge_kernel, d=d, prompt_tiles=prompt_tiles),
        grid=(t // tm,),
        in_specs=[
            p_spec,
            s_spec,
            pl.BlockSpec((1, tm, GLA_V_WIDTH), lambda i: (0, i, 0)),
            pl.BlockSpec((1, tm, GLA_V_WIDTH), lambda i: (1, i, 0)),
            pl.BlockSpec((tm, GLA_V_WIDTH), lambda i: (i, COL_GR // GLA_V_WIDTH)),
            p_spec,
            s_spec,
            pl.BlockSpec((tm, d), lambda i: (i, gate_blk)),
            pl.BlockSpec((tm, d), lambda i: (i, gate_blk + 1)),
            pl.BlockSpec((tm, d), lambda i: (i, gate_blk + 2)),
            pl.BlockSpec((tm, d), lambda i: (i, 0)),
            pl.BlockSpec((1, 1, 6 * d), lambda i: (i // tps, 0, 0)),
            pl.BlockSpec((1, GLA_DV), lambda i: (0, 0)),
            _resident((N_BRANCHES, bw, d), lambda i: (0, 0, 0)),
            _resident((d, d), lambda i: (0, 0)),
        ],
        out_specs=pl.BlockSpec((tm, d), lambda i: (i, 0)),
        out_shape=jax.ShapeDtypeStruct((t, d), F32),
        compiler_params=_cparams(("parallel",)),
        name="branch_merge",
    )(att_p, att_s, gla_o, gla_o, z, four_p, four_s, z, z, z, x, mod3, gla_g.reshape(1, GLA_DV), w_branch, w_out)


def _ffn_kernel(x_ref, mod_ref, g_ref, w1_ref, w3_ref, w2_ref, o_ref, *, d):
    x = x_ref[...]
    h = _norm_mod(x, g_ref[...], mod_ref[0, :, 3 * d:4 * d], mod_ref[0, :, 4 * d:5 * d]).astype(BF16)
    a = _dot(h, w1_ref[...])
    u = (_silu(a) * _dot(h, w3_ref[...])).astype(BF16)
    o_ref[...] = x + mod_ref[0, :, 5 * d:6 * d] * _dot(u, w2_ref[...])


def _dense_ffn(x, mod3, g, w1, w3, w2, seg_len):
    t, d = x.shape
    dff = w1.shape[1]
    tm = min(512, seg_len)
    tps = seg_len // tm
    return pl.pallas_call(
        functools.partial(_ffn_kernel, d=d),
        grid=(t // tm,),
        in_specs=[
            pl.BlockSpec((tm, d), lambda i: (i, 0)),
            pl.BlockSpec((1, 1, 6 * d), lambda i: (i // tps, 0, 0)),
            pl.BlockSpec((1, d), lambda i: (0, 0)),
            _resident((d, dff), lambda i: (0, 0)),
            _resident((d, dff), lambda i: (0, 0)),
            _resident((dff, d), lambda i: (0, 0)),
        ],
        out_specs=pl.BlockSpec((tm, d), lambda i: (i, 0)),
        out_shape=jax.ShapeDtypeStruct((t, d), F32),
        compiler_params=_cparams(("parallel",)),
        name="dense_ffn",
    )(x, mod3, g.reshape(1, d), w1, w3, w2)


MOE_TILE = 512
MOE_UP_CHUNK = 1024
ROUTE_E1, ROUTE_E2, ROUTE_R1, ROUTE_R2, ROUTE_W1, ROUTE_W2 = range(6)


def _route_kernel(x_ref, mod_ref, g_ref, rw_ref, h_ref, route_ref, count_ref, carry_scr, *, d):
    @pl.when(pl.program_id(0) == 0)
    def _():
        carry_scr[...] = jnp.zeros_like(carry_scr)

    h = _norm_mod(x_ref[...], g_ref[...], mod_ref[0, :, 3 * d:4 * d], mod_ref[0, :, 4 * d:5 * d])
    h_ref[...] = h
    logits = _dot3(h, rw_ref[...])
    tm = logits.shape[0]
    lane = lax.broadcasted_iota(jnp.int32, logits.shape, 1).astype(F32)
    neg = jnp.float32(-jnp.inf)
    l1 = jnp.where(lane < N_EXPERTS, logits, neg)
    m1 = jnp.max(l1, axis=-1, keepdims=True)
    i1 = jnp.min(jnp.where(l1 == m1, lane, float(LANES)), axis=-1, keepdims=True)
    l2 = jnp.where(lane == i1, neg, l1)
    m2 = jnp.max(l2, axis=-1, keepdims=True)
    i2 = jnp.min(jnp.where(l2 == m2, lane, float(LANES)), axis=-1, keepdims=True)
    e2 = jnp.exp(m2 - m1)
    w1 = 1.0 / (1.0 + e2)
    w2 = e2 * w1
    onehot = jnp.where((lane == i1) | (lane == i2), 1.0, 0.0)
    row = lax.broadcasted_iota(jnp.int32, (tm, tm), 0)
    col = lax.broadcasted_iota(jnp.int32, (tm, tm), 1)
    before = jnp.where(row > col, 1.0, 0.0).astype(BF16)
    rank = _dot(before, onehot.astype(BF16)) + carry_scr[...]
    r1 = jnp.sum(jnp.where(lane == i1, rank, 0.0), axis=-1, keepdims=True)
    r2 = jnp.sum(jnp.where(lane == i2, rank, 0.0), axis=-1, keepdims=True)
    rec = jnp.zeros_like(logits)
    for idx, val in ((ROUTE_E1, i1), (ROUTE_E2, i2), (ROUTE_R1, r1), (ROUTE_R2, r2), (ROUTE_W1, w1), (ROUTE_W2, w2)):
        rec = jnp.where(lane == idx, val, rec)
    route_ref[...] = rec
    total = carry_scr[...] + jnp.sum(onehot, axis=0, keepdims=True)
    carry_scr[...] = total
    count_ref[...] = total


def _route(x, mod3, g, rw_pad, seg_len):
    t, d = x.shape
    tm = min(512, seg_len)
    tps = seg_len // tm
    return pl.pallas_call(
        functools.partial(_route_kernel, d=d),
        grid=(t // tm,),
        in_specs=[
            pl.BlockSpec((tm, d), lambda i: (i, 0)),
            pl.BlockSpec((1, 1, 6 * d), lambda i: (i // tps, 0, 0)),
            pl.BlockSpec((1, d), lambda i: (0, 0)),
            pl.BlockSpec((d, LANES), lambda i: (0, 0)),
        ],
        out_specs=[
            pl.BlockSpec((tm, d), lambda i: (i, 0)),
            pl.BlockSpec((tm, LANES), lambda i: (i, 0)),
            pl.BlockSpec((1, LANES), lambda i: (0, 0)),
        ],
        out_shape=[
            jax.ShapeDtypeStruct((t, d), F32),
            jax.ShapeDtypeStruct((t, LANES), F32),
            jax.ShapeDtypeStruct((1, LANES), F32),
        ],
        scratch_shapes=[pltpu.VMEM((1, LANES), F32)],
        compiler_params=_cparams(("arbitrary",)),
        name="moe_route",
    )(x, mod3, g.reshape(1, d), rw_pad)


DMA_UNROLL = 8


def _start_row_gather(src_hbm, row_of, dst, sem, n):
    def body(i, carry):
        pltpu.make_async_copy(src_hbm.at[pl.ds(row_of(i), 1)], dst.at[pl.ds(i, 1)], sem).start()
        return carry

    lax.fori_loop(0, n, body, 0, unroll=DMA_UNROLL)


def _wait_row_gather(src_hbm, dst, sem, n):
    def body(i, carry):
        pltpu.make_async_copy(src_hbm.at[pl.ds(0, 1)], dst.at[pl.ds(0, 1)], sem).wait()
        return carry

    lax.fori_loop(0, n, body, 0, unroll=DMA_UNROLL)


def _gather_kernel(p1_ref, p2_ref, cs_ref, nv_ref, h_hbm, o_ref, tok_smem, buf, sem):
    j = pl.program_id(0)
    tm = o_ref.shape[0]
    n_tok = p1_ref.shape[0]
    n_valid = nv_ref[0]

    def issue(tile):
        cs = cs_ref[tile]
        slot = tile % 2
        _start_row_gather(h_hbm, lambda i: tok_smem[jnp.minimum(cs + i, 2 * n_tok - 1)], buf.at[slot], sem.at[slot], tm)

    @pl.when(j == 0)
    def _():
        def inv(t, carry):
            tok_smem[p1_ref[t]] = t
            tok_smem[p2_ref[t]] = t
            return carry

        lax.fori_loop(0, n_tok, inv, 0, unroll=DMA_UNROLL)
        issue(0)

    @pl.when(j + 1 < n_valid)
    def _():
        issue(j + 1)

    @pl.when(j < n_valid)
    def _():
        slot = j % 2
        _wait_row_gather(h_hbm, buf.at[slot], sem.at[slot], tm)
        o_ref[...] = buf[slot].astype(BF16)

    @pl.when(j >= n_valid)
    def _():
        o_ref[...] = jnp.zeros_like(o_ref)


def _gather_sorted(h, pos1, pos2, tile_cs, n_valid):
    t, d = h.shape
    nt = tile_cs.shape[0]
    return pl.pallas_call(
        _gather_kernel,
        grid_spec=pltpu.PrefetchScalarGridSpec(
            num_scalar_prefetch=4,
            grid=(nt,),
            in_specs=[pl.BlockSpec(memory_space=pl.ANY)],
            out_specs=pl.BlockSpec((MOE_TILE, d), lambda j, a, b, c, nv: (j, 0)),
            scratch_shapes=[
                pltpu.SMEM((2 * t,), jnp.int32),
                pltpu.VMEM((2, MOE_TILE, d), F32),
                pltpu.SemaphoreType.DMA((2,)),
            ],
        ),
        out_shape=jax.ShapeDtypeStruct((nt * MOE_TILE, d), BF16),
        compiler_params=_cparams(("arbitrary",)),
        name="moe_gather",
    )(pos1, pos2, tile_cs, n_valid, h)


def _expert_changed(te_ref, j):
    prev = te_ref[jnp.maximum(j - 1, 0)]
    return (j == 0) | (te_ref[j] != prev)


def _expert_up_kernel(te_ref, nv_ref, xs_ref, w1_ref, w3_ref, u_ref, w1b_scr, w3b_scr):
    j = pl.program_id(1)

    @pl.when(_expert_changed(te_ref, j))
    def _():
        w1b_scr[...] = w1_ref[0].astype(BF16)
        w3b_scr[...] = w3_ref[0].astype(BF16)

    @pl.when(j < nv_ref[0])
    def _():
        xs = xs_ref[...]
        tf = u_ref.shape[1]
        for c0 in range(0, tf, MOE_UP_CHUNK):
            cols = slice(c0, min(c0 + MOE_UP_CHUNK, tf))
            a = _dot(xs, w1b_scr[:, cols])
            u_ref[:, cols] = (_silu(a) * _dot(xs, w3b_scr[:, cols])).astype(BF16)

    @pl.when(j >= nv_ref[0])
    def _():
        u_ref[...] = jnp.zeros_like(u_ref)


def _expert_up(xs, tile_expert, n_valid, w1, w3, layer):
    s, d = xs.shape
    dff = w1.shape[3]
    tf = dff // 2 if (dff // 2) % MXU_WIDTH == 0 else dff
    nt = s // MOE_TILE

    def rows(f, j, te, nv):
        return (jnp.minimum(j, nv[0] - 1), 0)

    return pl.pallas_call(
        _expert_up_kernel,
        grid_spec=pltpu.PrefetchScalarGridSpec(
            num_scalar_prefetch=2,
            grid=(dff // tf, nt),
            in_specs=[
                pl.BlockSpec((MOE_TILE, d), rows),
                pl.BlockSpec((None, 1, d, tf), lambda f, j, te, nv: (layer, te[j], 0, f)),
                pl.BlockSpec((None, 1, d, tf), lambda f, j, te, nv: (layer, te[j], 0, f)),
            ],
            out_specs=pl.BlockSpec((MOE_TILE, tf), lambda f, j, te, nv: (j, f)),
            scratch_shapes=[pltpu.VMEM((d, tf), BF16), pltpu.VMEM((d, tf), BF16)],
        ),
        out_shape=jax.ShapeDtypeStruct((s, dff), BF16),
        compiler_params=_cparams(("arbitrary", "arbitrary")),
        name="moe_expert_up",
    )(tile_expert, n_valid, xs, w1, w3)


def _expert_down_kernel(te_ref, nv_ref, u_ref, w2_ref, y_ref, w2b_scr):
    j = pl.program_id(1)

    @pl.when(_expert_changed(te_ref, j))
    def _():
        w2b_scr[...] = w2_ref[0].astype(BF16)

    @pl.when(j < nv_ref[0])
    def _():
        y_ref[...] = _dot(u_ref[...], w2b_scr[...])

    @pl.when(j >= nv_ref[0])
    def _():
        y_ref[...] = jnp.zeros_like(y_ref)


def _expert_down(u, tile_expert, n_valid, w2, layer):
    s, dff = u.shape
    d = w2.shape[3]
    tn = d
    nt = s // MOE_TILE
    return pl.pallas_call(
        _expert_down_kernel,
        grid_spec=pltpu.PrefetchScalarGridSpec(
            num_scalar_prefetch=2,
            grid=(d // tn, nt),
            in_specs=[
                pl.BlockSpec((MOE_TILE, dff), lambda n, j, te, nv: (jnp.minimum(j, nv[0] - 1), 0)),
                pl.BlockSpec((None, 1, dff, tn), lambda n, j, te, nv: (layer, te[j], 0, n)),
            ],
            out_specs=pl.BlockSpec((MOE_TILE, tn), lambda n, j, te, nv: (j, n)),
            scratch_shapes=[pltpu.VMEM((dff, tn), BF16)],
        ),
        out_shape=jax.ShapeDtypeStruct((s, d), F32),
        compiler_params=_cparams(("arbitrary", "arbitrary")),
        name="moe_expert_down",
    )(tile_expert, n_valid, u, w2)


def _combine_kernel(d1_ref, d2_ref, x_ref, mod_ref, route_ref, ys_hbm, o_ref, buf1, buf2, sem, *, d):
    tc = x_ref.shape[0]
    j = pl.program_id(0)

    def issue(tile):
        slot = tile % 2
        base = tile * tc
        _start_row_gather(ys_hbm, lambda i: d1_ref[base + i], buf1.at[slot], sem.at[0, slot], tc)
        _start_row_gather(ys_hbm, lambda i: d2_ref[base + i], buf2.at[slot], sem.at[1, slot], tc)

    @pl.when(j == 0)
    def _():
        issue(0)

    @pl.when(j + 1 < pl.num_programs(0))
    def _():
        issue(j + 1)

    slot = j % 2
    _wait_row_gather(ys_hbm, buf1.at[slot], sem.at[0, slot], tc)
    _wait_row_gather(ys_hbm, buf2.at[slot], sem.at[1, slot], tc)
    route = route_ref[...]
    lane = lax.broadcasted_iota(jnp.int32, route.shape, 1)
    w1 = jnp.sum(jnp.where(lane == ROUTE_W1, route, 0.0), axis=-1, keepdims=True)
    w2 = jnp.sum(jnp.where(lane == ROUTE_W2, route, 0.0), axis=-1, keepdims=True)
    o_ref[...] = x_ref[...] + mod_ref[0, :, 5 * d:6 * d] * (w1 * buf1[slot] + w2 * buf2[slot])


def _combine(x, mod3, route, ys, dest1, dest2, seg_len):
    t, d = x.shape
    tc = 256
    tps = seg_len // tc
    return pl.pallas_call(
        functools.partial(_combine_kernel, d=d),
        grid_spec=pltpu.PrefetchScalarGridSpec(
            num_scalar_prefetch=2,
            grid=(t // tc,),
            in_specs=[
                pl.BlockSpec((tc, d), lambda i, a, b: (i, 0)),
                pl.BlockSpec((1, 1, 6 * d), lambda i, a, b: (i // tps, 0, 0)),
                pl.BlockSpec((tc, LANES), lambda i, a, b: (i, 0)),
                pl.BlockSpec(memory_space=pl.ANY),
            ],
            out_specs=pl.BlockSpec((tc, d), lambda i, a, b: (i, 0)),
            scratch_shapes=[pltpu.VMEM((2, tc, d), F32), pltpu.VMEM((2, tc, d), F32), pltpu.SemaphoreType.DMA((2, 2))],
        ),
        out_shape=jax.ShapeDtypeStruct((t, d), F32),
        compiler_params=_cparams(("arbitrary",)),
        name="moe_combine",
    )(dest1, dest2, x, mod3, route, ys)


def _moe_ffn(x, mod3, g, rw_pad, w1, w3, w2, layer, seg_len):
    t, d = x.shape
    h, route, counts = _route(x, mod3, g, rw_pad, seg_len)
    cnt = counts[0, :N_EXPERTS].astype(jnp.int32)
    padded = (cnt + MOE_TILE - 1) // MOE_TILE * MOE_TILE
    ends = jnp.cumsum(padded)
    offs = ends - padded
    starts = jnp.cumsum(cnt) - cnt
    e1 = route[:, ROUTE_E1].astype(jnp.int32)
    e2 = route[:, ROUTE_E2].astype(jnp.int32)
    r1 = route[:, ROUTE_R1].astype(jnp.int32)
    r2 = route[:, ROUTE_R2].astype(jnp.int32)
    pos1, pos2 = starts[e1] + r1, starts[e2] + r2
    dest1, dest2 = offs[e1] + r1, offs[e2] + r2
    n_tiles = (2 * t) // MOE_TILE + N_EXPERTS
    tile_start = jnp.arange(n_tiles, dtype=jnp.int32) * MOE_TILE
    tile_expert = jnp.minimum(jnp.sum((tile_start[:, None] >= ends[None, :]).astype(jnp.int32), axis=1), N_EXPERTS - 1)
    tile_cs = starts[tile_expert] + (tile_start - offs[tile_expert])
    n_valid = (ends[-1:] // MOE_TILE).astype(jnp.int32)
    xs = _gather_sorted(h, pos1, pos2, tile_cs, n_valid)
    u = _expert_up(xs, tile_expert, n_valid, w1, w3, layer)
    ys = _expert_down(u, tile_expert, n_valid, w2, layer)
    return _combine(x, mod3, route, ys, dest1, dest2, seg_len)


def _final_norm_kernel(x_ref, g_ref, o_ref):
    x = x_ref[...]
    ms = jnp.mean(x * x, axis=-1, keepdims=True)
    o_ref[...] = x * lax.rsqrt(ms + EPS) * g_ref[...]


def _final_norm(x, g):
    t, d = x.shape
    tm = 512
    return pl.pallas_call(
        _final_norm_kernel,
        grid=(t // tm,),
        in_specs=[pl.BlockSpec((tm, d), lambda i: (i, 0)), pl.BlockSpec((1, d), lambda i: (0, 0))],
        out_specs=pl.BlockSpec((tm, d), lambda i: (i, 0)),
        out_shape=jax.ShapeDtypeStruct((t, d), F32),
        compiler_params=_cparams(("parallel",)),
        name="final_norm",
    )(x, g.reshape(1, d))


def _rope_tables(n_tokens):
    pos = jnp.arange(n_tokens, dtype=jnp.int32)
    rowcol = jnp.stack([pos // GRID_W, pos % GRID_W], axis=-1).astype(F32)
    inv_freq = 1.0 / (ROPE_THETA ** (jnp.arange(ROPE_FREQS, dtype=F32) / ROPE_FREQS))
    ang = rowcol[:, :, None] * inv_freq
    cos = jnp.cos(ang)
    sin = jnp.sin(ang)
    cos_h = jnp.stack([cos, cos], axis=2).reshape(n_tokens, HEAD_DIM)
    sin_h = jnp.stack([-sin, sin], axis=2).reshape(n_tokens, HEAD_DIM)
    cos_t = jnp.tile(cos_h, (1, LANES // HEAD_DIM))
    sin_t = jnp.tile(sin_h, (1, LANES // HEAD_DIM))
    return (jnp.stack([jnp.ones_like(cos_t), cos_t]), jnp.stack([jnp.zeros_like(sin_t), sin_t]))


def _dft_matrices(n):
    n1 = 1 << (int(math.log2(n)) // 2)
    assert n % n1 == 0
    n2 = n // n1
    m = jnp.arange(n, dtype=jnp.int32)[None, :]
    ang_a = ((jnp.arange(n1, dtype=jnp.int32)[:, None] * m) % n1).astype(F32) * (2.0 * math.pi / n1)
    ang_b = ((jnp.arange(n2, dtype=jnp.int32)[:, None] * m) % n).astype(F32) * (2.0 * math.pi / n)
    ca, sa = jnp.cos(ang_a)[:, None, :], jnp.sin(ang_a)[:, None, :]
    cb, sb = jnp.cos(ang_b)[None, :, :], jnp.sin(ang_b)[None, :, :]
    scale = n ** -0.5
    cos_m = ((ca * cb - sa * sb) * scale).reshape(n, n)
    nsin_m = ((sa * cb + ca * sb) * -scale).reshape(n, n)
    return cos_m.astype(BF16), nsin_m.astype(BF16)


def _channel_dft_weight():
    n = FOURIER_GROUP_DIM
    idx = jnp.arange(n, dtype=jnp.int32)
    ang = ((idx[:, None] * idx[None, :]) % n).astype(F32) * (2.0 * math.pi / n)
    eye = jnp.eye(FOURIER_GROUPS, dtype=F32)
    cos_bd = jnp.kron(eye, jnp.cos(ang) * n ** -0.5)
    sin_bd = jnp.kron(eye, jnp.sin(ang) * n ** -0.5)
    return jnp.concatenate([cos_bd, sin_bd], axis=1).astype(BF16)


def _repack_w_in(w_in, lay):
    depth, d, _ = w_in.shape
    widths = (ATT_WIDTH, KV_WIDTH, KV_WIDTH, GLA_K_WIDTH, GLA_K_WIDTH, GLA_V_WIDTH, GLA_V_WIDTH,
              GLA_LOW_RANK, GLA_LOW_RANK, FOURIER_WIDTH, N_BRANCHES * d)
    offs = [0]
    for w in widths:
        offs.append(offs[-1] + w)
    aq, ak, av, gq, gk, gv, gr, lrf, lrb, fu, gates = [w_in[:, :, offs[i]:offs[i + 1]] for i in range(len(widths))]
    used = lay["lr"] + 2 * GLA_LOW_RANK
    pad = jnp.zeros((depth, d, lay["total"] - used), w_in.dtype)
    return jnp.concatenate([aq, gv, gr, fu, gates, gq, gk, ak, av, lrf, lrb, pad], axis=-1).astype(BF16)


def kernel(x_prompt, x_sample, cache_k, cache_v, state_gla, c, c_ctx, ada_w, ada_b, norm1_g, norm2_g, w_in, q_norm_g, k_norm_g, alpha_up, alpha_b, gla_norm_g, w_branch, w_out, ffn_w1, ffn_w3, ffn_w2, router_w, moe_w1, moe_w3, moe_w2, final_g):
    batch, seq, d = x_prompt.shape
    dec_batch, dec_seq, _ = x_sample.shape
    depth = w_in.shape[0]
    past = cache_k.shape[2]
    seg_len = batch * seq
    assert seg_len == dec_seq, "context tokens and each latent request must fill equal segments"
    assert seq % GLA_CHUNK == 0 and dec_seq % GLA_CHUNK == 0
    assert COL_GATES % d == 0
    n_seg = 1 + dec_batch
    t = n_seg * seg_len
    lay = _in_layout(d)

    w_in_p = _repack_w_in(w_in, lay)
    w_branch_b = w_branch.astype(BF16)
    w_out_b = w_out.astype(BF16)
    ffn_w1_b, ffn_w3_b, ffn_w2_b = ffn_w1.astype(BF16), ffn_w3.astype(BF16), ffn_w2.astype(BF16)
    router_pad = jnp.pad(router_w, ((0, 0), (0, 0), (0, LANES - N_EXPERTS)))
    cos_tab, sin_tab = _rope_tables(dec_seq)
    dft_cos_s, dft_nsin_s = _dft_matrices(dec_seq)
    dft_cos_p, dft_nsin_p = _dft_matrices(seq)
    w_chan = _channel_dft_weight()
    ones_bd = jnp.kron(jnp.eye(ATT_HEADS, dtype=F32), jnp.ones((HEAD_DIM, HEAD_DIM), F32)).astype(BF16)
    q_g = jnp.tile(q_norm_g, (1, ATT_HEADS)).reshape(depth, 1, ATT_WIDTH)
    k_g = jnp.tile(k_norm_g, (1, ATT_KV_HEADS)).reshape(depth, 1, KV_WIDTH)
    aup_pad = jnp.zeros((depth, 2, LANES, GLA_K_WIDTH), F32)
    aup_pad = aup_pad.at[:, 0, :GLA_LOW_RANK].set(alpha_up[:, 0])
    aup_pad = aup_pad.at[:, 1, GLA_LOW_RANK:2 * GLA_LOW_RANK].set(alpha_up[:, 1])
    ab = alpha_b.reshape(depth, 2, 1, GLA_K_WIDTH)
    s0_lat = state_gla.transpose(0, 1, 2, 5, 3, 4).reshape(dec_batch, depth, 2, GLA_DV, GLA_K_WIDTH)
    s0_all = jnp.concatenate([jnp.zeros((1,) + s0_lat.shape[1:], F32), s0_lat], axis=0)

    def ctx_keys(x):
        xb = x.astype(BF16)[:, :, :, None, :]
        return jnp.broadcast_to(xb, (dec_batch, past, ATT_KV_HEADS, 2, HEAD_DIM)).reshape(dec_batch, past, 2 * KV_WIDTH)

    def ctx_values_t(x):
        xt = x.astype(BF16).transpose(0, 2, 3, 1)
        return jnp.concatenate([xt, jnp.ones_like(xt)], axis=2).reshape(dec_batch, ATT_KV_HEADS * VT_ROWS, past)

    cond = jnp.zeros((8, d), F32).at[0].set(c_ctx).at[1:1 + dec_batch].set(c)
    mod_all = _modulation(cond, ada_w, ada_b)

    x = jnp.concatenate([x_prompt.reshape(seg_len, d), x_sample.reshape(dec_batch * dec_seq, d)], axis=0)
    new_k, new_v, new_s = [], [], []
    for l in range(depth):
        mod3 = mod_all[l, :n_seg].reshape(n_seg, 1, 6 * d)
        z = _in_projection(x, mod3, norm1_g[l], w_in_p[l], seg_len)
        qn, k_out, k2, vt = _attention_prep(z, lay, cos_tab, sin_tab, q_g[l], k_g[l], ones_bd, seg_len)
        att_p = _attention(qn, k2, vt, None, None, row0=0, n_seq=batch, seq_len=seq, tq=min(ATT_QUERY_TILE, seq))
        att_s = _attention(qn, k2, vt, ctx_keys(cache_k[:, l]), ctx_values_t(cache_v[:, l]),
                           row0=seg_len, n_seq=dec_batch, seq_len=dec_seq, tq=min(ATT_QUERY_TILE, dec_seq))
        gla_o, s_fin = _gla(z, lay, aup_pad[l], ab[l], s0_all[:, l], seg_len=seg_len, n_seg=n_seg, prompt_len=seq)
        xcs = _fourier_channels(z, w_chan)
        four_p = _fourier_positions(xcs, dft_cos_p, dft_nsin_p, row0=0, n_seq=batch, seq_len=seq)
        four_s = _fourier_positions(xcs, dft_cos_s, dft_nsin_s, row0=seg_len, n_seq=dec_batch, seq_len=dec_seq)
        x = _merge(att_p, att_s, gla_o, z, four_p, four_s, x, mod3, gla_norm_g[l], w_branch_b[l], w_out_b[l], seg_len)
        j = l // 2
        if l % 2 == 0:
            x = _dense_ffn(x, mod3, norm2_g[l], ffn_w1_b[j], ffn_w3_b[j], ffn_w2_b[j], seg_len)
        else:
            x = _moe_ffn(x, mod3, norm2_g[l], router_pad[j], moe_w1, moe_w3, moe_w2, j, seg_len)
        new_k.append(k_out[:seg_len].reshape(batch, seq, ATT_KV_HEADS, HEAD_DIM))
        v_col = lay["akv"] + KV_WIDTH
        new_v.append(z[:seg_len, v_col:v_col + KV_WIDTH].astype(F32).reshape(batch, seq, ATT_KV_HEADS, HEAD_DIM))
        cps = seq // GLA_CHUNK
        sf = s_fin[0, 0, cps - 1::cps]
        sb = s_fin[0, 1, 0::cps]
        st = jnp.stack([sf, sb], axis=1).reshape(batch, 2, GLA_DV, GLA_HEADS, GLA_DK)
        new_s.append(st.transpose(0, 1, 3, 4, 2))
    y = _final_norm(x, final_g)
    y_prompt = y[:seg_len].reshape(batch, seq, d)
    y_sample = y[seg_len:].reshape(dec_batch, dec_seq, d)
    return (y_prompt, y_sample, jnp.stack(new_k, axis=1), jnp.stack(new_v, axis=1), jnp.stack(new_s, axis=1))
```

```python
import functools
import math

import jax
import jax.numpy as jnp
from jax import lax
from jax.experimental import pallas as pl
from jax.experimental.pallas import tpu as pltpu

F32 = jnp.float32
BF16 = jnp.bfloat16

EPS = 1e-6
ATT_HEADS = 8
ATT_KV_HEADS = 2
HEAD_DIM = 64
ATT_WIDTH = ATT_HEADS * HEAD_DIM
KV_WIDTH = ATT_KV_HEADS * HEAD_DIM
ROPE_FREQS = HEAD_DIM // 4
ROPE_THETA = 10000.0
GRID_W = 64
GLA_HEADS = 4
GLA_DK = 64
GLA_DV = 128
GLA_K_WIDTH = GLA_HEADS * GLA_DK
GLA_V_WIDTH = GLA_HEADS * GLA_DV
GLA_LOW_RANK = 16
GLA_TAU = 16.0
FOURIER_GROUPS = 4
FOURIER_GROUP_DIM = 128
FOURIER_WIDTH = FOURIER_GROUPS * FOURIER_GROUP_DIM
N_BRANCHES = 3
N_EXPERTS = 8

LANES = 128
MXU_WIDTH = 256
VMEM_LIMIT_BYTES = 56 * 1024 * 1024

COL_AQ = 0
COL_GV = 512
COL_GR = 1024
COL_FU = 1536
COL_GATES = 2048

GLA_CHUNK = 256
GLA_SAFE_LOG = 60.0


def _cparams(sem):
    return pltpu.CompilerParams(dimension_semantics=sem, vmem_limit_bytes=VMEM_LIMIT_BYTES)


def _resident(shape, index_map):
    return pl.BlockSpec(shape, index_map, pipeline_mode=pl.Buffered(1))


def _split2(x):
    hi = x.astype(BF16)
    lo = (x - hi.astype(F32)).astype(BF16)
    return hi, lo


def _split3(x):
    hi = x.astype(BF16)
    r = x - hi.astype(F32)
    mid = r.astype(BF16)
    lo = (r - mid.astype(F32)).astype(BF16)
    return hi, mid, lo


def _dot(a, b):
    return jnp.dot(a, b, preferred_element_type=F32)


def _dot_nt(a, b):
    return lax.dot_general(a, b, (((1,), (1,)), ((), ())), preferred_element_type=F32)


def _dot_tn(a, b):
    return lax.dot_general(a, b, (((0,), (0,)), ((), ())), preferred_element_type=F32)


def _dot3(a, b):
    a_hi, a_lo = _split2(a)
    b_hi, b_lo = _split2(b)
    return _dot(a_hi, b_hi) + _dot(a_lo, b_hi) + _dot(a_hi, b_lo)


def _sigmoid(x):
    return 1.0 / (1.0 + jnp.exp(-x))


def _silu(x):
    return x * _sigmoid(x)


def _in_layout(d_model):
    gates_end = COL_GATES + N_BRANCHES * d_model
    col_gq = gates_end
    col_gk = col_gq + GLA_K_WIDTH
    col_akv = col_gk + GLA_K_WIDTH
    col_lr = col_akv + 2 * KV_WIDTH
    used = col_lr + LANES
    total = -(-used // 1024) * 1024
    return dict(gq=col_gq, gk=col_gk, akv=col_akv, lr=col_lr, total=total)


def _mod_kernel(cond_ref, w_ref, b_ref, o_ref):
    c = _silu(cond_ref[...])
    o_ref[0] = _dot3(c, w_ref[0]) + b_ref[0]


def _modulation(cond, ada_w, ada_b):
    depth, d, n6 = ada_w.shape
    tn = 1536 if n6 % 1536 == 0 else n6
    rows = cond.shape[0]
    return pl.pallas_call(
        _mod_kernel,
        grid=(depth, n6 // tn),
        in_specs=[
            pl.BlockSpec((rows, d), lambda l, j: (0, 0)),
            pl.BlockSpec((1, d, tn), lambda l, j: (l, 0, j)),
            pl.BlockSpec((1, 1, tn), lambda l, j: (l, 0, j)),
        ],
        out_specs=pl.BlockSpec((1, rows, tn), lambda l, j: (l, 0, j)),
        out_shape=jax.ShapeDtypeStruct((depth, rows, n6), F32),
        compiler_params=_cparams(("parallel", "parallel")),
        name="adaln_mod",
    )(cond, ada_w, ada_b.reshape(depth, 1, n6))


def _norm_mod(x, g, shift, scale):
    ms = jnp.mean(x * x, axis=-1, keepdims=True)
    return x * lax.rsqrt(ms + EPS) * g * (1.0 + scale) + shift


def _inproj_kernel(x_ref, mod_ref, g_ref, w_ref, z_ref, h_scr, *, d):
    @pl.when(pl.program_id(1) == 0)
    def _():
        h = _norm_mod(x_ref[...], g_ref[...], mod_ref[0, :, 0:d], mod_ref[0, :, d:2 * d])
        h_scr[...] = h.astype(BF16)

    z_ref[...] = _dot(h_scr[...], w_ref[...]).astype(z_ref.dtype)


def _in_projection(x, mod3, g, w, layer, seg_len):
    t, d = x.shape
    n = w.shape[2]
    tm = min(1024, seg_len)
    tn = 1024
    tps = seg_len // tm
    return pl.pallas_call(
        functools.partial(_inproj_kernel, d=d),
        grid=(t // tm, n // tn),
        in_specs=[
            pl.BlockSpec((tm, d), lambda i, j: (i, 0)),
            pl.BlockSpec((1, 1, 6 * d), lambda i, j: (i // tps, 0, 0)),
            pl.BlockSpec((1, d), lambda i, j: (0, 0)),
            pl.BlockSpec((None, d, tn), lambda i, j: (layer, 0, j)),
        ],
        out_specs=pl.BlockSpec((tm, tn), lambda i, j: (i, j)),
        out_shape=jax.ShapeDtypeStruct((t, n), BF16),
        scratch_shapes=[pltpu.VMEM((tm, d), BF16)],
        compiler_params=_cparams(("parallel", "arbitrary")),
        name="in_projection",
    )(x, mod3, g.reshape(1, d), w)


def _head_rms(x, g, ones_bd):
    hi, lo = _split2(x * x)
    ss = _dot(hi, ones_bd) + _dot(lo, ones_bd)
    return x * lax.rsqrt(ss * (1.0 / HEAD_DIM) + EPS) * g


def _rope(x, cos, sin_signed):
    width = x.shape[1]
    reps = width // LANES
    if reps > 1:
        cos = jnp.concatenate([cos] * reps, axis=1)
        sin_signed = jnp.concatenate([sin_signed] * reps, axis=1)
    lane = lax.broadcasted_iota(jnp.int32, x.shape, 1)
    first_half = (lane % (2 * ROPE_FREQS)) < ROPE_FREQS
    partner = jnp.where(first_half, pltpu.roll(x, width - ROPE_FREQS, 1), pltpu.roll(x, ROPE_FREQS, 1))
    return x * cos + partner * sin_signed


ATT_KEY_CHUNK = 512
ATT_QUERY_TILE = 256
VT_ROWS = 2 * HEAD_DIM
VT_USED = HEAD_DIM + 16


def _dup_heads(x):
    lane = lax.broadcasted_iota(jnp.int32, x.shape, 1)
    swapped = pltpu.roll(x, HEAD_DIM, 1)
    low = lane < HEAD_DIM
    return jnp.concatenate([jnp.where(low, x, swapped), jnp.where(low, swapped, x)], axis=1)


def _attn_prep_kernel(zq_ref, zkv_ref, cos_ref, sin_ref, qg_ref, kg_ref, bd_ref, q_ref, kout_ref, k2_ref, vt_ref):
    cos = cos_ref[0]
    sin = sin_ref[0]
    bd = bd_ref[...]
    q = _head_rms(zq_ref[...].astype(F32), qg_ref[...], bd)
    q_ref[...] = (_rope(q, cos, sin) * (HEAD_DIM ** -0.5)).astype(BF16)
    kv = zkv_ref[...].astype(F32)
    k = _head_rms(kv[:, :KV_WIDTH], kg_ref[...], bd[:KV_WIDTH, :KV_WIDTH])
    kout_ref[...] = k
    k2_ref[...] = _dup_heads(_rope(k, cos, sin)).astype(BF16)
    vt = kv[:, KV_WIDTH:].T
    ones = jnp.ones((HEAD_DIM, vt.shape[1]), F32)
    vt_ref[...] = jnp.concatenate([vt[:HEAD_DIM], ones, vt[HEAD_DIM:], ones], axis=0).astype(BF16)


def _attention_prep(z, lay, cos_tab, sin_tab, q_g, k_g, ones_bd, seg_len):
    t = z.shape[0]
    tm = min(512, seg_len)
    tps = seg_len // tm
    akv_blk = lay["akv"] // (2 * KV_WIDTH)
    tab_spec = pl.BlockSpec((1, tm, LANES), lambda i: (jnp.minimum(i // tps, 1), i % tps, 0))
    return pl.pallas_call(
        _attn_prep_kernel,
        grid=(t // tm,),
        in_specs=[
            pl.BlockSpec((tm, ATT_WIDTH), lambda i: (i, COL_AQ // ATT_WIDTH)),
            pl.BlockSpec((tm, 2 * KV_WIDTH), lambda i: (i, akv_blk)),
            tab_spec,
            tab_spec,
            pl.BlockSpec((1, ATT_WIDTH), lambda i: (0, 0)),
            pl.BlockSpec((1, KV_WIDTH), lambda i: (0, 0)),
            pl.BlockSpec((ATT_WIDTH, ATT_WIDTH), lambda i: (0, 0)),
        ],
        out_specs=[
            pl.BlockSpec((tm, ATT_WIDTH), lambda i: (i, 0)),
            pl.BlockSpec((tm, KV_WIDTH), lambda i: (i, 0)),
            pl.BlockSpec((tm, 2 * KV_WIDTH), lambda i: (i, 0)),
            pl.BlockSpec((ATT_KV_HEADS * VT_ROWS, tm), lambda i: (0, i)),
        ],
        out_shape=[
            jax.ShapeDtypeStruct((t, ATT_WIDTH), BF16),
            jax.ShapeDtypeStruct((t, KV_WIDTH), F32),
            jax.ShapeDtypeStruct((t, 2 * KV_WIDTH), BF16),
            jax.ShapeDtypeStruct((ATT_KV_HEADS * VT_ROWS, t), BF16),
        ],
        compiler_params=_cparams(("parallel",)),
        name="attention_prep",
    )(z, z, cos_tab, sin_tab, q_g, k_g, ones_bd)


def _attn_kernel(*refs, has_ctx):
    if has_ctx:
        q_ref, k_ref, v_ref, ck_ref, cv_ref, o_ref = refs
    else:
        q_ref, k_ref, v_ref, o_ref = refs
    tq = q_ref.shape[0]
    sources = [(k_ref, v_ref)] + ([(ck_ref.at[0], cv_ref.at[0])] if has_ctx else [])
    chunks = []
    for kr, vr in sources:
        n = min(ATT_KEY_CHUNK, kr.shape[0])
        chunks += [(kr, vr, c * n, n) for c in range(kr.shape[0] // n)]
    low = lax.broadcasted_iota(jnp.int32, (tq, LANES), 1) < HEAD_DIM
    zero = jnp.zeros((tq, LANES), BF16)
    n_heads = ATT_HEADS // ATT_KV_HEADS
    ws = []
    for h in range(n_heads):
        q2 = q_ref[:, (h // 2) * LANES:(h // 2 + 1) * LANES]
        ws.append(jnp.where(low, q2, zero) if h % 2 == 0 else jnp.where(low, zero, q2))
    w = jnp.concatenate(ws, axis=0)
    maxes, parts = [], []
    scores = lambda c: _dot_nt(c[0][c[2]:c[2] + c[3], :], w)
    st_next = scores(chunks[0])
    for idx, (kr, vr, r0, n) in enumerate(chunks):
        st = st_next
        if idx + 1 < len(chunks):
            st_next = scores(chunks[idx + 1])
        mc = jnp.max(st, axis=0, keepdims=True)
        p = jnp.exp(st - mc).astype(BF16)
        maxes.append(mc)
        parts.append(_dot(vr[0:VT_USED, r0:r0 + n], p))
    m = functools.reduce(jnp.maximum, maxes)
    acc = sum(jnp.exp(mc - m) * part for mc, part in zip(maxes, parts))
    o_t = acc[:HEAD_DIM] / acc[HEAD_DIM:HEAD_DIM + 1]
    o_heads = jnp.concatenate([o_t[:, h * tq:(h + 1) * tq] for h in range(n_heads)], axis=0)
    o_ref[...] = o_heads.T.astype(BF16)


def _attention(q, k2, v2, ctx_k2, ctx_v2, *, row0, n_seq, seq_len, tq):
    ntq = seq_len // tq
    gw = ATT_WIDTH // ATT_KV_HEADS
    qblk0 = row0 // tq
    kblk0 = row0 // seq_len
    has_ctx = ctx_k2 is not None
    in_specs = [
        pl.BlockSpec((tq, gw), lambda b, g, i: (qblk0 + b * ntq + i, g)),
        pl.BlockSpec((seq_len, LANES), lambda b, g, i: (kblk0 + b, g)),
        pl.BlockSpec((VT_ROWS, seq_len), lambda b, g, i: (g, kblk0 + b)),
    ]
    args = [q, k2, v2]
    if has_ctx:
        past = ctx_k2.shape[1]
        in_specs += [pl.BlockSpec((1, past, LANES), lambda b, g, i: (b, 0, g)),
                     pl.BlockSpec((1, VT_ROWS, past), lambda b, g, i: (b, g, 0))]
        args += [ctx_k2, ctx_v2]
    return pl.pallas_call(
        functools.partial(_attn_kernel, has_ctx=has_ctx),
        grid=(n_seq, ATT_KV_HEADS, ntq),
        in_specs=in_specs,
        out_specs=pl.BlockSpec((tq, gw), lambda b, g, i: (b * ntq + i, g)),
        out_shape=jax.ShapeDtypeStruct((n_seq * seq_len, ATT_WIDTH), BF16),
        compiler_params=_cparams(("parallel", "parallel", "parallel")),
        name="attention_ctx" if has_ctx else "attention_self",
    )(*args)


def _gla_kernel(q_ref, k_ref, v_ref, lr_ref, aup_ref, ab_ref, s0_ref, o_ref, sfin_ref,
                s_scr, la_scr, qf_scr, kf_scr, vt_scr, ot_scr, *, chunks_per_seq):
    seg = pl.program_id(0)
    direction = pl.program_id(1)
    n = pl.program_id(2)
    c = q_ref.shape[0]
    kw = GLA_K_WIDTH
    fwd = direction == 0

    @pl.when(n % chunks_per_seq == 0)
    def _():
        s_scr[...] = s0_ref[0, 0]

    aup_hi, aup_lo = _split2(aup_ref[0])
    lr = lr_ref[...]
    x = _dot(lr, aup_hi) + _dot(lr, aup_lo) + ab_ref[0]
    la = (jnp.minimum(x, 0.0) - jnp.log(1.0 + jnp.exp(-jnp.abs(x)))) * (1.0 / GLA_TAU)

    row = lax.broadcasted_iota(jnp.int32, (c, c), 0)
    col = lax.broadcasted_iota(jnp.int32, (c, c), 1)
    tri = jnp.where(fwd, row - col, col - row) >= 0
    tri_b = tri.astype(BF16)
    la_hi, la_mid, la_lo = _split3(la)
    cum = _dot(tri_b, la_hi) + _dot(tri_b, la_mid) + _dot(tri_b, la_lo)
    bend = jnp.sum(la, axis=0, keepdims=True)
    safe = jnp.min(cum) >= -GLA_SAFE_LOG
    head_of_lane = lax.broadcasted_iota(jnp.int32, (1, kw), 1) // GLA_DK

    @pl.when(safe)
    def _():
        q = q_ref[...].astype(F32) * (GLA_DK ** -0.5)
        k = k_ref[...].astype(F32)
        vb = v_ref[...]
        qt = (q * jnp.exp(cum)).astype(BF16)
        kt = (k * jnp.exp(-cum)).astype(BF16)
        ke = (k * jnp.exp(bend - cum)).astype(BF16)
        s_old = s_scr[...]
        s_b = s_old.astype(BF16)
        s_new = jnp.exp(bend) * s_old
        zero = jnp.zeros_like(qt)
        for h in range(GLA_HEADS):
            hm = head_of_lane == h
            qh = jnp.where(hm, qt, zero)
            a = jnp.where(tri, _dot_nt(qh, kt), 0.0).astype(BF16)
            vh = vb[:, h * GLA_DV:(h + 1) * GLA_DV]
            o_ref[0, :, h * GLA_DV:(h + 1) * GLA_DV] = _dot(a, vh) + _dot_nt(qh, s_b)
            s_new = s_new + _dot_tn(vh, jnp.where(hm, ke, zero))
        s_scr[...] = s_new
        sfin_ref[0, 0, 0] = s_new

    @pl.when(jnp.logical_not(safe))
    def _():
        la_scr[...] = la
        qf_scr[...] = q_ref[...].astype(F32) * (GLA_DK ** -0.5)
        kf_scr[...] = k_ref[...].astype(F32)
        vt_scr[...] = v_ref[...].astype(F32).T
        ot_scr[...] = jnp.zeros_like(ot_scr)
        lane_c = lax.broadcasted_iota(jnp.int32, (1, c), 1)

        def body(i, s):
            t = jnp.where(fwd, i, c - 1 - i)
            a_t = jnp.exp(la_scr[pl.ds(t, 1), :])
            k_t = kf_scr[pl.ds(t, 1), :]
            q_t = qf_scr[pl.ds(t, 1), :]
            onehot = lane_c == t
            vmat = jnp.zeros((GLA_DV, kw), F32)
            for h in range(GLA_HEADS):
                vcol = jnp.sum(jnp.where(onehot, vt_scr[h * GLA_DV:(h + 1) * GLA_DV, :], 0.0), axis=1, keepdims=True)
                vmat = jnp.where(head_of_lane == h, vcol, vmat)
            s = a_t * s + vmat * k_t
            prod = s * q_t
            for h in range(GLA_HEADS):
                ocol = jnp.sum(jnp.where(head_of_lane == h, prod, 0.0), axis=1, keepdims=True)
                rows = slice(h * GLA_DV, (h + 1) * GLA_DV)
                ot_scr[rows, :] = jnp.where(onehot, ocol, ot_scr[rows, :])
            return s

        s_new = lax.fori_loop(0, c, body, s_scr[...])
        o_ref[0] = ot_scr[...].T
        s_scr[...] = s_new
        sfin_ref[0, 0, 0] = s_new


def _gla(z, lay, aup_pad, ab, s0, *, seg_len, n_seg, prompt_len):
    t = z.shape[0]
    c = GLA_CHUNK
    nc = seg_len // c
    cps_prompt = prompt_len // c
    gq_blk = lay["gq"] // GLA_K_WIDTH
    gk_blk = lay["gk"] // GLA_K_WIDTH
    gv_blk = COL_GV // GLA_V_WIDTH
    lr_blk = lay["lr"] // LANES

    def chunk(seg, d, n):
        return seg * nc + jnp.where(d == 0, n, nc - 1 - n)

    def kernel(*refs):
        cps = jnp.where(pl.program_id(0) == 0, cps_prompt, nc)
        _gla_kernel(*refs, chunks_per_seq=cps)

    return pl.pallas_call(
        kernel,
        grid=(n_seg, 2, nc),
        in_specs=[
            pl.BlockSpec((c, GLA_K_WIDTH), lambda s, d, n: (chunk(s, d, n), gq_blk)),
            pl.BlockSpec((c, GLA_K_WIDTH), lambda s, d, n: (chunk(s, d, n), gk_blk)),
            pl.BlockSpec((c, GLA_V_WIDTH), lambda s, d, n: (chunk(s, d, n), gv_blk)),
            pl.BlockSpec((c, LANES), lambda s, d, n: (chunk(s, d, n), lr_blk)),
            pl.BlockSpec((1, LANES, GLA_K_WIDTH), lambda s, d, n: (d, 0, 0)),
            pl.BlockSpec((1, 1, GLA_K_WIDTH), lambda s, d, n: (d, 0, 0)),
            pl.BlockSpec((1, 1, GLA_DV, GLA_K_WIDTH), lambda s, d, n: (s, d, 0, 0)),
        ],
        out_specs=[
            pl.BlockSpec((1, c, GLA_V_WIDTH), lambda s, d, n: (d, chunk(s, d, n), 0)),
            pl.BlockSpec((1, 1, 1, GLA_DV, GLA_K_WIDTH), lambda s, d, n: (s, d, jnp.where(d == 0, n, nc - 1 - n), 0, 0)),
        ],
        out_shape=[
            jax.ShapeDtypeStruct((2, t, GLA_V_WIDTH), F32),
            jax.ShapeDtypeStruct((n_seg, 2, nc, GLA_DV, GLA_K_WIDTH), F32),
        ],
        scratch_shapes=[
            pltpu.VMEM((GLA_DV, GLA_K_WIDTH), F32),
            pltpu.VMEM((c, GLA_K_WIDTH), F32),
            pltpu.VMEM((c, GLA_K_WIDTH), F32),
            pltpu.VMEM((c, GLA_K_WIDTH), F32),
            pltpu.VMEM((GLA_V_WIDTH, c), F32),
            pltpu.VMEM((GLA_V_WIDTH, c), F32),
        ],
        compiler_params=_cparams(("parallel", "parallel", "arbitrary")),
        name="gla_scan",
    )(z, z, z, z, aup_pad, ab, s0)


def _four_chan_kernel(u_ref, w_ref, o_ref):
    o_ref[...] = _dot(u_ref[...], w_ref[...]).astype(BF16)


def _fourier_channels(z, w_chan):
    t = z.shape[0]
    tm = 512
    return pl.pallas_call(
        _four_chan_kernel,
        grid=(t // tm,),
        in_specs=[
            pl.BlockSpec((tm, FOURIER_WIDTH), lambda i: (i, COL_FU // FOURIER_WIDTH)),
            _resident((FOURIER_WIDTH, 2 * FOURIER_WIDTH), lambda i: (0, 0)),
        ],
        out_specs=pl.BlockSpec((tm, 2 * FOURIER_WIDTH), lambda i: (i, 0)),
        out_shape=jax.ShapeDtypeStruct((t, 2 * FOURIER_WIDTH), BF16),
        compiler_params=_cparams(("parallel",)),
        name="fourier_channels",
    )(z, w_chan)


def _four_pos_kernel(c_ref, s_ref, xc_ref, xs_ref, o_ref):
    o_ref[...] = (_dot(c_ref[...], xc_ref[...]) + _dot(s_ref[...], xs_ref[...])).astype(BF16)


def _fourier_positions(xcs, cos_m, nsin_m, *, row0, n_seq, seq_len):
    tm = min(512, seq_len)
    nt = seq_len // tm
    blk0 = row0 // seq_len
    return pl.pallas_call(
        _four_pos_kernel,
        grid=(n_seq, nt),
        in_specs=[
            pl.BlockSpec((tm, seq_len), lambda b, i: (i, 0)),
            pl.BlockSpec((tm, seq_len), lambda b, i: (i, 0)),
            pl.BlockSpec((seq_len, FOURIER_WIDTH), lambda b, i: (blk0 + b, 0)),
            pl.BlockSpec((seq_len, FOURIER_WIDTH), lambda b, i: (blk0 + b, 1)),
        ],
        out_specs=pl.BlockSpec((tm, FOURIER_WIDTH), lambda b, i: (b * nt + i, 0)),
        out_shape=jax.ShapeDtypeStruct((n_seq * seq_len, FOURIER_WIDTH), BF16),
        compiler_params=_cparams(("parallel", "parallel")),
        name="fourier_positions",
    )(cos_m, nsin_m, xcs, xcs)


def _merge_kernel(attp_ref, atts_ref, gof_ref, gob_ref, gr_ref, fourp_ref, fours_ref, g0_ref, g1_ref, g2_ref, x_ref,
                  mod_ref, gg_ref, wb_ref, wo_ref, o_ref, *, d, prompt_tiles):
    in_prompt = pl.program_id(0) < prompt_tiles
    att = jnp.where(in_prompt, attp_ref[...], atts_ref[...])
    four = jnp.where(in_prompt, fourp_ref[...], fours_ref[...])
    o = gof_ref[0] + gob_ref[0]
    gg = gg_ref[...]
    parts = []
    for h in range(GLA_HEADS):
        oh = o[:, h * GLA_DV:(h + 1) * GLA_DV]
        ms = jnp.mean(oh * oh, axis=-1, keepdims=True)
        parts.append(oh * lax.rsqrt(ms + EPS) * gg)
    gla = (jnp.concatenate(parts, axis=1) * _silu(gr_ref[...].astype(F32))).astype(BF16)
    merged = (_sigmoid(g0_ref[...].astype(F32)) * _dot(att, wb_ref[0])
              + _sigmoid(g1_ref[...].astype(F32)) * _dot(gla, wb_ref[1])
              + _sigmoid(g2_ref[...].astype(F32)) * _dot(four, wb_ref[2]))
    mix = _dot(merged.astype(BF16), wo_ref[...])
    o_ref[...] = x_ref[...] + mod_ref[0, :, 2 * d:3 * d] * mix


def _merge(att_p, att_s, gla_o, z, four_p, four_s, x, mod3, gla_g, w_branch, w_out, layer, seg_len):
    t, d = x.shape
    tm = min(512, seg_len)
    tps = seg_len // tm
    bw = att_p.shape[1]
    gate_blk = COL_GATES // d
    prompt_tiles = att_p.shape[0] // tm
    p_spec = pl.BlockSpec((tm, bw), lambda i: (jnp.minimum(i, prompt_tiles - 1), 0))
    s_spec = pl.BlockSpec((tm, bw), lambda i: (jnp.maximum(i - prompt_tiles, 0), 0))
    return pl.pallas_call(
        functools.partial(_merge_kernel, d=d, prompt_tiles=prompt_tiles),
        grid=(t // tm,),
        in_specs=[
            p_spec,
            s_spec,
            pl.BlockSpec((1, tm, GLA_V_WIDTH), lambda i: (0, i, 0)),
            pl.BlockSpec((1, tm, GLA_V_WIDTH), lambda i: (1, i, 0)),
            pl.BlockSpec((tm, GLA_V_WIDTH), lambda i: (i, COL_GR // GLA_V_WIDTH)),
            p_spec,
            s_spec,
            pl.BlockSpec((tm, d), lambda i: (i, gate_blk)),
            pl.BlockSpec((tm, d), lambda i: (i, gate_blk + 1)),
            pl.BlockSpec((tm, d), lambda i: (i, gate_blk + 2)),
            pl.BlockSpec((tm, d), lambda i: (i, 0)),
            pl.BlockSpec((1, 1, 6 * d), lambda i: (i // tps, 0, 0)),
            pl.BlockSpec((1, GLA_DV), lambda i: (0, 0)),
            _resident((None, N_BRANCHES, bw, d), lambda i: (layer, 0, 0, 0)),
            _resident((None, d, d), lambda i: (layer, 0, 0)),
        ],
        out_specs=pl.BlockSpec((tm, d), lambda i: (i, 0)),
        out_shape=jax.ShapeDtypeStruct((t, d), F32),
        compiler_params=_cparams(("parallel",)),
        name="branch_merge",
    )(att_p, att_s, gla_o, gla_o, z, four_p, four_s, z, z, z, x, mod3, gla_g.reshape(1, GLA_DV), w_branch, w_out)


def _ffn_kernel(x_ref, mod_ref, g_ref, w1_ref, w3_ref, w2_ref, o_ref, *, d):
    x = x_ref[...]
    h = _norm_mod(x, g_ref[...], mod_ref[0, :, 3 * d:4 * d], mod_ref[0, :, 4 * d:5 * d]).astype(BF16)
    a = _dot(h, w1_ref[...])
    u = (_silu(a) * _dot(h, w3_ref[...])).astype(BF16)
    o_ref[...] = x + mod_ref[0, :, 5 * d:6 * d] * _dot(u, w2_ref[...])


def _dense_ffn(x, mod3, g, w1, w3, w2, layer, seg_len):
    t, d = x.shape
    dff = w1.shape[2]
    tm = min(512, seg_len)
    tps = seg_len // tm
    return pl.pallas_call(
        functools.partial(_ffn_kernel, d=d),
        grid=(t // tm,),
        in_specs=[
            pl.BlockSpec((tm, d), lambda i: (i, 0)),
            pl.BlockSpec((1, 1, 6 * d), lambda i: (i // tps, 0, 0)),
            pl.BlockSpec((1, d), lambda i: (0, 0)),
            _resident((None, d, dff), lambda i: (layer, 0, 0)),
            _resident((None, d, dff), lambda i: (layer, 0, 0)),
            _resident((None, dff, d), lambda i: (layer, 0, 0)),
        ],
        out_specs=pl.BlockSpec((tm, d), lambda i: (i, 0)),
        out_shape=jax.ShapeDtypeStruct((t, d), F32),
        compiler_params=_cparams(("parallel",)),
        name="dense_ffn",
    )(x, mod3, g.reshape(1, d), w1, w3, w2)


MOE_TILE = 512
MOE_UP_CHUNK = 512
ROUTE_E1, ROUTE_E2, ROUTE_R1, ROUTE_R2, ROUTE_W1, ROUTE_W2 = range(6)


def _route_kernel(x_ref, mod_ref, g_ref, rw_ref, h_ref, route_ref, count_ref, carry_scr, *, d):
    @pl.when(pl.program_id(0) == 0)
    def _():
        carry_scr[...] = jnp.zeros_like(carry_scr)

    h = _norm_mod(x_ref[...], g_ref[...], mod_ref[0, :, 3 * d:4 * d], mod_ref[0, :, 4 * d:5 * d])
    h_ref[...] = _pack_rows(h)
    logits = _dot3(h, rw_ref[...])
    tm = logits.shape[0]
    lane = lax.broadcasted_iota(jnp.int32, logits.shape, 1).astype(F32)
    neg = jnp.float32(-jnp.inf)
    l1 = jnp.where(lane < N_EXPERTS, logits, neg)
    m1 = jnp.max(l1, axis=-1, keepdims=True)
    i1 = jnp.min(jnp.where(l1 == m1, lane, float(LANES)), axis=-1, keepdims=True)
    l2 = jnp.where(lane == i1, neg, l1)
    m2 = jnp.max(l2, axis=-1, keepdims=True)
    i2 = jnp.min(jnp.where(l2 == m2, lane, float(LANES)), axis=-1, keepdims=True)
    e2 = jnp.exp(m2 - m1)
    w1 = 1.0 / (1.0 + e2)
    w2 = e2 * w1
    onehot = jnp.where((lane == i1) | (lane == i2), 1.0, 0.0)
    row = lax.broadcasted_iota(jnp.int32, (tm, tm), 0)
    col = lax.broadcasted_iota(jnp.int32, (tm, tm), 1)
    before = jnp.where(row > col, 1.0, 0.0).astype(BF16)
    rank = _dot(before, onehot.astype(BF16)) + carry_scr[...]
    r1 = jnp.sum(jnp.where(lane == i1, rank, 0.0), axis=-1, keepdims=True)
    r2 = jnp.sum(jnp.where(lane == i2, rank, 0.0), axis=-1, keepdims=True)
    rec = jnp.zeros_like(logits)
    for idx, val in ((ROUTE_E1, i1), (ROUTE_E2, i2), (ROUTE_R1, r1), (ROUTE_R2, r2), (ROUTE_W1, w1), (ROUTE_W2, w2)):
        rec = jnp.where(lane == idx, val, rec)
    route_ref[...] = rec
    total = carry_scr[...] + jnp.sum(onehot, axis=0, keepdims=True)
    carry_scr[...] = total
    count_ref[...] = total


def _route(x, mod3, g, rw_pad, seg_len):
    t, d = x.shape
    tm = min(512, seg_len)
    tps = seg_len // tm
    return pl.pallas_call(
        functools.partial(_route_kernel, d=d),
        grid=(t // tm,),
        in_specs=[
            pl.BlockSpec((tm, d), lambda i: (i, 0)),
            pl.BlockSpec((1, 1, 6 * d), lambda i: (i // tps, 0, 0)),
            pl.BlockSpec((1, d), lambda i: (0, 0)),
            pl.BlockSpec((d, LANES), lambda i: (0, 0)),
        ],
        out_specs=[
            pl.BlockSpec((tm, d // 2), lambda i: (i, 0)),
            pl.BlockSpec((tm, LANES), lambda i: (i, 0)),
            pl.BlockSpec((1, LANES), lambda i: (0, 0)),
        ],
        out_shape=[
            jax.ShapeDtypeStruct((t, d // 2), jnp.uint32),
            jax.ShapeDtypeStruct((t, LANES), F32),
            jax.ShapeDtypeStruct((1, LANES), F32),
        ],
        scratch_shapes=[pltpu.VMEM((1, LANES), F32)],
        compiler_params=_cparams(("arbitrary",)),
        name="moe_route",
    )(x, mod3, g.reshape(1, d), rw_pad)


DMA_UNROLL = 8


def _pack_rows(x):
    n = x.shape[1] // 2
    xb = x.astype(BF16).astype(F32)
    lo = lax.bitcast_convert_type(xb[:, :n], jnp.uint32)
    hi = lax.bitcast_convert_type(xb[:, n:], jnp.uint32)
    return hi | (lo >> 16)


def _unpack_rows(u):
    lo = lax.bitcast_convert_type(u << 16, F32)
    hi = lax.bitcast_convert_type(u & jnp.uint32(0xFFFF0000), F32)
    return lo, hi


def _start_row_gather(src_hbm, row_of, dst, sem, n):
    def body(i, carry):
        pltpu.make_async_copy(src_hbm.at[pl.ds(row_of(i), 1)], dst.at[pl.ds(i, 1)], sem).start()
        return carry

    lax.fori_loop(0, n, body, 0, unroll=DMA_UNROLL)


def _wait_row_gather(src_hbm, dst, sem, n):
    def body(i, carry):
        pltpu.make_async_copy(src_hbm.at[pl.ds(0, 1)], dst.at[pl.ds(0, 1)], sem).wait()
        return carry

    lax.fori_loop(0, n, body, 0, unroll=DMA_UNROLL)


def _gather_kernel(p1_ref, p2_ref, cs_ref, nv_ref, h_hbm, o_ref, tok_smem, buf, sem):
    j = pl.program_id(0)
    tm = o_ref.shape[0]
    n_tok = p1_ref.shape[0]
    n_valid = nv_ref[0]

    def issue(tile):
        cs = cs_ref[tile]
        slot = tile % 2
        _start_row_gather(h_hbm, lambda i: tok_smem[jnp.minimum(cs + i, 2 * n_tok - 1)], buf.at[slot], sem.at[slot], tm)

    @pl.when(j == 0)
    def _():
        def inv(t, carry):
            tok_smem[p1_ref[t]] = t
            tok_smem[p2_ref[t]] = t
            return carry

        lax.fori_loop(0, n_tok, inv, 0, unroll=DMA_UNROLL)
        issue(0)

    @pl.when(j + 1 < n_valid)
    def _():
        issue(j + 1)

    @pl.when(j < n_valid)
    def _():
        slot = j % 2
        _wait_row_gather(h_hbm, buf.at[slot], sem.at[slot], tm)
        lo, hi = _unpack_rows(buf[slot])
        half = lo.shape[1]
        o_ref[:, :half] = lo.astype(BF16)
        o_ref[:, half:] = hi.astype(BF16)

    @pl.when(j >= n_valid)
    def _():
        o_ref[...] = jnp.zeros_like(o_ref)


def _gather_sorted(h, pos1, pos2, tile_cs, n_valid):
    t, half = h.shape
    d = 2 * half
    nt = tile_cs.shape[0]
    return pl.pallas_call(
        _gather_kernel,
        grid_spec=pltpu.PrefetchScalarGridSpec(
            num_scalar_prefetch=4,
            grid=(nt,),
            in_specs=[pl.BlockSpec(memory_space=pl.ANY)],
            out_specs=pl.BlockSpec((MOE_TILE, d), lambda j, a, b, c, nv: (j, 0)),
            scratch_shapes=[
                pltpu.SMEM((2 * t,), jnp.int32),
                pltpu.VMEM((2, MOE_TILE, half), jnp.uint32),
                pltpu.SemaphoreType.DMA((2,)),
            ],
        ),
        out_shape=jax.ShapeDtypeStruct((nt * MOE_TILE, d), BF16),
        compiler_params=_cparams(("arbitrary",)),
        name="moe_gather",
    )(pos1, pos2, tile_cs, n_valid, h)


def _expert_changed(te_ref, j):
    prev = te_ref[jnp.maximum(j - 1, 0)]
    return (j == 0) | (te_ref[j] != prev)


def _expert_up_kernel(te_ref, nv_ref, xs_ref, w1_ref, w3_ref, u_ref, w1b_scr, w3b_scr):
    j = pl.program_id(1)

    @pl.when(_expert_changed(te_ref, j))
    def _():
        w1b_scr[...] = w1_ref[0].astype(BF16)
        w3b_scr[...] = w3_ref[0].astype(BF16)

    @pl.when(j < nv_ref[0])
    def _():
        xs = xs_ref[...]
        tf = u_ref.shape[1]
        col_chunks = [slice(c0, min(c0 + MOE_UP_CHUNK, tf)) for c0 in range(0, tf, MOE_UP_CHUNK)]
        both = lambda cols: (_dot(xs, w1b_scr[:, cols]), _dot(xs, w3b_scr[:, cols]))
        nxt = both(col_chunks[0])
        for idx, cols in enumerate(col_chunks):
            a, b = nxt
            if idx + 1 < len(col_chunks):
                nxt = both(col_chunks[idx + 1])
            u_ref[:, cols] = (_silu(a) * b).astype(BF16)

    @pl.when(j >= nv_ref[0])
    def _():
        u_ref[...] = jnp.zeros_like(u_ref)


def _expert_up(xs, tile_expert, n_valid, w1, w3, layer):
    s, d = xs.shape
    dff = w1.shape[3]
    tf = dff // 2 if (dff // 2) % MXU_WIDTH == 0 else dff
    nt = s // MOE_TILE

    def rows(f, j, te, nv):
        return (jnp.minimum(j, nv[0] - 1), 0)

    return pl.pallas_call(
        _expert_up_kernel,
        grid_spec=pltpu.PrefetchScalarGridSpec(
            num_scalar_prefetch=2,
            grid=(dff // tf, nt),
            in_specs=[
                pl.BlockSpec((MOE_TILE, d), rows),
                pl.BlockSpec((None, 1, d, tf), lambda f, j, te, nv: (layer, te[j], 0, f)),
                pl.BlockSpec((None, 1, d, tf), lambda f, j, te, nv: (layer, te[j], 0, f)),
            ],
            out_specs=pl.BlockSpec((MOE_TILE, tf), lambda f, j, te, nv: (j, f)),
            scratch_shapes=[pltpu.VMEM((d, tf), BF16), pltpu.VMEM((d, tf), BF16)],
        ),
        out_shape=jax.ShapeDtypeStruct((s, dff), BF16),
        compiler_params=_cparams(("arbitrary", "arbitrary")),
        name="moe_expert_up",
    )(tile_expert, n_valid, xs, w1, w3)


def _expert_down_kernel(te_ref, nv_ref, u_ref, w2_ref, y_ref, w2b_scr):
    j = pl.program_id(1)

    @pl.when(_expert_changed(te_ref, j))
    def _():
        w2b_scr[...] = w2_ref[0].astype(BF16)

    @pl.when(j < nv_ref[0])
    def _():
        y_ref[...] = _pack_rows(_dot(u_ref[...], w2b_scr[...]))

    @pl.when(j >= nv_ref[0])
    def _():
        y_ref[...] = jnp.zeros_like(y_ref)


def _expert_down(u, tile_expert, n_valid, w2, layer):
    s, dff = u.shape
    d = w2.shape[3]
    nt = s // MOE_TILE
    return pl.pallas_call(
        _expert_down_kernel,
        grid_spec=pltpu.PrefetchScalarGridSpec(
            num_scalar_prefetch=2,
            grid=(1, nt),
            in_specs=[
                pl.BlockSpec((MOE_TILE, dff), lambda n, j, te, nv: (jnp.minimum(j, nv[0] - 1), 0)),
                pl.BlockSpec((None, 1, dff, d), lambda n, j, te, nv: (layer, te[j], 0, 0)),
            ],
            out_specs=pl.BlockSpec((MOE_TILE, d // 2), lambda n, j, te, nv: (j, 0)),
            scratch_shapes=[pltpu.VMEM((dff, d), BF16)],
        ),
        out_shape=jax.ShapeDtypeStruct((s, d // 2), jnp.uint32),
        compiler_params=_cparams(("arbitrary", "arbitrary")),
        name="moe_expert_down",
    )(tile_expert, n_valid, u, w2)


def _combine_kernel(d1_ref, d2_ref, x_ref, mod_ref, route_ref, fg_ref, ys_hbm, o_ref, buf1, buf2, sem, *,
                    d, tile0, final_norm):
    tc = x_ref.shape[0]
    half = d // 2
    j = pl.program_id(0)

    def issue(step):
        slot = step % 2
        base = (tile0 + step) * tc
        _start_row_gather(ys_hbm, lambda i: d1_ref[base + i], buf1.at[slot], sem.at[0, slot], tc)
        _start_row_gather(ys_hbm, lambda i: d2_ref[base + i], buf2.at[slot], sem.at[1, slot], tc)

    @pl.when(j == 0)
    def _():
        issue(0)

    @pl.when(j + 1 < pl.num_programs(0))
    def _():
        issue(j + 1)

    slot = j % 2
    _wait_row_gather(ys_hbm, buf1.at[slot], sem.at[0, slot], tc)
    _wait_row_gather(ys_hbm, buf2.at[slot], sem.at[1, slot], tc)
    route = route_ref[...]
    lane = lax.broadcasted_iota(jnp.int32, route.shape, 1)
    w1 = jnp.sum(jnp.where(lane == ROUTE_W1, route, 0.0), axis=-1, keepdims=True)
    w2 = jnp.sum(jnp.where(lane == ROUTE_W2, route, 0.0), axis=-1, keepdims=True)
    lo1, hi1 = _unpack_rows(buf1[slot])
    lo2, hi2 = _unpack_rows(buf2[slot])
    gate = mod_ref[0, :, 5 * d:6 * d]
    y_lo = x_ref[:, :half] + gate[:, :half] * (w1 * lo1 + w2 * lo2)
    y_hi = x_ref[:, half:] + gate[:, half:] * (w1 * hi1 + w2 * hi2)
    if final_norm:
        ms = (jnp.sum(y_lo * y_lo, axis=-1, keepdims=True) + jnp.sum(y_hi * y_hi, axis=-1, keepdims=True)) * (1.0 / d)
        inv = lax.rsqrt(ms + EPS)
        y_lo = y_lo * inv * fg_ref[:, :half]
        y_hi = y_hi * inv * fg_ref[:, half:]
    o_ref[:, :half] = y_lo
    o_ref[:, half:] = y_hi


def _combine(x, mod3, route, ys, dest1, dest2, seg_len, *, row0=0, n_rows=None, final_g=None):
    t, d = x.shape
    n_rows = t if n_rows is None else n_rows
    tc = 256
    tps = seg_len // tc
    tile0 = row0 // tc
    final_norm = final_g is not None
    fg = (final_g if final_norm else jnp.ones((d,), F32)).reshape(1, d)
    return pl.pallas_call(
        functools.partial(_combine_kernel, d=d, tile0=tile0, final_norm=final_norm),
        grid_spec=pltpu.PrefetchScalarGridSpec(
            num_scalar_prefetch=2,
            grid=(n_rows // tc,),
            in_specs=[
                pl.BlockSpec((tc, d), lambda i, a, b: (tile0 + i, 0)),
                pl.BlockSpec((1, 1, 6 * d), lambda i, a, b: ((tile0 + i) // tps, 0, 0)),
                pl.BlockSpec((tc, LANES), lambda i, a, b: (tile0 + i, 0)),
                pl.BlockSpec((1, d), lambda i, a, b: (0, 0)),
                pl.BlockSpec(memory_space=pl.ANY),
            ],
            out_specs=pl.BlockSpec((tc, d), lambda i, a, b: (i, 0)),
            scratch_shapes=[pltpu.VMEM((2, tc, d // 2), jnp.uint32), pltpu.VMEM((2, tc, d // 2), jnp.uint32),
                            pltpu.SemaphoreType.DMA((2, 2))],
        ),
        out_shape=jax.ShapeDtypeStruct((n_rows, d), F32),
        compiler_params=_cparams(("arbitrary",)),
        name="moe_combine",
    )(dest1, dest2, x, mod3, route, fg, ys)


def _moe_ffn(x, mod3, g, rw_pad, w1, w3, w2, layer, seg_len, final_g=None):
    t, d = x.shape
    h, route, counts = _route(x, mod3, g, rw_pad, seg_len)
    cnt = counts[0, :N_EXPERTS].astype(jnp.int32)
    padded = (cnt + MOE_TILE - 1) // MOE_TILE * MOE_TILE
    ends = jnp.cumsum(padded)
    offs = ends - padded
    starts = jnp.cumsum(cnt) - cnt
    e1 = route[:, ROUTE_E1].astype(jnp.int32)
    e2 = route[:, ROUTE_E2].astype(jnp.int32)
    r1 = route[:, ROUTE_R1].astype(jnp.int32)
    r2 = route[:, ROUTE_R2].astype(jnp.int32)
    pos1, pos2 = starts[e1] + r1, starts[e2] + r2
    dest1, dest2 = offs[e1] + r1, offs[e2] + r2
    n_tiles = (2 * t) // MOE_TILE + N_EXPERTS
    tile_start = jnp.arange(n_tiles, dtype=jnp.int32) * MOE_TILE
    tile_expert = jnp.minimum(jnp.sum((tile_start[:, None] >= ends[None, :]).astype(jnp.int32), axis=1), N_EXPERTS - 1)
    tile_cs = starts[tile_expert] + (tile_start - offs[tile_expert])
    n_valid = (ends[-1:] // MOE_TILE).astype(jnp.int32)
    xs = _gather_sorted(h, pos1, pos2, tile_cs, n_valid)
    u = _expert_up(xs, tile_expert, n_valid, w1, w3, layer)
    ys = _expert_down(u, tile_expert, n_valid, w2, layer)
    if final_g is None:
        return _combine(x, mod3, route, ys, dest1, dest2, seg_len)
    return (_combine(x, mod3, route, ys, dest1, dest2, seg_len, row0=0, n_rows=seg_len, final_g=final_g),
            _combine(x, mod3, route, ys, dest1, dest2, seg_len, row0=seg_len, n_rows=t - seg_len, final_g=final_g))


def _final_norm_kernel(x_ref, g_ref, o_ref):
    x = x_ref[...]
    ms = jnp.mean(x * x, axis=-1, keepdims=True)
    o_ref[...] = x * lax.rsqrt(ms + EPS) * g_ref[...]


def _final_norm(x, g, *, row0, n_rows):
    d = x.shape[1]
    tm = 512
    blk0 = row0 // tm
    return pl.pallas_call(
        _final_norm_kernel,
        grid=(n_rows // tm,),
        in_specs=[pl.BlockSpec((tm, d), lambda i: (blk0 + i, 0)), pl.BlockSpec((1, d), lambda i: (0, 0))],
        out_specs=pl.BlockSpec((tm, d), lambda i: (i, 0)),
        out_shape=jax.ShapeDtypeStruct((n_rows, d), F32),
        compiler_params=_cparams(("parallel",)),
        name="final_norm",
    )(x, g.reshape(1, d))


def _rope_tables(n_tokens):
    pos = jnp.arange(n_tokens, dtype=jnp.int32)
    rowcol = jnp.stack([pos // GRID_W, pos % GRID_W], axis=-1).astype(F32)
    inv_freq = 1.0 / (ROPE_THETA ** (jnp.arange(ROPE_FREQS, dtype=F32) / ROPE_FREQS))
    ang = rowcol[:, :, None] * inv_freq
    cos = jnp.cos(ang)
    sin = jnp.sin(ang)
    cos_h = jnp.stack([cos, cos], axis=2).reshape(n_tokens, HEAD_DIM)
    sin_h = jnp.stack([-sin, sin], axis=2).reshape(n_tokens, HEAD_DIM)
    cos_t = jnp.tile(cos_h, (1, LANES // HEAD_DIM))
    sin_t = jnp.tile(sin_h, (1, LANES // HEAD_DIM))
    return (jnp.stack([jnp.ones_like(cos_t), cos_t]), jnp.stack([jnp.zeros_like(sin_t), sin_t]))


def _dft_matrices(n):
    n1 = 1 << (int(math.log2(n)) // 2)
    assert n % n1 == 0
    n2 = n // n1
    m = jnp.arange(n, dtype=jnp.int32)[None, :]
    ang_a = ((jnp.arange(n1, dtype=jnp.int32)[:, None] * m) % n1).astype(F32) * (2.0 * math.pi / n1)
    ang_b = ((jnp.arange(n2, dtype=jnp.int32)[:, None] * m) % n).astype(F32) * (2.0 * math.pi / n)
    ca, sa = jnp.cos(ang_a)[:, None, :], jnp.sin(ang_a)[:, None, :]
    cb, sb = jnp.cos(ang_b)[None, :, :], jnp.sin(ang_b)[None, :, :]
    scale = n ** -0.5
    cos_m = ((ca * cb - sa * sb) * scale).reshape(n, n)
    nsin_m = ((sa * cb + ca * sb) * -scale).reshape(n, n)
    return cos_m.astype(BF16), nsin_m.astype(BF16)


def _channel_dft_weight():
    n = FOURIER_GROUP_DIM
    idx = jnp.arange(n, dtype=jnp.int32)
    ang = ((idx[:, None] * idx[None, :]) % n).astype(F32) * (2.0 * math.pi / n)
    eye = jnp.eye(FOURIER_GROUPS, dtype=F32)
    cos_bd = jnp.kron(eye, jnp.cos(ang) * n ** -0.5)
    sin_bd = jnp.kron(eye, jnp.sin(ang) * n ** -0.5)
    return jnp.concatenate([cos_bd, sin_bd], axis=1).astype(BF16)


def _repack_w_in(w_in, lay):
    depth, d, _ = w_in.shape
    widths = (ATT_WIDTH, KV_WIDTH, KV_WIDTH, GLA_K_WIDTH, GLA_K_WIDTH, GLA_V_WIDTH, GLA_V_WIDTH,
              GLA_LOW_RANK, GLA_LOW_RANK, FOURIER_WIDTH, N_BRANCHES * d)
    offs = [0]
    for w in widths:
        offs.append(offs[-1] + w)
    aq, ak, av, gq, gk, gv, gr, lrf, lrb, fu, gates = [w_in[:, :, offs[i]:offs[i + 1]] for i in range(len(widths))]
    used = lay["lr"] + 2 * GLA_LOW_RANK
    pad = jnp.zeros((depth, d, lay["total"] - used), w_in.dtype)
    return jnp.concatenate([aq, gv, gr, fu, gates, gq, gk, ak, av, lrf, lrb, pad], axis=-1).astype(BF16)


def kernel(x_prompt, x_sample, cache_k, cache_v, state_gla, c, c_ctx, ada_w, ada_b, norm1_g, norm2_g, w_in, q_norm_g, k_norm_g, alpha_up, alpha_b, gla_norm_g, w_branch, w_out, ffn_w1, ffn_w3, ffn_w2, router_w, moe_w1, moe_w3, moe_w2, final_g):
    batch, seq, d = x_prompt.shape
    dec_batch, dec_seq, _ = x_sample.shape
    depth = w_in.shape[0]
    past = cache_k.shape[2]
    seg_len = batch * seq
    assert seg_len == dec_seq, "context tokens and each latent request must fill equal segments"
    assert seq % GLA_CHUNK == 0 and dec_seq % GLA_CHUNK == 0
    assert COL_GATES % d == 0
    n_seg = 1 + dec_batch
    t = n_seg * seg_len
    lay = _in_layout(d)

    w_in_p = _repack_w_in(w_in, lay)
    w_branch_b = w_branch.astype(BF16)
    w_out_b = w_out.astype(BF16)
    ffn_w1_b, ffn_w3_b, ffn_w2_b = ffn_w1.astype(BF16), ffn_w3.astype(BF16), ffn_w2.astype(BF16)
    router_pad = jnp.pad(router_w, ((0, 0), (0, 0), (0, LANES - N_EXPERTS)))
    cos_tab, sin_tab = _rope_tables(dec_seq)
    dft_cos_s, dft_nsin_s = _dft_matrices(dec_seq)
    dft_cos_p, dft_nsin_p = _dft_matrices(seq)
    w_chan = _channel_dft_weight()
    ones_bd = jnp.kron(jnp.eye(ATT_HEADS, dtype=F32), jnp.ones((HEAD_DIM, HEAD_DIM), F32)).astype(BF16)
    q_g = jnp.tile(q_norm_g, (1, ATT_HEADS)).reshape(depth, 1, ATT_WIDTH)
    k_g = jnp.tile(k_norm_g, (1, ATT_KV_HEADS)).reshape(depth, 1, KV_WIDTH)
    aup_pad = jnp.zeros((depth, 2, LANES, GLA_K_WIDTH), F32)
    aup_pad = aup_pad.at[:, 0, :GLA_LOW_RANK].set(alpha_up[:, 0])
    aup_pad = aup_pad.at[:, 1, GLA_LOW_RANK:2 * GLA_LOW_RANK].set(alpha_up[:, 1])
    ab = alpha_b.reshape(depth, 2, 1, GLA_K_WIDTH)
    s0_lat = state_gla.transpose(0, 1, 2, 5, 3, 4).reshape(dec_batch, depth, 2, GLA_DV, GLA_K_WIDTH)
    s0_all = jnp.concatenate([jnp.zeros((1,) + s0_lat.shape[1:], F32), s0_lat], axis=0)

    def ctx_keys(x):
        xb = x.astype(BF16)[:, :, :, None, :]
        return jnp.broadcast_to(xb, (dec_batch, past, ATT_KV_HEADS, 2, HEAD_DIM)).reshape(dec_batch, past, 2 * KV_WIDTH)

    def ctx_values_t(x):
        xt = x.astype(BF16).transpose(0, 2, 3, 1)
        return jnp.concatenate([xt, jnp.ones_like(xt)], axis=2).reshape(dec_batch, ATT_KV_HEADS * VT_ROWS, past)

    cond = jnp.zeros((8, d), F32).at[0].set(c_ctx).at[1:1 + dec_batch].set(c)
    mod_all = _modulation(cond, ada_w, ada_b)

    x = jnp.concatenate([x_prompt.reshape(seg_len, d), x_sample.reshape(dec_batch * dec_seq, d)], axis=0)
    new_k, new_v, new_s = [], [], []
    for l in range(depth):
        mod3 = mod_all[l, :n_seg].reshape(n_seg, 1, 6 * d)
        z = _in_projection(x, mod3, norm1_g[l], w_in_p, l, seg_len)
        qn, k_out, k2, vt = _attention_prep(z, lay, cos_tab, sin_tab, q_g[l], k_g[l], ones_bd, seg_len)
        att_p = _attention(qn, k2, vt, None, None, row0=0, n_seq=batch, seq_len=seq, tq=min(ATT_QUERY_TILE, seq))
        att_s = _attention(qn, k2, vt, ctx_keys(cache_k[:, l]), ctx_values_t(cache_v[:, l]),
                           row0=seg_len, n_seq=dec_batch, seq_len=dec_seq, tq=min(ATT_QUERY_TILE, dec_seq))
        gla_o, s_fin = _gla(z, lay, aup_pad[l], ab[l], s0_all[:, l], seg_len=seg_len, n_seg=n_seg, prompt_len=seq)
        xcs = _fourier_channels(z, w_chan)
        four_p = _fourier_positions(xcs, dft_cos_p, dft_nsin_p, row0=0, n_seq=batch, seq_len=seq)
        four_s = _fourier_positions(xcs, dft_cos_s, dft_nsin_s, row0=seg_len, n_seq=dec_batch, seq_len=dec_seq)
        x = _merge(att_p, att_s, gla_o, z, four_p, four_s, x, mod3, gla_norm_g[l], w_branch_b, w_out_b, l, seg_len)
        j = l // 2
        if l % 2 == 0:
            x = _dense_ffn(x, mod3, norm2_g[l], ffn_w1_b, ffn_w3_b, ffn_w2_b, j, seg_len)
        elif l + 1 < depth:
            x = _moe_ffn(x, mod3, norm2_g[l], router_pad[j], moe_w1, moe_w3, moe_w2, j, seg_len)
        else:
            y_p, y_s = _moe_ffn(x, mod3, norm2_g[l], router_pad[j], moe_w1, moe_w3, moe_w2, j, seg_len, final_g=final_g)
        new_k.append(k_out[:seg_len].reshape(batch, seq, ATT_KV_HEADS, HEAD_DIM))
        v_col = lay["akv"] + KV_WIDTH
        new_v.append(z[:seg_len, v_col:v_col + KV_WIDTH].astype(F32).reshape(batch, seq, ATT_KV_HEADS, HEAD_DIM))
        cps = seq // GLA_CHUNK
        sf = s_fin[0, 0, cps - 1::cps]
        sb = s_fin[0, 1, 0::cps]
        st = jnp.stack([sf, sb], axis=1).reshape(batch, 2, GLA_DV, GLA_HEADS, GLA_DK)
        new_s.append(st.transpose(0, 1, 3, 4, 2))
    if depth % 2 == 1:
        y_p = _final_norm(x, final_g, row0=0, n_rows=seg_len)
        y_s = _final_norm(x, final_g, row0=seg_len, n_rows=t - seg_len)
    y_prompt = y_p.reshape(batch, seq, d)
    y_sample = y_s.reshape(dec_batch, dec_seq, d)
    return (y_prompt, y_sample, jnp.stack(new_k, axis=1), jnp.stack(new_v, axis=1), jnp.stack(new_s, axis=1))
```

```python
import functools
import math

import jax
import jax.numpy as jnp
from jax import lax
from jax.experimental import pallas as pl
from jax.experimental.pallas import tpu as pltpu

F32 = jnp.float32
BF16 = jnp.bfloat16

EPS = 1e-6
ATT_HEADS = 8
ATT_KV_HEADS = 2
HEAD_DIM = 64
ATT_WIDTH = ATT_HEADS * HEAD_DIM
KV_WIDTH = ATT_KV_HEADS * HEAD_DIM
ROPE_FREQS = HEAD_DIM // 4
ROPE_THETA = 10000.0
GRID_W = 64
GLA_HEADS = 4
GLA_DK = 64
GLA_DV = 128
GLA_K_WIDTH = GLA_HEADS * GLA_DK
GLA_V_WIDTH = GLA_HEADS * GLA_DV
GLA_LOW_RANK = 16
GLA_TAU = 16.0
FOURIER_GROUPS = 4
FOURIER_GROUP_DIM = 128
FOURIER_WIDTH = FOURIER_GROUPS * FOURIER_GROUP_DIM
N_BRANCHES = 3
N_EXPERTS = 8

LANES = 128
MXU_WIDTH = 256
VMEM_LIMIT_BYTES = 56 * 1024 * 1024

COL_AQ = 0
COL_GV = 512
COL_GR = 1024
COL_FU = 1536
COL_GATES = 2048

GLA_CHUNK = 256
GLA_SAFE_LOG = 60.0


def _cparams(sem):
    return pltpu.CompilerParams(dimension_semantics=sem, vmem_limit_bytes=VMEM_LIMIT_BYTES)


def _resident(shape, index_map):
    return pl.BlockSpec(shape, index_map, pipeline_mode=pl.Buffered(1))


def _split2(x):
    hi = x.astype(BF16)
    lo = (x - hi.astype(F32)).astype(BF16)
    return hi, lo


def _split3(x):
    hi = x.astype(BF16)
    r = x - hi.astype(F32)
    mid = r.astype(BF16)
    lo = (r - mid.astype(F32)).astype(BF16)
    return hi, mid, lo


def _dot(a, b):
    return jnp.dot(a, b, preferred_element_type=F32)


def _dot_nt(a, b):
    return lax.dot_general(a, b, (((1,), (1,)), ((), ())), preferred_element_type=F32)


def _dot_tn(a, b):
    return lax.dot_general(a, b, (((0,), (0,)), ((), ())), preferred_element_type=F32)


def _dot3(a, b):
    a_hi, a_lo = _split2(a)
    b_hi, b_lo = _split2(b)
    return _dot(a_hi, b_hi) + _dot(a_lo, b_hi) + _dot(a_hi, b_lo)


def _sigmoid(x):
    return 1.0 / (1.0 + jnp.exp(-x))


def _silu(x):
    return x * _sigmoid(x)


def _in_layout(d_model):
    gates_end = COL_GATES + N_BRANCHES * d_model
    col_gq = gates_end
    col_gk = col_gq + GLA_K_WIDTH
    col_akv = col_gk + GLA_K_WIDTH
    col_lr = col_akv + 2 * KV_WIDTH
    used = col_lr + LANES
    total = -(-used // 1024) * 1024
    return dict(gq=col_gq, gk=col_gk, akv=col_akv, lr=col_lr, total=total)


def _mod_kernel(cond_ref, w_ref, b_ref, o_ref):
    c = _silu(cond_ref[...])
    o_ref[0] = _dot3(c, w_ref[0]) + b_ref[0]


def _modulation(cond, ada_w, ada_b):
    depth, d, n6 = ada_w.shape
    tn = 1536 if n6 % 1536 == 0 else n6
    rows = cond.shape[0]
    return pl.pallas_call(
        _mod_kernel,
        grid=(depth, n6 // tn),
        in_specs=[
            pl.BlockSpec((rows, d), lambda l, j: (0, 0)),
            pl.BlockSpec((1, d, tn), lambda l, j: (l, 0, j)),
            pl.BlockSpec((1, 1, tn), lambda l, j: (l, 0, j)),
        ],
        out_specs=pl.BlockSpec((1, rows, tn), lambda l, j: (l, 0, j)),
        out_shape=jax.ShapeDtypeStruct((depth, rows, n6), F32),
        compiler_params=_cparams(("parallel", "parallel")),
        name="adaln_mod",
    )(cond, ada_w, ada_b.reshape(depth, 1, n6))


def _norm_mod(x, g, shift, scale):
    ms = jnp.mean(x * x, axis=-1, keepdims=True)
    return x * lax.rsqrt(ms + EPS) * g * (1.0 + scale) + shift


def _inproj_kernel(x_ref, mod_ref, g_ref, w_ref, z_ref, h_scr, *, d):
    @pl.when(pl.program_id(1) == 0)
    def _():
        h = _norm_mod(x_ref[...], g_ref[...], mod_ref[0, :, 0:d], mod_ref[0, :, d:2 * d])
        h_scr[...] = h.astype(BF16)

    z_ref[...] = _dot(h_scr[...], w_ref[...]).astype(z_ref.dtype)


def _in_projection(x, mod3, g, w, layer, seg_len):
    t, d = x.shape
    n = w.shape[2]
    tm = min(1024, seg_len)
    tn = 1024
    tps = seg_len // tm
    return pl.pallas_call(
        functools.partial(_inproj_kernel, d=d),
        grid=(t // tm, n // tn),
        in_specs=[
            pl.BlockSpec((tm, d), lambda i, j: (i, 0)),
            pl.BlockSpec((1, 1, 6 * d), lambda i, j: (i // tps, 0, 0)),
            pl.BlockSpec((1, d), lambda i, j: (0, 0)),
            pl.BlockSpec((None, d, tn), lambda i, j: (layer, 0, j)),
        ],
        out_specs=pl.BlockSpec((tm, tn), lambda i, j: (i, j)),
        out_shape=jax.ShapeDtypeStruct((t, n), BF16),
        scratch_shapes=[pltpu.VMEM((tm, d), BF16)],
        compiler_params=_cparams(("parallel", "arbitrary")),
        name="in_projection",
    )(x, mod3, g.reshape(1, d), w)


def _head_rms(x, g, ones_bd):
    hi, lo = _split2(x * x)
    ss = _dot(hi, ones_bd) + _dot(lo, ones_bd)
    return x * lax.rsqrt(ss * (1.0 / HEAD_DIM) + EPS) * g


def _rope(x, cos, sin_signed):
    width = x.shape[1]
    reps = width // LANES
    if reps > 1:
        cos = jnp.concatenate([cos] * reps, axis=1)
        sin_signed = jnp.concatenate([sin_signed] * reps, axis=1)
    lane = lax.broadcasted_iota(jnp.int32, x.shape, 1)
    first_half = (lane % (2 * ROPE_FREQS)) < ROPE_FREQS
    partner = jnp.where(first_half, pltpu.roll(x, width - ROPE_FREQS, 1), pltpu.roll(x, ROPE_FREQS, 1))
    return x * cos + partner * sin_signed


ATT_KEY_CHUNK = 512
ATT_QUERY_TILE = 256
VT_ROWS = 2 * HEAD_DIM
VT_USED = HEAD_DIM + 16


def _dup_heads(x):
    lane = lax.broadcasted_iota(jnp.int32, x.shape, 1)
    swapped = pltpu.roll(x, HEAD_DIM, 1)
    low = lane < HEAD_DIM
    return jnp.concatenate([jnp.where(low, x, swapped), jnp.where(low, swapped, x)], axis=1)


def _attn_prep_kernel(zq_ref, zkv_ref, cos_ref, sin_ref, qg_ref, kg_ref, bd_ref, q_ref, kout_ref, k2_ref, vt_ref):
    cos = cos_ref[0]
    sin = sin_ref[0]
    bd = bd_ref[...]
    q = _head_rms(zq_ref[...].astype(F32), qg_ref[...], bd)
    q_ref[...] = (_rope(q, cos, sin) * (HEAD_DIM ** -0.5)).astype(BF16)
    kv = zkv_ref[...].astype(F32)
    k = _head_rms(kv[:, :KV_WIDTH], kg_ref[...], bd[:KV_WIDTH, :KV_WIDTH])
    kout_ref[...] = k
    k2_ref[...] = _dup_heads(_rope(k, cos, sin)).astype(BF16)
    vt = kv[:, KV_WIDTH:].T
    ones = jnp.ones((HEAD_DIM, vt.shape[1]), F32)
    vt_ref[...] = jnp.concatenate([vt[:HEAD_DIM], ones, vt[HEAD_DIM:], ones], axis=0).astype(BF16)


def _attention_prep(z, lay, cos_tab, sin_tab, q_g, k_g, ones_bd, seg_len):
    t = z.shape[0]
    tm = min(512, seg_len)
    tps = seg_len // tm
    akv_blk = lay["akv"] // (2 * KV_WIDTH)
    tab_spec = pl.BlockSpec((1, tm, LANES), lambda i: (jnp.minimum(i // tps, 1), i % tps, 0))
    return pl.pallas_call(
        _attn_prep_kernel,
        grid=(t // tm,),
        in_specs=[
            pl.BlockSpec((tm, ATT_WIDTH), lambda i: (i, COL_AQ // ATT_WIDTH)),
            pl.BlockSpec((tm, 2 * KV_WIDTH), lambda i: (i, akv_blk)),
            tab_spec,
            tab_spec,
            pl.BlockSpec((1, ATT_WIDTH), lambda i: (0, 0)),
            pl.BlockSpec((1, KV_WIDTH), lambda i: (0, 0)),
            pl.BlockSpec((ATT_WIDTH, ATT_WIDTH), lambda i: (0, 0)),
        ],
        out_specs=[
            pl.BlockSpec((tm, ATT_WIDTH), lambda i: (i, 0)),
            pl.BlockSpec((tm, KV_WIDTH), lambda i: (i, 0)),
            pl.BlockSpec((tm, 2 * KV_WIDTH), lambda i: (i, 0)),
            pl.BlockSpec((ATT_KV_HEADS * VT_ROWS, tm), lambda i: (0, i)),
        ],
        out_shape=[
            jax.ShapeDtypeStruct((t, ATT_WIDTH), BF16),
            jax.ShapeDtypeStruct((t, KV_WIDTH), F32),
            jax.ShapeDtypeStruct((t, 2 * KV_WIDTH), BF16),
            jax.ShapeDtypeStruct((ATT_KV_HEADS * VT_ROWS, t), BF16),
        ],
        compiler_params=_cparams(("parallel",)),
        name="attention_prep",
    )(z, z, cos_tab, sin_tab, q_g, k_g, ones_bd)


def _attn_kernel(*refs, has_ctx):
    if has_ctx:
        q_ref, k_ref, v_ref, ck_ref, cv_ref, o_ref = refs
    else:
        q_ref, k_ref, v_ref, o_ref = refs
    tq = q_ref.shape[0]
    sources = [(k_ref, v_ref)] + ([(ck_ref.at[0], cv_ref.at[0])] if has_ctx else [])
    chunks = []
    for kr, vr in sources:
        n = min(ATT_KEY_CHUNK, kr.shape[0])
        chunks += [(kr, vr, c * n, n) for c in range(kr.shape[0] // n)]
    low = lax.broadcasted_iota(jnp.int32, (tq, LANES), 1) < HEAD_DIM
    zero = jnp.zeros((tq, LANES), BF16)
    n_heads = ATT_HEADS // ATT_KV_HEADS
    ws = []
    for h in range(n_heads):
        q2 = q_ref[:, (h // 2) * LANES:(h // 2 + 1) * LANES]
        ws.append(jnp.where(low, q2, zero) if h % 2 == 0 else jnp.where(low, zero, q2))
    w = jnp.concatenate(ws, axis=0)
    maxes, parts = [], []
    scores = lambda c: _dot_nt(c[0][c[2]:c[2] + c[3], :], w)
    st_next = scores(chunks[0])
    for idx, (kr, vr, r0, n) in enumerate(chunks):
        st = st_next
        if idx + 1 < len(chunks):
            st_next = scores(chunks[idx + 1])
        mc = jnp.max(st, axis=0, keepdims=True)
        p = jnp.exp(st - mc).astype(BF16)
        maxes.append(mc)
        parts.append(_dot(vr[0:VT_USED, r0:r0 + n], p))
    m = functools.reduce(jnp.maximum, maxes)
    acc = sum(jnp.exp(mc - m) * part for mc, part in zip(maxes, parts))
    o_t = acc[:HEAD_DIM] / acc[HEAD_DIM:HEAD_DIM + 1]
    o_heads = jnp.concatenate([o_t[:, h * tq:(h + 1) * tq] for h in range(n_heads)], axis=0)
    o_ref[...] = o_heads.T.astype(BF16)


def _attention(q, k2, v2, ctx_k2, ctx_v2, *, row0, n_seq, seq_len, tq):
    ntq = seq_len // tq
    gw = ATT_WIDTH // ATT_KV_HEADS
    qblk0 = row0 // tq
    kblk0 = row0 // seq_len
    has_ctx = ctx_k2 is not None
    in_specs = [
        pl.BlockSpec((tq, gw), lambda b, g, i: (qblk0 + b * ntq + i, g)),
        pl.BlockSpec((seq_len, LANES), lambda b, g, i: (kblk0 + b, g)),
        pl.BlockSpec((VT_ROWS, seq_len), lambda b, g, i: (g, kblk0 + b)),
    ]
    args = [q, k2, v2]
    if has_ctx:
        past = ctx_k2.shape[1]
        in_specs += [pl.BlockSpec((1, past, LANES), lambda b, g, i: (b, 0, g)),
                     pl.BlockSpec((1, VT_ROWS, past), lambda b, g, i: (b, g, 0))]
        args += [ctx_k2, ctx_v2]
    return pl.pallas_call(
        functools.partial(_attn_kernel, has_ctx=has_ctx),
        grid=(n_seq, ATT_KV_HEADS, ntq),
        in_specs=in_specs,
        out_specs=pl.BlockSpec((tq, gw), lambda b, g, i: (b * ntq + i, g)),
        out_shape=jax.ShapeDtypeStruct((n_seq * seq_len, ATT_WIDTH), BF16),
        compiler_params=_cparams(("parallel", "parallel", "parallel")),
        name="attention_ctx" if has_ctx else "attention_self",
    )(*args)


def _gla_kernel(q_ref, k_ref, v_ref, lr_ref, aup_ref, ab_ref, s0_ref, o_ref, sfin_ref,
                s_scr, la_scr, qf_scr, kf_scr, vt_scr, ot_scr, *, chunks_per_seq):
    seg = pl.program_id(0)
    direction = pl.program_id(1)
    n = pl.program_id(2)
    c = q_ref.shape[0]
    kw = GLA_K_WIDTH
    fwd = direction == 0

    @pl.when(n % chunks_per_seq == 0)
    def _():
        s_scr[...] = s0_ref[0, 0]

    aup_hi, aup_lo = _split2(aup_ref[0])
    lr = lr_ref[...]
    x = _dot(lr, aup_hi) + _dot(lr, aup_lo) + ab_ref[0]
    la = (jnp.minimum(x, 0.0) - jnp.log(1.0 + jnp.exp(-jnp.abs(x)))) * (1.0 / GLA_TAU)

    row = lax.broadcasted_iota(jnp.int32, (c, c), 0)
    col = lax.broadcasted_iota(jnp.int32, (c, c), 1)
    tri = jnp.where(fwd, row - col, col - row) >= 0
    tri_b = tri.astype(BF16)
    la_hi, la_mid, la_lo = _split3(la)
    cum = _dot(tri_b, la_hi) + _dot(tri_b, la_mid) + _dot(tri_b, la_lo)
    bend = jnp.sum(la, axis=0, keepdims=True)
    safe = jnp.min(cum) >= -GLA_SAFE_LOG
    head_of_lane = lax.broadcasted_iota(jnp.int32, (1, kw), 1) // GLA_DK

    @pl.when(safe)
    def _():
        q = q_ref[...].astype(F32) * (GLA_DK ** -0.5)
        k = k_ref[...].astype(F32)
        vb = v_ref[...]
        qt = (q * jnp.exp(cum)).astype(BF16)
        kt = (k * jnp.exp(-cum)).astype(BF16)
        ke = (k * jnp.exp(bend - cum)).astype(BF16)
        s_old = s_scr[...]
        s_b = s_old.astype(BF16)
        s_new = jnp.exp(bend) * s_old
        zero = jnp.zeros_like(qt)
        for h in range(GLA_HEADS):
            hm = head_of_lane == h
            qh = jnp.where(hm, qt, zero)
            a = jnp.where(tri, _dot_nt(qh, kt), 0.0).astype(BF16)
            vh = vb[:, h * GLA_DV:(h + 1) * GLA_DV]
            o_ref[0, :, h * GLA_DV:(h + 1) * GLA_DV] = _dot(a, vh) + _dot_nt(qh, s_b)
            s_new = s_new + _dot_tn(vh, jnp.where(hm, ke, zero))
        s_scr[...] = s_new
        sfin_ref[0, 0, 0] = s_new

    @pl.when(jnp.logical_not(safe))
    def _():
        la_scr[...] = la
        qf_scr[...] = q_ref[...].astype(F32) * (GLA_DK ** -0.5)
        kf_scr[...] = k_ref[...].astype(F32)
        vt_scr[...] = v_ref[...].astype(F32).T
        ot_scr[...] = jnp.zeros_like(ot_scr)
        lane_c = lax.broadcasted_iota(jnp.int32, (1, c), 1)

        def body(i, s):
            t = jnp.where(fwd, i, c - 1 - i)
            a_t = jnp.exp(la_scr[pl.ds(t, 1), :])
            k_t = kf_scr[pl.ds(t, 1), :]
            q_t = qf_scr[pl.ds(t, 1), :]
            onehot = lane_c == t
            vmat = jnp.zeros((GLA_DV, kw), F32)
            for h in range(GLA_HEADS):
                vcol = jnp.sum(jnp.where(onehot, vt_scr[h * GLA_DV:(h + 1) * GLA_DV, :], 0.0), axis=1, keepdims=True)
                vmat = jnp.where(head_of_lane == h, vcol, vmat)
            s = a_t * s + vmat * k_t
            prod = s * q_t
            for h in range(GLA_HEADS):
                ocol = jnp.sum(jnp.where(head_of_lane == h, prod, 0.0), axis=1, keepdims=True)
                rows = slice(h * GLA_DV, (h + 1) * GLA_DV)
                ot_scr[rows, :] = jnp.where(onehot, ocol, ot_scr[rows, :])
            return s

        s_new = lax.fori_loop(0, c, body, s_scr[...])
        o_ref[0] = ot_scr[...].T
        s_scr[...] = s_new
        sfin_ref[0, 0, 0] = s_new


def _gla(z, lay, aup_pad, ab, s0, *, seg_len, n_seg, prompt_len):
    t = z.shape[0]
    c = GLA_CHUNK
    nc = seg_len // c
    cps_prompt = prompt_len // c
    gq_blk = lay["gq"] // GLA_K_WIDTH
    gk_blk = lay["gk"] // GLA_K_WIDTH
    gv_blk = COL_GV // GLA_V_WIDTH
    lr_blk = lay["lr"] // LANES

    def chunk(seg, d, n):
        return seg * nc + jnp.where(d == 0, n, nc - 1 - n)

    def kernel(*refs):
        cps = jnp.where(pl.program_id(0) == 0, cps_prompt, nc)
        _gla_kernel(*refs, chunks_per_seq=cps)

    return pl.pallas_call(
        kernel,
        grid=(n_seg, 2, nc),
        in_specs=[
            pl.BlockSpec((c, GLA_K_WIDTH), lambda s, d, n: (chunk(s, d, n), gq_blk)),
            pl.BlockSpec((c, GLA_K_WIDTH), lambda s, d, n: (chunk(s, d, n), gk_blk)),
            pl.BlockSpec((c, GLA_V_WIDTH), lambda s, d, n: (chunk(s, d, n), gv_blk)),
            pl.BlockSpec((c, LANES), lambda s, d, n: (chunk(s, d, n), lr_blk)),
            pl.BlockSpec((1, LANES, GLA_K_WIDTH), lambda s, d, n: (d, 0, 0)),
            pl.BlockSpec((1, 1, GLA_K_WIDTH), lambda s, d, n: (d, 0, 0)),
            pl.BlockSpec((1, 1, GLA_DV, GLA_K_WIDTH), lambda s, d, n: (s, d, 0, 0)),
        ],
        out_specs=[
            pl.BlockSpec((1, c, GLA_V_WIDTH), lambda s, d, n: (d, chunk(s, d, n), 0)),
            pl.BlockSpec((1, 1, 1, GLA_DV, GLA_K_WIDTH), lambda s, d, n: (s, d, jnp.where(d == 0, n, nc - 1 - n), 0, 0)),
        ],
        out_shape=[
            jax.ShapeDtypeStruct((2, t, GLA_V_WIDTH), F32),
            jax.ShapeDtypeStruct((n_seg, 2, nc, GLA_DV, GLA_K_WIDTH), F32),
        ],
        scratch_shapes=[
            pltpu.VMEM((GLA_DV, GLA_K_WIDTH), F32),
            pltpu.VMEM((c, GLA_K_WIDTH), F32),
            pltpu.VMEM((c, GLA_K_WIDTH), F32),
            pltpu.VMEM((c, GLA_K_WIDTH), F32),
            pltpu.VMEM((GLA_V_WIDTH, c), F32),
            pltpu.VMEM((GLA_V_WIDTH, c), F32),
        ],
        compiler_params=_cparams(("parallel", "parallel", "arbitrary")),
        name="gla_scan",
    )(z, z, z, z, aup_pad, ab, s0)


def _four_chan_kernel(u_ref, w_ref, o_ref):
    o_ref[...] = _dot(u_ref[...], w_ref[...]).astype(BF16)


def _fourier_channels(z, w_chan):
    t = z.shape[0]
    tm = 512
    return pl.pallas_call(
        _four_chan_kernel,
        grid=(t // tm,),
        in_specs=[
            pl.BlockSpec((tm, FOURIER_WIDTH), lambda i: (i, COL_FU // FOURIER_WIDTH)),
            _resident((FOURIER_WIDTH, 2 * FOURIER_WIDTH), lambda i: (0, 0)),
        ],
        out_specs=pl.BlockSpec((tm, 2 * FOURIER_WIDTH), lambda i: (i, 0)),
        out_shape=jax.ShapeDtypeStruct((t, 2 * FOURIER_WIDTH), BF16),
        compiler_params=_cparams(("parallel",)),
        name="fourier_channels",
    )(z, w_chan)


def _four_pos_kernel(c_ref, s_ref, xc_ref, xs_ref, o_ref):
    o_ref[...] = (_dot(c_ref[...], xc_ref[...]) + _dot(s_ref[...], xs_ref[...])).astype(BF16)


def _fourier_positions(xcs, cos_m, nsin_m, *, row0, n_seq, seq_len):
    tm = min(512, seq_len)
    nt = seq_len // tm
    blk0 = row0 // seq_len
    return pl.pallas_call(
        _four_pos_kernel,
        grid=(n_seq, nt),
        in_specs=[
            pl.BlockSpec((tm, seq_len), lambda b, i: (i, 0)),
            pl.BlockSpec((tm, seq_len), lambda b, i: (i, 0)),
            pl.BlockSpec((seq_len, FOURIER_WIDTH), lambda b, i: (blk0 + b, 0)),
            pl.BlockSpec((seq_len, FOURIER_WIDTH), lambda b, i: (blk0 + b, 1)),
        ],
        out_specs=pl.BlockSpec((tm, FOURIER_WIDTH), lambda b, i: (b * nt + i, 0)),
        out_shape=jax.ShapeDtypeStruct((n_seq * seq_len, FOURIER_WIDTH), BF16),
        compiler_params=_cparams(("parallel", "parallel")),
        name="fourier_positions",
    )(cos_m, nsin_m, xcs, xcs)


def _merge_kernel(attp_ref, atts_ref, gof_ref, gob_ref, gr_ref, fourp_ref, fours_ref, g0_ref, g1_ref, g2_ref, x_ref,
                  mod_ref, gg_ref, wb_ref, wo_ref, o_ref, *, d, prompt_tiles):
    in_prompt = pl.program_id(0) < prompt_tiles
    att = jnp.where(in_prompt, attp_ref[...], atts_ref[...])
    four = jnp.where(in_prompt, fourp_ref[...], fours_ref[...])
    o = gof_ref[0] + gob_ref[0]
    gg = gg_ref[...]
    parts = []
    for h in range(GLA_HEADS):
        oh = o[:, h * GLA_DV:(h + 1) * GLA_DV]
        ms = jnp.mean(oh * oh, axis=-1, keepdims=True)
        parts.append(oh * lax.rsqrt(ms + EPS) * gg)
    gla = (jnp.concatenate(parts, axis=1) * _silu(gr_ref[...].astype(F32))).astype(BF16)
    merged = (_sigmoid(g0_ref[...].astype(F32)) * _dot(att, wb_ref[0])
              + _sigmoid(g1_ref[...].astype(F32)) * _dot(gla, wb_ref[1])
              + _sigmoid(g2_ref[...].astype(F32)) * _dot(four, wb_ref[2]))
    mix = _dot(merged.astype(BF16), wo_ref[...])
    o_ref[...] = x_ref[...] + mod_ref[0, :, 2 * d:3 * d] * mix


def _merge(att_p, att_s, gla_o, z, four_p, four_s, x, mod3, gla_g, w_branch, w_out, layer, seg_len):
    t, d = x.shape
    tm = min(512, seg_len)
    tps = seg_len // tm
    bw = att_p.shape[1]
    gate_blk = COL_GATES // d
    prompt_tiles = att_p.shape[0] // tm
    p_spec = pl.BlockSpec((tm, bw), lambda i: (jnp.minimum(i, prompt_tiles - 1), 0))
    s_spec = pl.BlockSpec((tm, bw), lambda i: (jnp.maximum(i - prompt_tiles, 0), 0))
    return pl.pallas_call(
        functools.partial(_merge_kernel, d=d, prompt_tiles=prompt_tiles),
        grid=(t // tm,),
        in_specs=[
            p_spec,
            s_spec,
            pl.BlockSpec((1, tm, GLA_V_WIDTH), lambda i: (0, i, 0)),
            pl.BlockSpec((1, tm, GLA_V_WIDTH), lambda i: (1, i, 0)),
            pl.BlockSpec((tm, GLA_V_WIDTH), lambda i: (i, COL_GR // GLA_V_WIDTH)),
            p_spec,
            s_spec,
            pl.BlockSpec((tm, d), lambda i: (i, gate_blk)),
            pl.BlockSpec((tm, d), lambda i: (i, gate_blk + 1)),
            pl.BlockSpec((tm, d), lambda i: (i, gate_blk + 2)),
            pl.BlockSpec((tm, d), lambda i: (i, 0)),
            pl.BlockSpec((1, 1, 6 * d), lambda i: (i // tps, 0, 0)),
            pl.BlockSpec((1, GLA_DV), lambda i: (0, 0)),
            _resident((None, N_BRANCHES, bw, d), lambda i: (layer, 0, 0, 0)),
            _resident((None, d, d), lambda i: (layer, 0, 0)),
        ],
        out_specs=pl.BlockSpec((tm, d), lambda i: (i, 0)),
        out_shape=jax.ShapeDtypeStruct((t, d), F32),
        compiler_params=_cparams(("parallel",)),
        name="branch_merge",
    )(att_p, att_s, gla_o, gla_o, z, four_p, four_s, z, z, z, x, mod3, gla_g.reshape(1, GLA_DV), w_branch, w_out)


def _ffn_kernel(x_ref, mod_ref, g_ref, w1_ref, w3_ref, w2_ref, o_ref, *, d):
    x = x_ref[...]
    h = _norm_mod(x, g_ref[...], mod_ref[0, :, 3 * d:4 * d], mod_ref[0, :, 4 * d:5 * d]).astype(BF16)
    a = _dot(h, w1_ref[...])
    u = (_silu(a) * _dot(h, w3_ref[...])).astype(BF16)
    o_ref[...] = x + mod_ref[0, :, 5 * d:6 * d] * _dot(u, w2_ref[...])


def _dense_ffn(x, mod3, g, w1, w3, w2, layer, seg_len):
    t, d = x.shape
    dff = w1.shape[2]
    tm = min(512, seg_len)
    tps = seg_len // tm
    return pl.pallas_call(
        functools.partial(_ffn_kernel, d=d),
        grid=(t // tm,),
        in_specs=[
            pl.BlockSpec((tm, d), lambda i: (i, 0)),
            pl.BlockSpec((1, 1, 6 * d), lambda i: (i // tps, 0, 0)),
            pl.BlockSpec((1, d), lambda i: (0, 0)),
            _resident((None, d, dff), lambda i: (layer, 0, 0)),
            _resident((None, d, dff), lambda i: (layer, 0, 0)),
            _resident((None, dff, d), lambda i: (layer, 0, 0)),
        ],
        out_specs=pl.BlockSpec((tm, d), lambda i: (i, 0)),
        out_shape=jax.ShapeDtypeStruct((t, d), F32),
        compiler_params=_cparams(("parallel",)),
        name="dense_ffn",
    )(x, mod3, g.reshape(1, d), w1, w3, w2)


MOE_TILE = 512
MOE_UP_CHUNK = 512
ROUTE_E1, ROUTE_E2, ROUTE_R1, ROUTE_R2, ROUTE_W1, ROUTE_W2 = range(6)


def _route_kernel(x_ref, mod_ref, g_ref, rw_ref, h_ref, route_ref, count_ref, carry_scr, *, d):
    @pl.when(pl.program_id(0) == 0)
    def _():
        carry_scr[...] = jnp.zeros_like(carry_scr)

    h = _norm_mod(x_ref[...], g_ref[...], mod_ref[0, :, 3 * d:4 * d], mod_ref[0, :, 4 * d:5 * d])
    _store_as_tiles(h_ref, h)
    logits = _dot3(h, rw_ref[...])
    tm = logits.shape[0]
    lane = lax.broadcasted_iota(jnp.int32, logits.shape, 1).astype(F32)
    neg = jnp.float32(-jnp.inf)
    l1 = jnp.where(lane < N_EXPERTS, logits, neg)
    m1 = jnp.max(l1, axis=-1, keepdims=True)
    i1 = jnp.min(jnp.where(l1 == m1, lane, float(LANES)), axis=-1, keepdims=True)
    l2 = jnp.where(lane == i1, neg, l1)
    m2 = jnp.max(l2, axis=-1, keepdims=True)
    i2 = jnp.min(jnp.where(l2 == m2, lane, float(LANES)), axis=-1, keepdims=True)
    e2 = jnp.exp(m2 - m1)
    w1 = 1.0 / (1.0 + e2)
    w2 = e2 * w1
    onehot = jnp.where((lane == i1) | (lane == i2), 1.0, 0.0)
    row = lax.broadcasted_iota(jnp.int32, (tm, tm), 0)
    col = lax.broadcasted_iota(jnp.int32, (tm, tm), 1)
    before = jnp.where(row > col, 1.0, 0.0).astype(BF16)
    rank = _dot(before, onehot.astype(BF16)) + carry_scr[...]
    r1 = jnp.sum(jnp.where(lane == i1, rank, 0.0), axis=-1, keepdims=True)
    r2 = jnp.sum(jnp.where(lane == i2, rank, 0.0), axis=-1, keepdims=True)
    rec = jnp.zeros_like(logits)
    for idx, val in ((ROUTE_E1, i1), (ROUTE_E2, i2), (ROUTE_R1, r1), (ROUTE_R2, r2), (ROUTE_W1, w1), (ROUTE_W2, w2)):
        rec = jnp.where(lane == idx, val, rec)
    route_ref[...] = rec
    total = carry_scr[...] + jnp.sum(onehot, axis=0, keepdims=True)
    carry_scr[...] = total
    count_ref[...] = total


def _route(x, mod3, g, rw_pad, seg_len):
    t, d = x.shape
    tm = min(512, seg_len)
    tps = seg_len // tm
    return pl.pallas_call(
        functools.partial(_route_kernel, d=d),
        grid=(t // tm,),
        in_specs=[
            pl.BlockSpec((tm, d), lambda i: (i, 0)),
            pl.BlockSpec((1, 1, 6 * d), lambda i: (i // tps, 0, 0)),
            pl.BlockSpec((1, d), lambda i: (0, 0)),
            pl.BlockSpec((d, LANES), lambda i: (0, 0)),
        ],
        out_specs=[
            pl.BlockSpec((tm * TILE_ROWS, LANES), lambda i: (i, 0)),
            pl.BlockSpec((tm, LANES), lambda i: (i, 0)),
            pl.BlockSpec((1, LANES), lambda i: (0, 0)),
        ],
        out_shape=[
            jax.ShapeDtypeStruct((t * TILE_ROWS, LANES), F32),
            jax.ShapeDtypeStruct((t, LANES), F32),
            jax.ShapeDtypeStruct((1, LANES), F32),
        ],
        scratch_shapes=[pltpu.VMEM((1, LANES), F32)],
        compiler_params=_cparams(("arbitrary",)),
        name="moe_route",
    )(x, mod3, g.reshape(1, d), rw_pad)


DMA_UNROLL = 8


TILE_ROWS = 8


def _store_as_tiles(ref, x):
    rows = x.shape[0]
    for c in range(TILE_ROWS):
        ref[pl.ds(c, rows, stride=TILE_ROWS), :] = x[:, c * LANES:(c + 1) * LANES]


def _tile_lane_block(ref, rows, c):
    return ref[pl.ds(c, rows, stride=TILE_ROWS), :]


def _row_tile(ref, row):
    return ref.at[pl.ds(pl.multiple_of(row * TILE_ROWS, TILE_ROWS), TILE_ROWS), :]


def _start_row_gather(src_hbm, row_of, dst, sem, n):
    def body(i, carry):
        pltpu.make_async_copy(_row_tile(src_hbm, row_of(i)), _row_tile(dst, i), sem).start()
        return carry

    lax.fori_loop(0, n, body, 0, unroll=DMA_UNROLL)


def _wait_row_gather(src_hbm, dst, sem, n):
    def body(i, carry):
        pltpu.make_async_copy(_row_tile(src_hbm, 0), _row_tile(dst, 0), sem).wait()
        return carry

    lax.fori_loop(0, n, body, 0, unroll=DMA_UNROLL)


def _gather_kernel(p1_ref, p2_ref, cs_ref, nv_ref, h_hbm, o_ref, tok_smem, buf, sem):
    j = pl.program_id(0)
    tm = o_ref.shape[0]
    n_tok = p1_ref.shape[0]
    n_valid = nv_ref[0]

    def issue(tile):
        cs = cs_ref[tile]
        slot = tile % 2
        _start_row_gather(h_hbm, lambda i: tok_smem[jnp.minimum(cs + i, 2 * n_tok - 1)], buf.at[slot], sem.at[slot], tm)

    @pl.when(j == 0)
    def _():
        def inv(t, carry):
            tok_smem[p1_ref[t]] = t
            tok_smem[p2_ref[t]] = t
            return carry

        lax.fori_loop(0, n_tok, inv, 0, unroll=DMA_UNROLL)
        issue(0)

    @pl.when(j + 1 < n_valid)
    def _():
        issue(j + 1)

    @pl.when(j < n_valid)
    def _():
        slot = j % 2
        _wait_row_gather(h_hbm, buf.at[slot], sem.at[slot], tm)
        for c in range(TILE_ROWS):
            o_ref[:, c * LANES:(c + 1) * LANES] = _tile_lane_block(buf.at[slot], tm, c).astype(BF16)

    @pl.when(j >= n_valid)
    def _():
        o_ref[...] = jnp.zeros_like(o_ref)


def _gather_sorted(h, pos1, pos2, tile_cs, n_valid):
    t = h.shape[0] // TILE_ROWS
    d = TILE_ROWS * LANES
    nt = tile_cs.shape[0]
    return pl.pallas_call(
        _gather_kernel,
        grid_spec=pltpu.PrefetchScalarGridSpec(
            num_scalar_prefetch=4,
            grid=(nt,),
            in_specs=[pl.BlockSpec(memory_space=pl.ANY)],
            out_specs=pl.BlockSpec((MOE_TILE, d), lambda j, a, b, c, nv: (j, 0)),
            scratch_shapes=[
                pltpu.SMEM((2 * t,), jnp.int32),
                pltpu.VMEM((2, MOE_TILE * TILE_ROWS, LANES), F32),
                pltpu.SemaphoreType.DMA((2,)),
            ],
        ),
        out_shape=jax.ShapeDtypeStruct((nt * MOE_TILE, d), BF16),
        compiler_params=_cparams(("arbitrary",)),
        name="moe_gather",
    )(pos1, pos2, tile_cs, n_valid, h)


def _expert_changed(te_ref, j):
    prev = te_ref[jnp.maximum(j - 1, 0)]
    return (j == 0) | (te_ref[j] != prev)


def _expert_up_kernel(te_ref, nv_ref, xs_ref, w1_ref, w3_ref, u_ref, w1b_scr, w3b_scr):
    j = pl.program_id(1)

    @pl.when(_expert_changed(te_ref, j))
    def _():
        w1b_scr[...] = w1_ref[0].astype(BF16)
        w3b_scr[...] = w3_ref[0].astype(BF16)

    @pl.when(j < nv_ref[0])
    def _():
        xs = xs_ref[...]
        tf = u_ref.shape[1]
        col_chunks = [slice(c0, min(c0 + MOE_UP_CHUNK, tf)) for c0 in range(0, tf, MOE_UP_CHUNK)]
        both = lambda cols: (_dot(xs, w1b_scr[:, cols]), _dot(xs, w3b_scr[:, cols]))
        nxt = both(col_chunks[0])
        for idx, cols in enumerate(col_chunks):
            a, b = nxt
            if idx + 1 < len(col_chunks):
                nxt = both(col_chunks[idx + 1])
            u_ref[:, cols] = (_silu(a) * b).astype(BF16)

    @pl.when(j >= nv_ref[0])
    def _():
        u_ref[...] = jnp.zeros_like(u_ref)


def _expert_up(xs, tile_expert, n_valid, w1, w3, layer):
    s, d = xs.shape
    dff = w1.shape[3]
    tf = dff // 2 if (dff // 2) % MXU_WIDTH == 0 else dff
    nt = s // MOE_TILE

    def rows(f, j, te, nv):
        return (jnp.minimum(j, nv[0] - 1), 0)

    return pl.pallas_call(
        _expert_up_kernel,
        grid_spec=pltpu.PrefetchScalarGridSpec(
            num_scalar_prefetch=2,
            grid=(dff // tf, nt),
            in_specs=[
                pl.BlockSpec((MOE_TILE, d), rows),
                pl.BlockSpec((None, 1, d, tf), lambda f, j, te, nv: (layer, te[j], 0, f)),
                pl.BlockSpec((None, 1, d, tf), lambda f, j, te, nv: (layer, te[j], 0, f)),
            ],
            out_specs=pl.BlockSpec((MOE_TILE, tf), lambda f, j, te, nv: (j, f)),
            scratch_shapes=[pltpu.VMEM((d, tf), BF16), pltpu.VMEM((d, tf), BF16)],
        ),
        out_shape=jax.ShapeDtypeStruct((s, dff), BF16),
        compiler_params=_cparams(("arbitrary", "arbitrary")),
        name="moe_expert_up",
    )(tile_expert, n_valid, xs, w1, w3)


def _expert_down_kernel(te_ref, nv_ref, u_ref, w2_ref, y_ref, w2b_scr):
    j = pl.program_id(1)

    @pl.when(_expert_changed(te_ref, j))
    def _():
        w2b_scr[...] = w2_ref[0].astype(BF16)

    @pl.when(j < nv_ref[0])
    def _():
        _store_as_tiles(y_ref, _dot(u_ref[...], w2b_scr[...]))

    @pl.when(j >= nv_ref[0])
    def _():
        y_ref[...] = jnp.zeros_like(y_ref)


def _expert_down(u, tile_expert, n_valid, w2, layer):
    s, dff = u.shape
    d = w2.shape[3]
    nt = s // MOE_TILE
    return pl.pallas_call(
        _expert_down_kernel,
        grid_spec=pltpu.PrefetchScalarGridSpec(
            num_scalar_prefetch=2,
            grid=(1, nt),
            in_specs=[
                pl.BlockSpec((MOE_TILE, dff), lambda n, j, te, nv: (jnp.minimum(j, nv[0] - 1), 0)),
                pl.BlockSpec((None, 1, dff, d), lambda n, j, te, nv: (layer, te[j], 0, 0)),
            ],
            out_specs=pl.BlockSpec((MOE_TILE * TILE_ROWS, LANES), lambda n, j, te, nv: (j, 0)),
            scratch_shapes=[pltpu.VMEM((dff, d), BF16)],
        ),
        out_shape=jax.ShapeDtypeStruct((s * TILE_ROWS, LANES), F32),
        compiler_params=_cparams(("arbitrary", "arbitrary")),
        name="moe_expert_down",
    )(tile_expert, n_valid, u, w2)


def _combine_kernel(d1_ref, d2_ref, x_ref, mod_ref, route_ref, fg_ref, ys_hbm, o_ref, buf1, buf2, sem, *,
                    d, tile0, final_norm):
    tc = x_ref.shape[0]
    j = pl.program_id(0)

    def issue(step):
        slot = step % 2
        base = (tile0 + step) * tc
        _start_row_gather(ys_hbm, lambda i: d1_ref[base + i], buf1.at[slot], sem.at[0, slot], tc)
        _start_row_gather(ys_hbm, lambda i: d2_ref[base + i], buf2.at[slot], sem.at[1, slot], tc)

    @pl.when(j == 0)
    def _():
        issue(0)

    @pl.when(j + 1 < pl.num_programs(0))
    def _():
        issue(j + 1)

    slot = j % 2
    _wait_row_gather(ys_hbm, buf1.at[slot], sem.at[0, slot], tc)
    _wait_row_gather(ys_hbm, buf2.at[slot], sem.at[1, slot], tc)
    route = route_ref[...]
    lane = lax.broadcasted_iota(jnp.int32, route.shape, 1)
    w1 = jnp.sum(jnp.where(lane == ROUTE_W1, route, 0.0), axis=-1, keepdims=True)
    w2 = jnp.sum(jnp.where(lane == ROUTE_W2, route, 0.0), axis=-1, keepdims=True)
    gate = mod_ref[0, :, 5 * d:6 * d]
    blocks = []
    for c in range(TILE_ROWS):
        cols = slice(c * LANES, (c + 1) * LANES)
        e1 = _tile_lane_block(buf1.at[slot], tc, c)
        e2 = _tile_lane_block(buf2.at[slot], tc, c)
        blocks.append(x_ref[:, cols] + gate[:, cols] * (w1 * e1 + w2 * e2))
    if final_norm:
        ms = sum(jnp.sum(b * b, axis=-1, keepdims=True) for b in blocks) * (1.0 / d)
        inv = lax.rsqrt(ms + EPS)
        blocks = [b * inv * fg_ref[:, c * LANES:(c + 1) * LANES] for c, b in enumerate(blocks)]
    for c, b in enumerate(blocks):
        o_ref[:, c * LANES:(c + 1) * LANES] = b


def _combine(x, mod3, route, ys, dest1, dest2, seg_len, *, row0=0, n_rows=None, final_g=None):
    t, d = x.shape
    n_rows = t if n_rows is None else n_rows
    tc = 256
    tps = seg_len // tc
    tile0 = row0 // tc
    final_norm = final_g is not None
    fg = (final_g if final_norm else jnp.ones((d,), F32)).reshape(1, d)
    return pl.pallas_call(
        functools.partial(_combine_kernel, d=d, tile0=tile0, final_norm=final_norm),
        grid_spec=pltpu.PrefetchScalarGridSpec(
            num_scalar_prefetch=2,
            grid=(n_rows // tc,),
            in_specs=[
                pl.BlockSpec((tc, d), lambda i, a, b: (tile0 + i, 0)),
                pl.BlockSpec((1, 1, 6 * d), lambda i, a, b: ((tile0 + i) // tps, 0, 0)),
                pl.BlockSpec((tc, LANES), lambda i, a, b: (tile0 + i, 0)),
                pl.BlockSpec((1, d), lambda i, a, b: (0, 0)),
                pl.BlockSpec(memory_space=pl.ANY),
            ],
            out_specs=pl.BlockSpec((tc, d), lambda i, a, b: (i, 0)),
            scratch_shapes=[pltpu.VMEM((2, tc * TILE_ROWS, LANES), F32), pltpu.VMEM((2, tc * TILE_ROWS, LANES), F32),
                            pltpu.SemaphoreType.DMA((2, 2))],
        ),
        out_shape=jax.ShapeDtypeStruct((n_rows, d), F32),
        compiler_params=_cparams(("arbitrary",)),
        name="moe_combine",
    )(dest1, dest2, x, mod3, route, fg, ys)


def _moe_ffn(x, mod3, g, rw_pad, w1, w3, w2, layer, seg_len, final_g=None):
    t, d = x.shape
    h, route, counts = _route(x, mod3, g, rw_pad, seg_len)
    cnt = counts[0, :N_EXPERTS].astype(jnp.int32)
    padded = (cnt + MOE_TILE - 1) // MOE_TILE * MOE_TILE
    ends = jnp.cumsum(padded)
    offs = ends - padded
    starts = jnp.cumsum(cnt) - cnt
    e1 = route[:, ROUTE_E1].astype(jnp.int32)
    e2 = route[:, ROUTE_E2].astype(jnp.int32)
    r1 = route[:, ROUTE_R1].astype(jnp.int32)
    r2 = route[:, ROUTE_R2].astype(jnp.int32)
    pos1, pos2 = starts[e1] + r1, starts[e2] + r2
    dest1, dest2 = offs[e1] + r1, offs[e2] + r2
    n_tiles = (2 * t) // MOE_TILE + N_EXPERTS
    tile_start = jnp.arange(n_tiles, dtype=jnp.int32) * MOE_TILE
    tile_expert = jnp.minimum(jnp.sum((tile_start[:, None] >= ends[None, :]).astype(jnp.int32), axis=1), N_EXPERTS - 1)
    tile_cs = starts[tile_expert] + (tile_start - offs[tile_expert])
    n_valid = (ends[-1:] // MOE_TILE).astype(jnp.int32)
    xs = _gather_sorted(h, pos1, pos2, tile_cs, n_valid)
    u = _expert_up(xs, tile_expert, n_valid, w1, w3, layer)
    ys = _expert_down(u, tile_expert, n_valid, w2, layer)
    if final_g is None:
        return _combine(x, mod3, route, ys, dest1, dest2, seg_len)
    return (_combine(x, mod3, route, ys, dest1, dest2, seg_len, row0=0, n_rows=seg_len, final_g=final_g),
            _combine(x, mod3, route, ys, dest1, dest2, seg_len, row0=seg_len, n_rows=t - seg_len, final_g=final_g))


def _final_norm_kernel(x_ref, g_ref, o_ref):
    x = x_ref[...]
    ms = jnp.mean(x * x, axis=-1, keepdims=True)
    o_ref[...] = x * lax.rsqrt(ms + EPS) * g_ref[...]


def _final_norm(x, g, *, row0, n_rows):
    d = x.shape[1]
    tm = 512
    blk0 = row0 // tm
    return pl.pallas_call(
        _final_norm_kernel,
        grid=(n_rows // tm,),
        in_specs=[pl.BlockSpec((tm, d), lambda i: (blk0 + i, 0)), pl.BlockSpec((1, d), lambda i: (0, 0))],
        out_specs=pl.BlockSpec((tm, d), lambda i: (i, 0)),
        out_shape=jax.ShapeDtypeStruct((n_rows, d), F32),
        compiler_params=_cparams(("parallel",)),
        name="final_norm",
    )(x, g.reshape(1, d))


def _rope_tables(n_tokens):
    pos = jnp.arange(n_tokens, dtype=jnp.int32)
    rowcol = jnp.stack([pos // GRID_W, pos % GRID_W], axis=-1).astype(F32)
    inv_freq = 1.0 / (ROPE_THETA ** (jnp.arange(ROPE_FREQS, dtype=F32) / ROPE_FREQS))
    ang = rowcol[:, :, None] * inv_freq
    cos = jnp.cos(ang)
    sin = jnp.sin(ang)
    cos_h = jnp.stack([cos, cos], axis=2).reshape(n_tokens, HEAD_DIM)
    sin_h = jnp.stack([-sin, sin], axis=2).reshape(n_tokens, HEAD_DIM)
    cos_t = jnp.tile(cos_h, (1, LANES // HEAD_DIM))
    sin_t = jnp.tile(sin_h, (1, LANES // HEAD_DIM))
    return (jnp.stack([jnp.ones_like(cos_t), cos_t]), jnp.stack([jnp.zeros_like(sin_t), sin_t]))


def _dft_matrices(n):
    n1 = 1 << (int(math.log2(n)) // 2)
    assert n % n1 == 0
    n2 = n // n1
    m = jnp.arange(n, dtype=jnp.int32)[None, :]
    ang_a = ((jnp.arange(n1, dtype=jnp.int32)[:, None] * m) % n1).astype(F32) * (2.0 * math.pi / n1)
    ang_b = ((jnp.arange(n2, dtype=jnp.int32)[:, None] * m) % n).astype(F32) * (2.0 * math.pi / n)
    ca, sa = jnp.cos(ang_a)[:, None, :], jnp.sin(ang_a)[:, None, :]
    cb, sb = jnp.cos(ang_b)[None, :, :], jnp.sin(ang_b)[None, :, :]
    scale = n ** -0.5
    cos_m = ((ca * cb - sa * sb) * scale).reshape(n, n)
    nsin_m = ((sa * cb + ca * sb) * -scale).reshape(n, n)
    return cos_m.astype(BF16), nsin_m.astype(BF16)


def _channel_dft_weight():
    n = FOURIER_GROUP_DIM
    idx = jnp.arange(n, dtype=jnp.int32)
    ang = ((idx[:, None] * idx[None, :]) % n).astype(F32) * (2.0 * math.pi / n)
    eye = jnp.eye(FOURIER_GROUPS, dtype=F32)
    cos_bd = jnp.kron(eye, jnp.cos(ang) * n ** -0.5)
    sin_bd = jnp.kron(eye, jnp.sin(ang) * n ** -0.5)
    return jnp.concatenate([cos_bd, sin_bd], axis=1).astype(BF16)


def _repack_w_in(w_in, lay):
    depth, d, _ = w_in.shape
    widths = (ATT_WIDTH, KV_WIDTH, KV_WIDTH, GLA_K_WIDTH, GLA_K_WIDTH, GLA_V_WIDTH, GLA_V_WIDTH,
              GLA_LOW_RANK, GLA_LOW_RANK, FOURIER_WIDTH, N_BRANCHES * d)
    offs = [0]
    for w in widths:
        offs.append(offs[-1] + w)
    aq, ak, av, gq, gk, gv, gr, lrf, lrb, fu, gates = [w_in[:, :, offs[i]:offs[i + 1]] for i in range(len(widths))]
    used = lay["lr"] + 2 * GLA_LOW_RANK
    pad = jnp.zeros((depth, d, lay["total"] - used), w_in.dtype)
    return jnp.concatenate([aq, gv, gr, fu, gates, gq, gk, ak, av, lrf, lrb, pad], axis=-1).astype(BF16)


def kernel(x_prompt, x_sample, cache_k, cache_v, state_gla, c, c_ctx, ada_w, ada_b, norm1_g, norm2_g, w_in, q_norm_g, k_norm_g, alpha_up, alpha_b, gla_norm_g, w_branch, w_out, ffn_w1, ffn_w3, ffn_w2, router_w, moe_w1, moe_w3, moe_w2, final_g):
    batch, seq, d = x_prompt.shape
    dec_batch, dec_seq, _ = x_sample.shape
    depth = w_in.shape[0]
    past = cache_k.shape[2]
    seg_len = batch * seq
    assert seg_len == dec_seq, "context tokens and each latent request must fill equal segments"
    assert seq % GLA_CHUNK == 0 and dec_seq % GLA_CHUNK == 0
    assert COL_GATES % d == 0
    assert d == TILE_ROWS * LANES, "MoE row DMAs move one (8, 128) f32 tile per token"
    n_seg = 1 + dec_batch
    t = n_seg * seg_len
    lay = _in_layout(d)

    w_in_p = _repack_w_in(w_in, lay)
    w_branch_b = w_branch.astype(BF16)
    w_out_b = w_out.astype(BF16)
    ffn_w1_b, ffn_w3_b, ffn_w2_b = ffn_w1.astype(BF16), ffn_w3.astype(BF16), ffn_w2.astype(BF16)
    router_pad = jnp.pad(router_w, ((0, 0), (0, 0), (0, LANES - N_EXPERTS)))
    cos_tab, sin_tab = _rope_tables(dec_seq)
    dft_cos_s, dft_nsin_s = _dft_matrices(dec_seq)
    dft_cos_p, dft_nsin_p = _dft_matrices(seq)
    w_chan = _channel_dft_weight()
    ones_bd = jnp.kron(jnp.eye(ATT_HEADS, dtype=F32), jnp.ones((HEAD_DIM, HEAD_DIM), F32)).astype(BF16)
    q_g = jnp.tile(q_norm_g, (1, ATT_HEADS)).reshape(depth, 1, ATT_WIDTH)
    k_g = jnp.tile(k_norm_g, (1, ATT_KV_HEADS)).reshape(depth, 1, KV_WIDTH)
    aup_pad = jnp.zeros((depth, 2, LANES, GLA_K_WIDTH), F32)
    aup_pad = aup_pad.at[:, 0, :GLA_LOW_RANK].set(alpha_up[:, 0])
    aup_pad = aup_pad.at[:, 1, GLA_LOW_RANK:2 * GLA_LOW_RANK].set(alpha_up[:, 1])
    ab = alpha_b.reshape(depth, 2, 1, GLA_K_WIDTH)
    s0_lat = state_gla.transpose(0, 1, 2, 5, 3, 4).reshape(dec_batch, depth, 2, GLA_DV, GLA_K_WIDTH)
    s0_all = jnp.concatenate([jnp.zeros((1,) + s0_lat.shape[1:], F32), s0_lat], axis=0)

    def ctx_keys(x):
        xb = x.astype(BF16)[:, :, :, None, :]
        return jnp.broadcast_to(xb, (dec_batch, past, ATT_KV_HEADS, 2, HEAD_DIM)).reshape(dec_batch, past, 2 * KV_WIDTH)

    def ctx_values_t(x):
        xt = x.astype(BF16).transpose(0, 2, 3, 1)
        return jnp.concatenate([xt, jnp.ones_like(xt)], axis=2).reshape(dec_batch, ATT_KV_HEADS * VT_ROWS, past)

    cond = jnp.zeros((8, d), F32).at[0].set(c_ctx).at[1:1 + dec_batch].set(c)
    mod_all = _modulation(cond, ada_w, ada_b)

    x = jnp.concatenate([x_prompt.reshape(seg_len, d), x_sample.reshape(dec_batch * dec_seq, d)], axis=0)
    new_k, new_v, new_s = [], [], []
    for l in range(depth):
        mod3 = mod_all[l, :n_seg].reshape(n_seg, 1, 6 * d)
        z = _in_projection(x, mod3, norm1_g[l], w_in_p, l, seg_len)
        qn, k_out, k2, vt = _attention_prep(z, lay, cos_tab, sin_tab, q_g[l], k_g[l], ones_bd, seg_len)
        att_p = _attention(qn, k2, vt, None, None, row0=0, n_seq=batch, seq_len=seq, tq=min(ATT_QUERY_TILE, seq))
        att_s = _attention(qn, k2, vt, ctx_keys(cache_k[:, l]), ctx_values_t(cache_v[:, l]),
                           row0=seg_len, n_seq=dec_batch, seq_len=dec_seq, tq=min(ATT_QUERY_TILE, dec_seq))
        gla_o, s_fin = _gla(z, lay, aup_pad[l], ab[l], s0_all[:, l], seg_len=seg_len, n_seg=n_seg, prompt_len=seq)
        xcs = _fourier_channels(z, w_chan)
        four_p = _fourier_positions(xcs, dft_cos_p, dft_nsin_p, row0=0, n_seq=batch, seq_len=seq)
        four_s = _fourier_positions(xcs, dft_cos_s, dft_nsin_s, row0=seg_len, n_seq=dec_batch, seq_len=dec_seq)
        x = _merge(att_p, att_s, gla_o, z, four_p, four_s, x, mod3, gla_norm_g[l], w_branch_b, w_out_b, l, seg_len)
        j = l // 2
        if l % 2 == 0:
            x = _dense_ffn(x, mod3, norm2_g[l], ffn_w1_b, ffn_w3_b, ffn_w2_b, j, seg_len)
        elif l + 1 < depth:
            x = _moe_ffn(x, mod3, norm2_g[l], router_pad[j], moe_w1, moe_w3, moe_w2, j, seg_len)
        else:
            y_p, y_s = _moe_ffn(x, mod3, norm2_g[l], router_pad[j], moe_w1, moe_w3, moe_w2, j, seg_len, final_g=final_g)
        new_k.append(k_out[:seg_len].reshape(batch, seq, ATT_KV_HEADS, HEAD_DIM))
        v_col = lay["akv"] + KV_WIDTH
        new_v.append(z[:seg_len, v_col:v_col + KV_WIDTH].astype(F32).reshape(batch, seq, ATT_KV_HEADS, HEAD_DIM))
        cps = seq // GLA_CHUNK
        sf = s_fin[0, 0, cps - 1::cps]
        sb = s_fin[0, 1, 0::cps]
        st = jnp.stack([sf, sb], axis=1).reshape(batch, 2, GLA_DV, GLA_HEADS, GLA_DK)
        new_s.append(st.transpose(0, 1, 3, 4, 2))
    if depth % 2 == 1:
        y_p = _final_norm(x, final_g, row0=0, n_rows=seg_len)
        y_s = _final_norm(x, final_g, row0=seg_len, n_rows=t - seg_len)
    y_prompt = y_p.reshape(batch, seq, d)
    y_sample = y_s.reshape(dec_batch, dec_seq, d)
    return (y_prompt, y_sample, jnp.stack(new_k, axis=1), jnp.stack(new_v, axis=1), jnp.stack(new_s, axis=1))
```

```python
import functools
import math

import jax
import jax.numpy as jnp
from jax import lax
from jax.experimental import pallas as pl
from jax.experimental.pallas import tpu as pltpu

F32 = jnp.float32
BF16 = jnp.bfloat16

EPS = 1e-6
ATT_HEADS = 8
ATT_KV_HEADS = 2
HEAD_DIM = 64
ATT_WIDTH = ATT_HEADS * HEAD_DIM
KV_WIDTH = ATT_KV_HEADS * HEAD_DIM
ROPE_FREQS = HEAD_DIM // 4
ROPE_THETA = 10000.0
GRID_W = 64
GLA_HEADS = 4
GLA_DK = 64
GLA_DV = 128
GLA_K_WIDTH = GLA_HEADS * GLA_DK
GLA_V_WIDTH = GLA_HEADS * GLA_DV
GLA_LOW_RANK = 16
GLA_TAU = 16.0
FOURIER_GROUPS = 4
FOURIER_GROUP_DIM = 128
FOURIER_WIDTH = FOURIER_GROUPS * FOURIER_GROUP_DIM
N_BRANCHES = 3
N_EXPERTS = 8

LANES = 128
MXU_WIDTH = 256
VMEM_LIMIT_BYTES = 56 * 1024 * 1024

COL_AQ = 0
COL_GV = 512
COL_GR = 1024
COL_FU = 1536
COL_GATES = 2048

GLA_CHUNK = 256
GLA_SAFE_LOG = 60.0


def _cparams(sem):
    return pltpu.CompilerParams(dimension_semantics=sem, vmem_limit_bytes=VMEM_LIMIT_BYTES)


def _resident(shape, index_map):
    return pl.BlockSpec(shape, index_map, pipeline_mode=pl.Buffered(1))


def _split2(x):
    hi = x.astype(BF16)
    lo = (x - hi.astype(F32)).astype(BF16)
    return hi, lo


def _split3(x):
    hi = x.astype(BF16)
    r = x - hi.astype(F32)
    mid = r.astype(BF16)
    lo = (r - mid.astype(F32)).astype(BF16)
    return hi, mid, lo


def _dot(a, b):
    return jnp.dot(a, b, preferred_element_type=F32)


def _dot_nt(a, b):
    return lax.dot_general(a, b, (((1,), (1,)), ((), ())), preferred_element_type=F32)


def _dot_tn(a, b):
    return lax.dot_general(a, b, (((0,), (0,)), ((), ())), preferred_element_type=F32)


def _dot3(a, b):
    a_hi, a_lo = _split2(a)
    b_hi, b_lo = _split2(b)
    return _dot(a_hi, b_hi) + _dot(a_lo, b_hi) + _dot(a_hi, b_lo)


def _sigmoid(x):
    return 1.0 / (1.0 + jnp.exp(-x))


def _silu(x):
    return x * _sigmoid(x)


def _in_layout(d_model):
    gates_end = COL_GATES + N_BRANCHES * d_model
    col_gq = gates_end
    col_gk = col_gq + GLA_K_WIDTH
    col_akv = col_gk + GLA_K_WIDTH
    col_lr = col_akv + 2 * KV_WIDTH
    used = col_lr + LANES
    total = -(-used // 1024) * 1024
    return dict(gq=col_gq, gk=col_gk, akv=col_akv, lr=col_lr, total=total)


def _mod_kernel(cond_ref, w_ref, b_ref, o_ref):
    c = _silu(cond_ref[...])
    o_ref[0] = _dot3(c, w_ref[0]) + b_ref[0]


def _modulation(cond, ada_w, ada_b):
    depth, d, n6 = ada_w.shape
    tn = 1536 if n6 % 1536 == 0 else n6
    rows = cond.shape[0]
    return pl.pallas_call(
        _mod_kernel,
        grid=(depth, n6 // tn),
        in_specs=[
            pl.BlockSpec((rows, d), lambda l, j: (0, 0)),
            pl.BlockSpec((1, d, tn), lambda l, j: (l, 0, j)),
            pl.BlockSpec((1, 1, tn), lambda l, j: (l, 0, j)),
        ],
        out_specs=pl.BlockSpec((1, rows, tn), lambda l, j: (l, 0, j)),
        out_shape=jax.ShapeDtypeStruct((depth, rows, n6), F32),
        compiler_params=_cparams(("parallel", "parallel")),
        name="adaln_mod",
    )(cond, ada_w, ada_b.reshape(depth, 1, n6))


def _norm_mod(x, g, shift, scale):
    ms = jnp.mean(x * x, axis=-1, keepdims=True)
    return x * lax.rsqrt(ms + EPS) * g * (1.0 + scale) + shift


def _inproj_kernel(x_ref, mod_ref, g_ref, w_ref, z_ref, h_scr, *, d):
    @pl.when(pl.program_id(1) == 0)
    def _():
        h = _norm_mod(x_ref[...], g_ref[...], mod_ref[0, :, 0:d], mod_ref[0, :, d:2 * d])
        h_scr[...] = h.astype(BF16)

    z_ref[...] = _dot(h_scr[...], w_ref[...]).astype(z_ref.dtype)


def _in_projection(x, mod3, g, w, layer, seg_len):
    t, d = x.shape
    n = w.shape[2]
    tm = min(1024, seg_len)
    tn = 3072
    tps = seg_len // tm
    return pl.pallas_call(
        functools.partial(_inproj_kernel, d=d),
        grid=(t // tm, n // tn),
        in_specs=[
            pl.BlockSpec((tm, d), lambda i, j: (i, 0)),
            pl.BlockSpec((1, 1, 6 * d), lambda i, j: (i // tps, 0, 0)),
            pl.BlockSpec((1, d), lambda i, j: (0, 0)),
            pl.BlockSpec((None, d, tn), lambda i, j: (layer, 0, j)),
        ],
        out_specs=pl.BlockSpec((tm, tn), lambda i, j: (i, j)),
        out_shape=jax.ShapeDtypeStruct((t, n), BF16),
        scratch_shapes=[pltpu.VMEM((tm, d), BF16)],
        compiler_params=_cparams(("parallel", "arbitrary")),
        name="in_projection",
    )(x, mod3, g.reshape(1, d), w)


def _head_rms(x, g, ones_bd):
    hi, lo = _split2(x * x)
    ss = _dot(hi, ones_bd) + _dot(lo, ones_bd)
    return x * lax.rsqrt(ss * (1.0 / HEAD_DIM) + EPS) * g


def _rope(x, cos, sin_signed):
    width = x.shape[1]
    reps = width // LANES
    if reps > 1:
        cos = jnp.concatenate([cos] * reps, axis=1)
        sin_signed = jnp.concatenate([sin_signed] * reps, axis=1)
    lane = lax.broadcasted_iota(jnp.int32, x.shape, 1)
    first_half = (lane % (2 * ROPE_FREQS)) < ROPE_FREQS
    partner = jnp.where(first_half, pltpu.roll(x, width - ROPE_FREQS, 1), pltpu.roll(x, ROPE_FREQS, 1))
    return x * cos + partner * sin_signed


ATT_KEY_CHUNK = 512
ATT_QUERY_TILE = 256
VT_ROWS = 2 * HEAD_DIM
VT_USED = HEAD_DIM + 16


def _dup_heads(x):
    lane = lax.broadcasted_iota(jnp.int32, x.shape, 1)
    swapped = pltpu.roll(x, HEAD_DIM, 1)
    low = lane < HEAD_DIM
    return jnp.concatenate([jnp.where(low, x, swapped), jnp.where(low, swapped, x)], axis=1)


def _attn_prep_kernel(zq_ref, zkv_ref, cos_ref, sin_ref, qg_ref, kg_ref, bd_ref, q_ref, kout_ref, k2_ref, vt_ref):
    cos = cos_ref[0]
    sin = sin_ref[0]
    bd = bd_ref[...]
    q = _head_rms(zq_ref[...].astype(F32), qg_ref[...], bd)
    q_ref[...] = (_rope(q, cos, sin) * (HEAD_DIM ** -0.5)).astype(BF16)
    kv = zkv_ref[...].astype(F32)
    k = _head_rms(kv[:, :KV_WIDTH], kg_ref[...], bd[:KV_WIDTH, :KV_WIDTH])
    kout_ref[...] = k
    k2_ref[...] = _dup_heads(_rope(k, cos, sin)).astype(BF16)
    vt = kv[:, KV_WIDTH:].T
    ones = jnp.ones((HEAD_DIM, vt.shape[1]), F32)
    vt_ref[...] = jnp.concatenate([vt[:HEAD_DIM], ones, vt[HEAD_DIM:], ones], axis=0).astype(BF16)


def _attention_prep(z, lay, cos_tab, sin_tab, q_g, k_g, ones_bd, seg_len):
    t = z.shape[0]
    tm = min(512, seg_len)
    tps = seg_len // tm
    akv_blk = lay["akv"] // (2 * KV_WIDTH)
    tab_spec = pl.BlockSpec((1, tm, LANES), lambda i: (jnp.minimum(i // tps, 1), i % tps, 0))
    return pl.pallas_call(
        _attn_prep_kernel,
        grid=(t // tm,),
        in_specs=[
            pl.BlockSpec((tm, ATT_WIDTH), lambda i: (i, COL_AQ // ATT_WIDTH)),
            pl.BlockSpec((tm, 2 * KV_WIDTH), lambda i: (i, akv_blk)),
            tab_spec,
            tab_spec,
            pl.BlockSpec((1, ATT_WIDTH), lambda i: (0, 0)),
            pl.BlockSpec((1, KV_WIDTH), lambda i: (0, 0)),
            pl.BlockSpec((ATT_WIDTH, ATT_WIDTH), lambda i: (0, 0)),
        ],
        out_specs=[
            pl.BlockSpec((tm, ATT_WIDTH), lambda i: (i, 0)),
            pl.BlockSpec((tm, KV_WIDTH), lambda i: (i, 0)),
            pl.BlockSpec((tm, 2 * KV_WIDTH), lambda i: (i, 0)),
            pl.BlockSpec((ATT_KV_HEADS * VT_ROWS, tm), lambda i: (0, i)),
        ],
        out_shape=[
            jax.ShapeDtypeStruct((t, ATT_WIDTH), BF16),
            jax.ShapeDtypeStruct((t, KV_WIDTH), F32),
            jax.ShapeDtypeStruct((t, 2 * KV_WIDTH), BF16),
            jax.ShapeDtypeStruct((ATT_KV_HEADS * VT_ROWS, t), BF16),
        ],
        compiler_params=_cparams(("parallel",)),
        name="attention_prep",
    )(z, z, cos_tab, sin_tab, q_g, k_g, ones_bd)


def _attn_kernel(*refs, has_ctx):
    if has_ctx:
        q_ref, k_ref, v_ref, ck_ref, cv_ref, o_ref = refs
    else:
        q_ref, k_ref, v_ref, o_ref = refs
    tq = q_ref.shape[0]
    sources = [(k_ref, v_ref)] + ([(ck_ref.at[0], cv_ref.at[0])] if has_ctx else [])
    chunks = []
    for kr, vr in sources:
        n = min(ATT_KEY_CHUNK, kr.shape[0])
        chunks += [(kr, vr, c * n, n) for c in range(kr.shape[0] // n)]
    low = lax.broadcasted_iota(jnp.int32, (tq, LANES), 1) < HEAD_DIM
    zero = jnp.zeros((tq, LANES), BF16)
    n_heads = ATT_HEADS // ATT_KV_HEADS
    ws = []
    for h in range(n_heads):
        q2 = q_ref[:, (h // 2) * LANES:(h // 2 + 1) * LANES]
        ws.append(jnp.where(low, q2, zero) if h % 2 == 0 else jnp.where(low, zero, q2))
    w = jnp.concatenate(ws, axis=0)
    maxes, parts = [], []
    scores = lambda c: _dot_nt(c[0][c[2]:c[2] + c[3], :], w)
    st_next = scores(chunks[0])
    for idx, (kr, vr, r0, n) in enumerate(chunks):
        st = st_next
        if idx + 1 < len(chunks):
            st_next = scores(chunks[idx + 1])
        mc = jnp.max(st, axis=0, keepdims=True)
        p = jnp.exp(st - mc).astype(BF16)
        maxes.append(mc)
        parts.append(_dot(vr[0:VT_USED, r0:r0 + n], p))
    m = functools.reduce(jnp.maximum, maxes)
    acc = sum(jnp.exp(mc - m) * part for mc, part in zip(maxes, parts))
    o_t = acc[:HEAD_DIM] / acc[HEAD_DIM:HEAD_DIM + 1]
    o_heads = jnp.concatenate([o_t[:, h * tq:(h + 1) * tq] for h in range(n_heads)], axis=0)
    o_ref[...] = o_heads.T.astype(BF16)


def _attention(q, k2, v2, ctx_k2, ctx_v2, *, row0, n_seq, seq_len, tq):
    ntq = seq_len // tq
    gw = ATT_WIDTH // ATT_KV_HEADS
    qblk0 = row0 // tq
    kblk0 = row0 // seq_len
    has_ctx = ctx_k2 is not None
    in_specs = [
        pl.BlockSpec((tq, gw), lambda b, g, i: (qblk0 + b * ntq + i, g)),
        pl.BlockSpec((seq_len, LANES), lambda b, g, i: (kblk0 + b, g)),
        pl.BlockSpec((VT_ROWS, seq_len), lambda b, g, i: (g, kblk0 + b)),
    ]
    args = [q, k2, v2]
    if has_ctx:
        past = ctx_k2.shape[1]
        in_specs += [pl.BlockSpec((1, past, LANES), lambda b, g, i: (b, 0, g)),
                     pl.BlockSpec((1, VT_ROWS, past), lambda b, g, i: (b, g, 0))]
        args += [ctx_k2, ctx_v2]
    return pl.pallas_call(
        functools.partial(_attn_kernel, has_ctx=has_ctx),
        grid=(n_seq, ATT_KV_HEADS, ntq),
        in_specs=in_specs,
        out_specs=pl.BlockSpec((tq, gw), lambda b, g, i: (b * ntq + i, g)),
        out_shape=jax.ShapeDtypeStruct((n_seq * seq_len, ATT_WIDTH), BF16),
        compiler_params=_cparams(("parallel", "parallel", "parallel")),
        name="attention_ctx" if has_ctx else "attention_self",
    )(*args)


def _gla_prepare(lr_ref, aup, ab, fwd):
    c = lr_ref.shape[0]
    aup_hi, aup_lo = _split2(aup)
    lr = lr_ref[...]
    x = _dot(lr, aup_hi) + _dot(lr, aup_lo) + ab
    la = (jnp.minimum(x, 0.0) - jnp.log(1.0 + jnp.exp(-jnp.abs(x)))) * (1.0 / GLA_TAU)
    row = lax.broadcasted_iota(jnp.int32, (c, c), 0)
    col = lax.broadcasted_iota(jnp.int32, (c, c), 1)
    tri = (row >= col) if fwd else (row <= col)
    tri_b = jnp.where(tri, 1.0, 0.0).astype(BF16)
    la_hi, la_mid, la_lo = _split3(la)
    cum = _dot(tri_b, la_hi) + _dot(tri_b, la_mid) + _dot(tri_b, la_lo)
    bend = jnp.sum(la, axis=0, keepdims=True)
    return la, cum, bend, tri, jnp.min(cum) >= -GLA_SAFE_LOG


def _gla_fast(q_ref, k_ref, v_ref, o_ref, sfin_ref, s_ref, cum, bend, tri):
    head_of_lane = lax.broadcasted_iota(jnp.int32, (1, GLA_K_WIDTH), 1) // GLA_DK
    q = q_ref[...].astype(F32) * (GLA_DK ** -0.5)
    k = k_ref[...].astype(F32)
    vb = v_ref[...]
    qt = (q * jnp.exp(cum)).astype(BF16)
    kt = (k * jnp.exp(-cum)).astype(BF16)
    ke = (k * jnp.exp(bend - cum)).astype(BF16)
    s_old = s_ref[...]
    s_b = s_old.astype(BF16)
    s_new = jnp.exp(bend) * s_old
    zero = jnp.zeros_like(qt)
    for h in range(GLA_HEADS):
        hm = head_of_lane == h
        qh = jnp.where(hm, qt, zero)
        a = jnp.where(tri, _dot_nt(qh, kt), 0.0).astype(BF16)
        vh = vb[:, h * GLA_DV:(h + 1) * GLA_DV]
        o_ref[:, h * GLA_DV:(h + 1) * GLA_DV] = _dot(a, vh) + _dot_nt(qh, s_b)
        s_new = s_new + _dot_tn(vh, jnp.where(hm, ke, zero))
    s_ref[...] = s_new
    sfin_ref[0, 0] = s_new


def _gla_slow(q_ref, k_ref, v_ref, o_ref, sfin_ref, s_ref, la, fwd, la_scr, qf_scr, kf_scr, vt_scr, ot_scr):
    c = q_ref.shape[0]
    head_of_lane = lax.broadcasted_iota(jnp.int32, (1, GLA_K_WIDTH), 1) // GLA_DK
    la_scr[...] = la
    qf_scr[...] = q_ref[...].astype(F32) * (GLA_DK ** -0.5)
    kf_scr[...] = k_ref[...].astype(F32)
    vt_scr[...] = v_ref[...].astype(F32).T
    ot_scr[...] = jnp.zeros_like(ot_scr)
    lane_c = lax.broadcasted_iota(jnp.int32, (1, c), 1)

    def body(i, s):
        t = i if fwd else c - 1 - i
        a_t = jnp.exp(la_scr[pl.ds(t, 1), :])
        k_t = kf_scr[pl.ds(t, 1), :]
        q_t = qf_scr[pl.ds(t, 1), :]
        onehot = lane_c == t
        vmat = jnp.zeros((GLA_DV, GLA_K_WIDTH), F32)
        for h in range(GLA_HEADS):
            vcol = jnp.sum(jnp.where(onehot, vt_scr[h * GLA_DV:(h + 1) * GLA_DV, :], 0.0), axis=1, keepdims=True)
            vmat = jnp.where(head_of_lane == h, vcol, vmat)
        s = a_t * s + vmat * k_t
        prod = s * q_t
        for h in range(GLA_HEADS):
            ocol = jnp.sum(jnp.where(head_of_lane == h, prod, 0.0), axis=1, keepdims=True)
            rows = slice(h * GLA_DV, (h + 1) * GLA_DV)
            ot_scr[rows, :] = jnp.where(onehot, ocol, ot_scr[rows, :])
        return s

    s_new = lax.fori_loop(0, c, body, s_ref[...])
    o_ref[...] = ot_scr[...].T
    s_ref[...] = s_new
    sfin_ref[0, 0] = s_new


def _gla_kernel(qf_ref, kf_ref, vf_ref, lrf_ref, qb_ref, kb_ref, vb_ref, lrb_ref, aup_ref, ab_ref, s0_ref,
                of_ref, ob_ref, sff_ref, sfb_ref, s_scr, *slow_scr, chunks_per_seq):
    @pl.when(pl.program_id(1) % chunks_per_seq == 0)
    def _():
        s_scr[...] = s0_ref[0]

    dirs = ((True, qf_ref, kf_ref, vf_ref, lrf_ref, of_ref, sff_ref), (False, qb_ref, kb_ref, vb_ref, lrb_ref, ob_ref, sfb_ref))
    prep = [_gla_prepare(d[4], aup_ref[i], ab_ref[i], d[0]) for i, d in enumerate(dirs)]
    both_safe = prep[0][4] & prep[1][4]

    def fast(i):
        _, q_ref, k_ref, v_ref, _, o_ref, sfin_ref = dirs[i]
        _, cum, bend, tri, _ = prep[i]
        _gla_fast(q_ref, k_ref, v_ref, o_ref, sfin_ref, s_scr.at[i], cum, bend, tri)

    @pl.when(both_safe)
    def _():
        fast(0)
        fast(1)

    @pl.when(jnp.logical_not(both_safe))
    def _():
        for i, (fwd, q_ref, k_ref, v_ref, _, o_ref, sfin_ref) in enumerate(dirs):
            pl.when(prep[i][4])(functools.partial(fast, i))
            slow = functools.partial(_gla_slow, q_ref, k_ref, v_ref, o_ref, sfin_ref, s_scr.at[i], prep[i][0], fwd, *slow_scr)
            pl.when(jnp.logical_not(prep[i][4]))(slow)


def _gla(z, lay, aup_pad, ab, s0, *, seg_len, n_seg, prompt_len):
    t = z.shape[0]
    c = GLA_CHUNK
    nc = seg_len // c
    cps_prompt = prompt_len // c
    gq_blk = lay["gq"] // GLA_K_WIDTH
    gk_blk = lay["gk"] // GLA_K_WIDTH
    gv_blk = COL_GV // GLA_V_WIDTH
    lr_blk = lay["lr"] // LANES

    fwd_chunk = lambda s, n: s * nc + n
    bwd_chunk = lambda s, n: s * nc + nc - 1 - n

    def kernel(*refs):
        cps = jnp.where(pl.program_id(0) == 0, cps_prompt, nc)
        _gla_kernel(*refs, chunks_per_seq=cps)

    def operand_specs(chunk):
        return [
            pl.BlockSpec((c, GLA_K_WIDTH), lambda s, n: (chunk(s, n), gq_blk)),
            pl.BlockSpec((c, GLA_K_WIDTH), lambda s, n: (chunk(s, n), gk_blk)),
            pl.BlockSpec((c, GLA_V_WIDTH), lambda s, n: (chunk(s, n), gv_blk)),
            pl.BlockSpec((c, LANES), lambda s, n: (chunk(s, n), lr_blk)),
        ]

    return pl.pallas_call(
        kernel,
        grid=(n_seg, nc),
        in_specs=operand_specs(fwd_chunk) + operand_specs(bwd_chunk) + [
            pl.BlockSpec((2, LANES, GLA_K_WIDTH), lambda s, n: (0, 0, 0)),
            pl.BlockSpec((2, 1, GLA_K_WIDTH), lambda s, n: (0, 0, 0)),
            pl.BlockSpec((1, 2, GLA_DV, GLA_K_WIDTH), lambda s, n: (s, 0, 0, 0)),
        ],
        out_specs=[
            pl.BlockSpec((c, GLA_V_WIDTH), lambda s, n: (fwd_chunk(s, n), 0)),
            pl.BlockSpec((c, GLA_V_WIDTH), lambda s, n: (bwd_chunk(s, n), 0)),
            pl.BlockSpec((1, 1, GLA_DV, GLA_K_WIDTH), lambda s, n: (s, n, 0, 0)),
            pl.BlockSpec((1, 1, GLA_DV, GLA_K_WIDTH), lambda s, n: (s, nc - 1 - n, 0, 0)),
        ],
        out_shape=[
            jax.ShapeDtypeStruct((t, GLA_V_WIDTH), F32),
            jax.ShapeDtypeStruct((t, GLA_V_WIDTH), F32),
            jax.ShapeDtypeStruct((n_seg, nc, GLA_DV, GLA_K_WIDTH), F32),
            jax.ShapeDtypeStruct((n_seg, nc, GLA_DV, GLA_K_WIDTH), F32),
        ],
        scratch_shapes=[
            pltpu.VMEM((2, GLA_DV, GLA_K_WIDTH), F32),
            pltpu.VMEM((c, GLA_K_WIDTH), F32),
            pltpu.VMEM((c, GLA_K_WIDTH), F32),
            pltpu.VMEM((c, GLA_K_WIDTH), F32),
            pltpu.VMEM((GLA_V_WIDTH, c), F32),
            pltpu.VMEM((GLA_V_WIDTH, c), F32),
        ],
        compiler_params=_cparams(("parallel", "arbitrary")),
        name="gla_scan",
    )(z, z, z, z, z, z, z, z, aup_pad, ab, s0)


def _four_chan_kernel(u_ref, w_ref, o_ref):
    o_ref[...] = _dot(u_ref[...], w_ref[...]).astype(BF16)


def _fourier_channels(z, w_chan):
    t = z.shape[0]
    tm = 512
    return pl.pallas_call(
        _four_chan_kernel,
        grid=(t // tm,),
        in_specs=[
            pl.BlockSpec((tm, FOURIER_WIDTH), lambda i: (i, COL_FU // FOURIER_WIDTH)),
            _resident((FOURIER_WIDTH, 2 * FOURIER_WIDTH), lambda i: (0, 0)),
        ],
        out_specs=pl.BlockSpec((tm, 2 * FOURIER_WIDTH), lambda i: (i, 0)),
        out_shape=jax.ShapeDtypeStruct((t, 2 * FOURIER_WIDTH), BF16),
        compiler_params=_cparams(("parallel",)),
        name="fourier_channels",
    )(z, w_chan)


def _four_pos_kernel(c_ref, s_ref, xc_ref, xs_ref, o_ref):
    o_ref[...] = (_dot(c_ref[...], xc_ref[...]) + _dot(s_ref[...], xs_ref[...])).astype(BF16)


def _fourier_positions(xcs, cos_m, nsin_m, *, row0, n_seq, seq_len):
    tm = min(512, seq_len)
    nt = seq_len // tm
    blk0 = row0 // seq_len
    return pl.pallas_call(
        _four_pos_kernel,
        grid=(n_seq, nt),
        in_specs=[
            pl.BlockSpec((tm, seq_len), lambda b, i: (i, 0)),
            pl.BlockSpec((tm, seq_len), lambda b, i: (i, 0)),
            pl.BlockSpec((seq_len, FOURIER_WIDTH), lambda b, i: (blk0 + b, 0)),
            pl.BlockSpec((seq_len, FOURIER_WIDTH), lambda b, i: (blk0 + b, 1)),
        ],
        out_specs=pl.BlockSpec((tm, FOURIER_WIDTH), lambda b, i: (b * nt + i, 0)),
        out_shape=jax.ShapeDtypeStruct((n_seq * seq_len, FOURIER_WIDTH), BF16),
        compiler_params=_cparams(("parallel", "parallel")),
        name="fourier_positions",
    )(cos_m, nsin_m, xcs, xcs)


def _merge_kernel(attp_ref, atts_ref, gof_ref, gob_ref, gr_ref, fourp_ref, fours_ref, g0_ref, g1_ref, g2_ref, x_ref,
                  mod_ref, gg_ref, wb_ref, wo_ref, o_ref, *, d, prompt_tiles):
    in_prompt = pl.program_id(0) < prompt_tiles
    att = jnp.where(in_prompt, attp_ref[...], atts_ref[...])
    four = jnp.where(in_prompt, fourp_ref[...], fours_ref[...])
    o = gof_ref[...] + gob_ref[...]
    gg = gg_ref[...]
    parts = []
    for h in range(GLA_HEADS):
        oh = o[:, h * GLA_DV:(h + 1) * GLA_DV]
        ms = jnp.mean(oh * oh, axis=-1, keepdims=True)
        parts.append(oh * lax.rsqrt(ms + EPS) * gg)
    gla = (jnp.concatenate(parts, axis=1) * _silu(gr_ref[...].astype(F32))).astype(BF16)
    merged = (_sigmoid(g0_ref[...].astype(F32)) * _dot(att, wb_ref[0])
              + _sigmoid(g1_ref[...].astype(F32)) * _dot(gla, wb_ref[1])
              + _sigmoid(g2_ref[...].astype(F32)) * _dot(four, wb_ref[2]))
    mix = _dot(merged.astype(BF16), wo_ref[...])
    o_ref[...] = x_ref[...] + mod_ref[0, :, 2 * d:3 * d] * mix


def _merge(att_p, att_s, gla_f, gla_b, z, four_p, four_s, x, mod3, gla_g, w_branch, w_out, layer, seg_len):
    t, d = x.shape
    tm = min(512, seg_len)
    tps = seg_len // tm
    bw = att_p.shape[1]
    gate_blk = COL_GATES // d
    prompt_tiles = att_p.shape[0] // tm
    p_spec = pl.BlockSpec((tm, bw), lambda i: (jnp.minimum(i, prompt_tiles - 1), 0))
    s_spec = pl.BlockSpec((tm, bw), lambda i: (jnp.maximum(i - prompt_tiles, 0), 0))
    return pl.pallas_call(
        functools.partial(_merge_kernel, d=d, prompt_tiles=prompt_tiles),
        grid=(t // tm,),
        in_specs=[
            p_spec,
            s_spec,
            pl.BlockSpec((tm, GLA_V_WIDTH), lambda i: (i, 0)),
            pl.BlockSpec((tm, GLA_V_WIDTH), lambda i: (i, 0)),
            pl.BlockSpec((tm, GLA_V_WIDTH), lambda i: (i, COL_GR // GLA_V_WIDTH)),
            p_spec,
            s_spec,
            pl.BlockSpec((tm, d), lambda i: (i, gate_blk)),
            pl.BlockSpec((tm, d), lambda i: (i, gate_blk + 1)),
            pl.BlockSpec((tm, d), lambda i: (i, gate_blk + 2)),
            pl.BlockSpec((tm, d), lambda i: (i, 0)),
            pl.BlockSpec((1, 1, 6 * d), lambda i: (i // tps, 0, 0)),
            pl.BlockSpec((1, GLA_DV), lambda i: (0, 0)),
            _resident((None, N_BRANCHES, bw, d), lambda i: (layer, 0, 0, 0)),
            _resident((None, d, d), lambda i: (layer, 0, 0)),
        ],
        out_specs=pl.BlockSpec((tm, d), lambda i: (i, 0)),
        out_shape=jax.ShapeDtypeStruct((t, d), F32),
        compiler_params=_cparams(("parallel",)),
        name="branch_merge",
    )(att_p, att_s, gla_f, gla_b, z, four_p, four_s, z, z, z, x, mod3, gla_g.reshape(1, GLA_DV), w_branch, w_out)


def _ffn_kernel(x_ref, mod_ref, g_ref, w1_ref, w3_ref, w2_ref, o_ref, *, d):
    x = x_ref[...]
    h = _norm_mod(x, g_ref[...], mod_ref[0, :, 3 * d:4 * d], mod_ref[0, :, 4 * d:5 * d]).astype(BF16)
    a = _dot(h, w1_ref[...])
    u = (_silu(a) * _dot(h, w3_ref[...])).astype(BF16)
    o_ref[...] = x + mod_ref[0, :, 5 * d:6 * d] * _dot(u, w2_ref[...])


def _dense_ffn(x, mod3, g, w1, w3, w2, layer, seg_len):
    t, d = x.shape
    dff = w1.shape[2]
    tm = min(512, seg_len)
    tps = seg_len // tm
    return pl.pallas_call(
        functools.partial(_ffn_kernel, d=d),
        grid=(t // tm,),
        in_specs=[
            pl.BlockSpec((tm, d), lambda i: (i, 0)),
            pl.BlockSpec((1, 1, 6 * d), lambda i: (i // tps, 0, 0)),
            pl.BlockSpec((1, d), lambda i: (0, 0)),
            _resident((None, d, dff), lambda i: (layer, 0, 0)),
            _resident((None, d, dff), lambda i: (layer, 0, 0)),
            _resident((None, dff, d), lambda i: (layer, 0, 0)),
        ],
        out_specs=pl.BlockSpec((tm, d), lambda i: (i, 0)),
        out_shape=jax.ShapeDtypeStruct((t, d), F32),
        compiler_params=_cparams(("parallel",)),
        name="dense_ffn",
    )(x, mod3, g.reshape(1, d), w1, w3, w2)


MOE_TILE = 512
MOE_UP_CHUNK = 1792
ROUTE_E1, ROUTE_E2, ROUTE_R1, ROUTE_R2, ROUTE_W1, ROUTE_W2 = range(6)


def _route_kernel(x_ref, mod_ref, g_ref, rw_ref, h_ref, route_ref, count_ref, carry_scr, *, d):
    @pl.when(pl.program_id(0) == 0)
    def _():
        carry_scr[...] = jnp.zeros_like(carry_scr)

    h = _norm_mod(x_ref[...], g_ref[...], mod_ref[0, :, 3 * d:4 * d], mod_ref[0, :, 4 * d:5 * d])
    _store_as_tiles(h_ref, h)
    logits = _dot3(h, rw_ref[...])
    tm = logits.shape[0]
    lane = lax.broadcasted_iota(jnp.int32, logits.shape, 1).astype(F32)
    neg = jnp.float32(-jnp.inf)
    l1 = jnp.where(lane < N_EXPERTS, logits, neg)
    m1 = jnp.max(l1, axis=-1, keepdims=True)
    i1 = jnp.min(jnp.where(l1 == m1, lane, float(LANES)), axis=-1, keepdims=True)
    l2 = jnp.where(lane == i1, neg, l1)
    m2 = jnp.max(l2, axis=-1, keepdims=True)
    i2 = jnp.min(jnp.where(l2 == m2, lane, float(LANES)), axis=-1, keepdims=True)
    e2 = jnp.exp(m2 - m1)
    w1 = 1.0 / (1.0 + e2)
    w2 = e2 * w1
    onehot = jnp.where((lane == i1) | (lane == i2), 1.0, 0.0)
    row = lax.broadcasted_iota(jnp.int32, (tm, tm), 0)
    col = lax.broadcasted_iota(jnp.int32, (tm, tm), 1)
    before = jnp.where(row > col, 1.0, 0.0).astype(BF16)
    rank = _dot(before, onehot.astype(BF16)) + carry_scr[...]
    r1 = jnp.sum(jnp.where(lane == i1, rank, 0.0), axis=-1, keepdims=True)
    r2 = jnp.sum(jnp.where(lane == i2, rank, 0.0), axis=-1, keepdims=True)
    rec = jnp.zeros_like(logits)
    for idx, val in ((ROUTE_E1, i1), (ROUTE_E2, i2), (ROUTE_R1, r1), (ROUTE_R2, r2), (ROUTE_W1, w1), (ROUTE_W2, w2)):
        rec = jnp.where(lane == idx, val, rec)
    route_ref[...] = rec
    total = carry_scr[...] + jnp.sum(onehot, axis=0, keepdims=True)
    carry_scr[...] = total
    count_ref[...] = total


def _route(x, mod3, g, rw_pad, seg_len):
    t, d = x.shape
    tm = min(512, seg_len)
    tps = seg_len // tm
    return pl.pallas_call(
        functools.partial(_route_kernel, d=d),
        grid=(t // tm,),
        in_specs=[
            pl.BlockSpec((tm, d), lambda i: (i, 0)),
            pl.BlockSpec((1, 1, 6 * d), lambda i: (i // tps, 0, 0)),
            pl.BlockSpec((1, d), lambda i: (0, 0)),
            pl.BlockSpec((d, LANES), lambda i: (0, 0)),
        ],
        out_specs=[
            pl.BlockSpec((tm * TILE_ROWS, LANES), lambda i: (i, 0)),
            pl.BlockSpec((tm, LANES), lambda i: (i, 0)),
            pl.BlockSpec((1, LANES), lambda i: (0, 0)),
        ],
        out_shape=[
            jax.ShapeDtypeStruct((t * TILE_ROWS, LANES), F32),
            jax.ShapeDtypeStruct((t, LANES), F32),
            jax.ShapeDtypeStruct((1, LANES), F32),
        ],
        scratch_shapes=[pltpu.VMEM((1, LANES), F32)],
        compiler_params=_cparams(("arbitrary",)),
        name="moe_route",
    )(x, mod3, g.reshape(1, d), rw_pad)


DMA_UNROLL = 8


TILE_ROWS = 8


def _store_as_tiles(ref, x):
    rows = x.shape[0]
    for c in range(TILE_ROWS):
        ref[pl.ds(c, rows, stride=TILE_ROWS), :] = x[:, c * LANES:(c + 1) * LANES]


def _tile_lane_block(ref, rows, c):
    return ref[pl.ds(c, rows, stride=TILE_ROWS), :]


def _row_tile(ref, row):
    return ref.at[pl.ds(pl.multiple_of(row * TILE_ROWS, TILE_ROWS), TILE_ROWS), :]


def _start_row_gather(src_hbm, row_of, dst, sem, n):
    def body(i, carry):
        pltpu.make_async_copy(_row_tile(src_hbm, row_of(i)), _row_tile(dst, i), sem).start()
        return carry

    lax.fori_loop(0, n, body, 0, unroll=DMA_UNROLL)


def _wait_row_gather(src_hbm, dst, sem, n):
    def body(i, carry):
        pltpu.make_async_copy(_row_tile(src_hbm, 0), _row_tile(dst, 0), sem).wait()
        return carry

    lax.fori_loop(0, n, body, 0, unroll=DMA_UNROLL)


def _gather_kernel(p1_ref, p2_ref, cs_ref, nv_ref, h_hbm, o_ref, tok_smem, buf, sem):
    j = pl.program_id(0)
    tm = o_ref.shape[0]
    n_tok = p1_ref.shape[0]
    n_valid = nv_ref[0]

    def issue(tile):
        cs = cs_ref[tile]
        slot = tile % 2
        _start_row_gather(h_hbm, lambda i: tok_smem[jnp.minimum(cs + i, 2 * n_tok - 1)], buf.at[slot], sem.at[slot], tm)

    @pl.when(j == 0)
    def _():
        def inv(t, carry):
            tok_smem[p1_ref[t]] = t
            tok_smem[p2_ref[t]] = t
            return carry

        lax.fori_loop(0, n_tok, inv, 0, unroll=DMA_UNROLL)
        issue(0)

    @pl.when(j + 1 < n_valid)
    def _():
        issue(j + 1)

    @pl.when(j < n_valid)
    def _():
        slot = j % 2
        _wait_row_gather(h_hbm, buf.at[slot], sem.at[slot], tm)
        for c in range(TILE_ROWS):
            o_ref[:, c * LANES:(c + 1) * LANES] = _tile_lane_block(buf.at[slot], tm, c).astype(BF16)

    @pl.when(j >= n_valid)
    def _():
        o_ref[...] = jnp.zeros_like(o_ref)


def _gather_sorted(h, pos1, pos2, tile_cs, n_valid):
    t = h.shape[0] // TILE_ROWS
    d = TILE_ROWS * LANES
    nt = tile_cs.shape[0]
    return pl.pallas_call(
        _gather_kernel,
        grid_spec=pltpu.PrefetchScalarGridSpec(
            num_scalar_prefetch=4,
            grid=(nt,),
            in_specs=[pl.BlockSpec(memory_space=pl.ANY)],
            out_specs=pl.BlockSpec((MOE_TILE, d), lambda j, a, b, c, nv: (j, 0)),
            scratch_shapes=[
                pltpu.SMEM((2 * t,), jnp.int32),
                pltpu.VMEM((2, MOE_TILE * TILE_ROWS, LANES), F32),
                pltpu.SemaphoreType.DMA((2,)),
            ],
        ),
        out_shape=jax.ShapeDtypeStruct((nt * MOE_TILE, d), BF16),
        compiler_params=_cparams(("arbitrary",)),
        name="moe_gather",
    )(pos1, pos2, tile_cs, n_valid, h)


def _expert_changed(te_ref, j):
    prev = te_ref[jnp.maximum(j - 1, 0)]
    return (j == 0) | (te_ref[j] != prev)


def _expert_up_kernel(te_ref, nv_ref, xs_ref, w1_ref, w3_ref, u_ref, w1b_scr, w3b_scr):
    j = pl.program_id(1)

    @pl.when(_expert_changed(te_ref, j))
    def _():
        w1b_scr[...] = w1_ref[0].astype(BF16)
        w3b_scr[...] = w3_ref[0].astype(BF16)

    @pl.when(j < nv_ref[0])
    def _():
        xs = xs_ref[...]
        tf = u_ref.shape[1]
        col_chunks = [slice(c0, min(c0 + MOE_UP_CHUNK, tf)) for c0 in range(0, tf, MOE_UP_CHUNK)]
        both = lambda cols: (_dot(xs, w1b_scr[:, cols]), _dot(xs, w3b_scr[:, cols]))
        nxt = both(col_chunks[0])
        for idx, cols in enumerate(col_chunks):
            a, b = nxt
            if idx + 1 < len(col_chunks):
                nxt = both(col_chunks[idx + 1])
            u_ref[:, cols] = (_silu(a) * b).astype(BF16)

    @pl.when(j >= nv_ref[0])
    def _():
        u_ref[...] = jnp.zeros_like(u_ref)


def _expert_up(xs, tile_expert, n_valid, w1, w3, layer):
    s, d = xs.shape
    dff = w1.shape[3]
    tf = dff // 2 if (dff // 2) % MXU_WIDTH == 0 else dff
    nt = s // MOE_TILE

    def rows(f, j, te, nv):
        return (jnp.minimum(j, nv[0] - 1), 0)

    return pl.pallas_call(
        _expert_up_kernel,
        grid_spec=pltpu.PrefetchScalarGridSpec(
            num_scalar_prefetch=2,
            grid=(dff // tf, nt),
            in_specs=[
                pl.BlockSpec((MOE_TILE, d), rows),
                pl.BlockSpec((None, 1, d, tf), lambda f, j, te, nv: (layer, te[j], 0, f)),
                pl.BlockSpec((None, 1, d, tf), lambda f, j, te, nv: (layer, te[j], 0, f)),
            ],
            out_specs=pl.BlockSpec((MOE_TILE, tf), lambda f, j, te, nv: (j, f)),
            scratch_shapes=[pltpu.VMEM((d, tf), BF16), pltpu.VMEM((d, tf), BF16)],
        ),
        out_shape=jax.ShapeDtypeStruct((s, dff), BF16),
        compiler_params=_cparams(("arbitrary", "arbitrary")),
        name="moe_expert_up",
    )(tile_expert, n_valid, xs, w1, w3)


def _expert_down_kernel(te_ref, nv_ref, u_ref, w2_ref, y_ref, w2b_scr):
    j = pl.program_id(1)

    @pl.when(_expert_changed(te_ref, j))
    def _():
        w2b_scr[...] = w2_ref[0].astype(BF16)

    @pl.when(j < nv_ref[0])
    def _():
        _store_as_tiles(y_ref, _dot(u_ref[...], w2b_scr[...]))

    @pl.when(j >= nv_ref[0])
    def _():
        y_ref[...] = jnp.zeros_like(y_ref)


def _expert_down(u, tile_expert, n_valid, w2, layer):
    s, dff = u.shape
    d = w2.shape[3]
    nt = s // MOE_TILE
    return pl.pallas_call(
        _expert_down_kernel,
        grid_spec=pltpu.PrefetchScalarGridSpec(
            num_scalar_prefetch=2,
            grid=(1, nt),
            in_specs=[
                pl.BlockSpec((MOE_TILE, dff), lambda n, j, te, nv: (jnp.minimum(j, nv[0] - 1), 0)),
                pl.BlockSpec((None, 1, dff, d), lambda n, j, te, nv: (layer, te[j], 0, 0)),
            ],
            out_specs=pl.BlockSpec((MOE_TILE * TILE_ROWS, LANES), lambda n, j, te, nv: (j, 0)),
            scratch_shapes=[pltpu.VMEM((dff, d), BF16)],
        ),
        out_shape=jax.ShapeDtypeStruct((s * TILE_ROWS, LANES), F32),
        compiler_params=_cparams(("arbitrary", "arbitrary")),
        name="moe_expert_down",
    )(tile_expert, n_valid, u, w2)


def _combine_kernel(d1_ref, d2_ref, x_ref, mod_ref, route_ref, fg_ref, ys_hbm, o_ref, buf1, buf2, sem, *,
                    d, tile0, final_norm):
    tc = x_ref.shape[0]
    j = pl.program_id(0)

    def issue(step):
        slot = step % 2
        base = (tile0 + step) * tc
        _start_row_gather(ys_hbm, lambda i: d1_ref[base + i], buf1.at[slot], sem.at[0, slot], tc)
        _start_row_gather(ys_hbm, lambda i: d2_ref[base + i], buf2.at[slot], sem.at[1, slot], tc)

    @pl.when(j == 0)
    def _():
        issue(0)

    @pl.when(j + 1 < pl.num_programs(0))
    def _():
        issue(j + 1)

    slot = j % 2
    _wait_row_gather(ys_hbm, buf1.at[slot], sem.at[0, slot], tc)
    _wait_row_gather(ys_hbm, buf2.at[slot], sem.at[1, slot], tc)
    route = route_ref[...]
    lane = lax.broadcasted_iota(jnp.int32, route.shape, 1)
    w1 = jnp.sum(jnp.where(lane == ROUTE_W1, route, 0.0), axis=-1, keepdims=True)
    w2 = jnp.sum(jnp.where(lane == ROUTE_W2, route, 0.0), axis=-1, keepdims=True)
    gate = mod_ref[0, :, 5 * d:6 * d]
    blocks = []
    for c in range(TILE_ROWS):
        cols = slice(c * LANES, (c + 1) * LANES)
        e1 = _tile_lane_block(buf1.at[slot], tc, c)
        e2 = _tile_lane_block(buf2.at[slot], tc, c)
        blocks.append(x_ref[:, cols] + gate[:, cols] * (w1 * e1 + w2 * e2))
    if final_norm:
        ms = sum(jnp.sum(b * b, axis=-1, keepdims=True) for b in blocks) * (1.0 / d)
        inv = lax.rsqrt(ms + EPS)
        blocks = [b * inv * fg_ref[:, c * LANES:(c + 1) * LANES] for c, b in enumerate(blocks)]
    for c, b in enumerate(blocks):
        o_ref[:, c * LANES:(c + 1) * LANES] = b


def _combine(x, mod3, route, ys, dest1, dest2, seg_len, *, row0=0, n_rows=None, final_g=None):
    t, d = x.shape
    n_rows = t if n_rows is None else n_rows
    tc = 256
    tps = seg_len // tc
    tile0 = row0 // tc
    final_norm = final_g is not None
    fg = (final_g if final_norm else jnp.ones((d,), F32)).reshape(1, d)
    return pl.pallas_call(
        functools.partial(_combine_kernel, d=d, tile0=tile0, final_norm=final_norm),
        grid_spec=pltpu.PrefetchScalarGridSpec(
            num_scalar_prefetch=2,
            grid=(n_rows // tc,),
            in_specs=[
                pl.BlockSpec((tc, d), lambda i, a, b: (tile0 + i, 0)),
                pl.BlockSpec((1, 1, 6 * d), lambda i, a, b: ((tile0 + i) // tps, 0, 0)),
                pl.BlockSpec((tc, LANES), lambda i, a, b: (tile0 + i, 0)),
                pl.BlockSpec((1, d), lambda i, a, b: (0, 0)),
                pl.BlockSpec(memory_space=pl.ANY),
            ],
            out_specs=pl.BlockSpec((tc, d), lambda i, a, b: (i, 0)),
            scratch_shapes=[pltpu.VMEM((2, tc * TILE_ROWS, LANES), F32), pltpu.VMEM((2, tc * TILE_ROWS, LANES), F32),
                            pltpu.SemaphoreType.DMA((2, 2))],
        ),
        out_shape=jax.ShapeDtypeStruct((n_rows, d), F32),
        compiler_params=_cparams(("arbitrary",)),
        name="moe_combine",
    )(dest1, dest2, x, mod3, route, fg, ys)


def _moe_ffn(x, mod3, g, rw_pad, w1, w3, w2, layer, seg_len, final_g=None):
    t, d = x.shape
    h, route, counts = _route(x, mod3, g, rw_pad, seg_len)
    cnt = counts[0, :N_EXPERTS].astype(jnp.int32)
    padded = (cnt + MOE_TILE - 1) // MOE_TILE * MOE_TILE
    ends = jnp.cumsum(padded)
    offs = ends - padded
    starts = jnp.cumsum(cnt) - cnt
    e1 = route[:, ROUTE_E1].astype(jnp.int32)
    e2 = route[:, ROUTE_E2].astype(jnp.int32)
    r1 = route[:, ROUTE_R1].astype(jnp.int32)
    r2 = route[:, ROUTE_R2].astype(jnp.int32)
    pos1, pos2 = starts[e1] + r1, starts[e2] + r2
    dest1, dest2 = offs[e1] + r1, offs[e2] + r2
    n_tiles = (2 * t) // MOE_TILE + N_EXPERTS
    tile_start = jnp.arange(n_tiles, dtype=jnp.int32) * MOE_TILE
    tile_expert = jnp.minimum(jnp.sum((tile_start[:, None] >= ends[None, :]).astype(jnp.int32), axis=1), N_EXPERTS - 1)
    tile_cs = starts[tile_expert] + (tile_start - offs[tile_expert])
    n_valid = (ends[-1:] // MOE_TILE).astype(jnp.int32)
    xs = _gather_sorted(h, pos1, pos2, tile_cs, n_valid)
    u = _expert_up(xs, tile_expert, n_valid, w1, w3, layer)
    ys = _expert_down(u, tile_expert, n_valid, w2, layer)
    if final_g is None:
        return _combine(x, mod3, route, ys, dest1, dest2, seg_len)
    return (_combine(x, mod3, route, ys, dest1, dest2, seg_len, row0=0, n_rows=seg_len, final_g=final_g),
            _combine(x, mod3, route, ys, dest1, dest2, seg_len, row0=seg_len, n_rows=t - seg_len, final_g=final_g))


def _final_norm_kernel(x_ref, g_ref, o_ref):
    x = x_ref[...]
    ms = jnp.mean(x * x, axis=-1, keepdims=True)
    o_ref[...] = x * lax.rsqrt(ms + EPS) * g_ref[...]


def _final_norm(x, g, *, row0, n_rows):
    d = x.shape[1]
    tm = 512
    blk0 = row0 // tm
    return pl.pallas_call(
        _final_norm_kernel,
        grid=(n_rows // tm,),
        in_specs=[pl.BlockSpec((tm, d), lambda i: (blk0 + i, 0)), pl.BlockSpec((1, d), lambda i: (0, 0))],
        out_specs=pl.BlockSpec((tm, d), lambda i: (i, 0)),
        out_shape=jax.ShapeDtypeStruct((n_rows, d), F32),
        compiler_params=_cparams(("parallel",)),
        name="final_norm",
    )(x, g.reshape(1, d))


def _rope_tables(n_tokens):
    pos = jnp.arange(n_tokens, dtype=jnp.int32)
    rowcol = jnp.stack([pos // GRID_W, pos % GRID_W], axis=-1).astype(F32)
    inv_freq = 1.0 / (ROPE_THETA ** (jnp.arange(ROPE_FREQS, dtype=F32) / ROPE_FREQS))
    ang = rowcol[:, :, None] * inv_freq
    cos = jnp.cos(ang)
    sin = jnp.sin(ang)
    cos_h = jnp.stack([cos, cos], axis=2).reshape(n_tokens, HEAD_DIM)
    sin_h = jnp.stack([-sin, sin], axis=2).reshape(n_tokens, HEAD_DIM)
    cos_t = jnp.tile(cos_h, (1, LANES // HEAD_DIM))
    sin_t = jnp.tile(sin_h, (1, LANES // HEAD_DIM))
    return (jnp.stack([jnp.ones_like(cos_t), cos_t]), jnp.stack([jnp.zeros_like(sin_t), sin_t]))


def _dft_matrices(n):
    n1 = 1 << (int(math.log2(n)) // 2)
    assert n % n1 == 0
    n2 = n // n1
    m = jnp.arange(n, dtype=jnp.int32)[None, :]
    ang_a = ((jnp.arange(n1, dtype=jnp.int32)[:, None] * m) % n1).astype(F32) * (2.0 * math.pi / n1)
    ang_b = ((jnp.arange(n2, dtype=jnp.int32)[:, None] * m) % n).astype(F32) * (2.0 * math.pi / n)
    ca, sa = jnp.cos(ang_a)[:, None, :], jnp.sin(ang_a)[:, None, :]
    cb, sb = jnp.cos(ang_b)[None, :, :], jnp.sin(ang_b)[None, :, :]
    scale = n ** -0.5
    cos_m = ((ca * cb - sa * sb) * scale).reshape(n, n)
    nsin_m = ((sa * cb + ca * sb) * -scale).reshape(n, n)
    return cos_m.astype(BF16), nsin_m.astype(BF16)


def _channel_dft_weight():
    n = FOURIER_GROUP_DIM
    idx = jnp.arange(n, dtype=jnp.int32)
    ang = ((idx[:, None] * idx[None, :]) % n).astype(F32) * (2.0 * math.pi / n)
    eye = jnp.eye(FOURIER_GROUPS, dtype=F32)
    cos_bd = jnp.kron(eye, jnp.cos(ang) * n ** -0.5)
    sin_bd = jnp.kron(eye, jnp.sin(ang) * n ** -0.5)
    return jnp.concatenate([cos_bd, sin_bd], axis=1).astype(BF16)


def _repack_w_in(w_in, lay):
    depth, d, _ = w_in.shape
    widths = (ATT_WIDTH, KV_WIDTH, KV_WIDTH, GLA_K_WIDTH, GLA_K_WIDTH, GLA_V_WIDTH, GLA_V_WIDTH,
              GLA_LOW_RANK, GLA_LOW_RANK, FOURIER_WIDTH, N_BRANCHES * d)
    offs = [0]
    for w in widths:
        offs.append(offs[-1] + w)
    aq, ak, av, gq, gk, gv, gr, lrf, lrb, fu, gates = [w_in[:, :, offs[i]:offs[i + 1]] for i in range(len(widths))]
    used = lay["lr"] + 2 * GLA_LOW_RANK
    pad = jnp.zeros((depth, d, lay["total"] - used), w_in.dtype)
    return jnp.concatenate([aq, gv, gr, fu, gates, gq, gk, ak, av, lrf, lrb, pad], axis=-1).astype(BF16)


def kernel(x_prompt, x_sample, cache_k, cache_v, state_gla, c, c_ctx, ada_w, ada_b, norm1_g, norm2_g, w_in, q_norm_g, k_norm_g, alpha_up, alpha_b, gla_norm_g, w_branch, w_out, ffn_w1, ffn_w3, ffn_w2, router_w, moe_w1, moe_w3, moe_w2, final_g):
    batch, seq, d = x_prompt.shape
    dec_batch, dec_seq, _ = x_sample.shape
    depth = w_in.shape[0]
    past = cache_k.shape[2]
    seg_len = batch * seq
    assert seg_len == dec_seq, "context tokens and each latent request must fill equal segments"
    assert seq % GLA_CHUNK == 0 and dec_seq % GLA_CHUNK == 0
    assert COL_GATES % d == 0
    assert d == TILE_ROWS * LANES, "MoE row DMAs move one (8, 128) f32 tile per token"
    n_seg = 1 + dec_batch
    t = n_seg * seg_len
    lay = _in_layout(d)

    w_in_p = _repack_w_in(w_in, lay)
    w_branch_b = w_branch.astype(BF16)
    w_out_b = w_out.astype(BF16)
    ffn_w1_b, ffn_w3_b, ffn_w2_b = ffn_w1.astype(BF16), ffn_w3.astype(BF16), ffn_w2.astype(BF16)
    router_pad = jnp.pad(router_w, ((0, 0), (0, 0), (0, LANES - N_EXPERTS)))
    cos_tab, sin_tab = _rope_tables(dec_seq)
    dft_cos_s, dft_nsin_s = _dft_matrices(dec_seq)
    dft_cos_p, dft_nsin_p = _dft_matrices(seq)
    w_chan = _channel_dft_weight()
    ones_bd = jnp.kron(jnp.eye(ATT_HEADS, dtype=F32), jnp.ones((HEAD_DIM, HEAD_DIM), F32)).astype(BF16)
    q_g = jnp.tile(q_norm_g, (1, ATT_HEADS)).reshape(depth, 1, ATT_WIDTH)
    k_g = jnp.tile(k_norm_g, (1, ATT_KV_HEADS)).reshape(depth, 1, KV_WIDTH)
    aup_pad = jnp.zeros((depth, 2, LANES, GLA_K_WIDTH), F32)
    aup_pad = aup_pad.at[:, 0, :GLA_LOW_RANK].set(alpha_up[:, 0])
    aup_pad = aup_pad.at[:, 1, GLA_LOW_RANK:2 * GLA_LOW_RANK].set(alpha_up[:, 1])
    ab = alpha_b.reshape(depth, 2, 1, GLA_K_WIDTH)
    s0_lat = state_gla.transpose(0, 1, 2, 5, 3, 4).reshape(dec_batch, depth, 2, GLA_DV, GLA_K_WIDTH)
    s0_all = jnp.concatenate([jnp.zeros((1,) + s0_lat.shape[1:], F32), s0_lat], axis=0)

    def ctx_keys(x):
        xb = x.astype(BF16)[:, :, :, None, :]
        return jnp.broadcast_to(xb, (dec_batch, past, ATT_KV_HEADS, 2, HEAD_DIM)).reshape(dec_batch, past, 2 * KV_WIDTH)

    def ctx_values_t(x):
        xt = x.astype(BF16).transpose(0, 2, 3, 1)
        return jnp.concatenate([xt, jnp.ones_like(xt)], axis=2).reshape(dec_batch, ATT_KV_HEADS * VT_ROWS, past)

    cond = jnp.zeros((8, d), F32).at[0].set(c_ctx).at[1:1 + dec_batch].set(c)
    mod_all = _modulation(cond, ada_w, ada_b)

    x = jnp.concatenate([x_prompt.reshape(seg_len, d), x_sample.reshape(dec_batch * dec_seq, d)], axis=0)
    new_k, new_v, new_s = [], [], []
    for l in range(depth):
        mod3 = mod_all[l, :n_seg].reshape(n_seg, 1, 6 * d)
        z = _in_projection(x, mod3, norm1_g[l], w_in_p, l, seg_len)
        qn, k_out, k2, vt = _attention_prep(z, lay, cos_tab, sin_tab, q_g[l], k_g[l], ones_bd, seg_len)
        att_p = _attention(qn, k2, vt, None, None, row0=0, n_seq=batch, seq_len=seq, tq=min(ATT_QUERY_TILE, seq))
        att_s = _attention(qn, k2, vt, ctx_keys(cache_k[:, l]), ctx_values_t(cache_v[:, l]),
                           row0=seg_len, n_seq=dec_batch, seq_len=dec_seq, tq=min(ATT_QUERY_TILE, dec_seq))
        gla_f, gla_b, s_fin_f, s_fin_b = _gla(z, lay, aup_pad[l], ab[l], s0_all[:, l],
                                              seg_len=seg_len, n_seg=n_seg, prompt_len=seq)
        xcs = _fourier_channels(z, w_chan)
        four_p = _fourier_positions(xcs, dft_cos_p, dft_nsin_p, row0=0, n_seq=batch, seq_len=seq)
        four_s = _fourier_positions(xcs, dft_cos_s, dft_nsin_s, row0=seg_len, n_seq=dec_batch, seq_len=dec_seq)
        x = _merge(att_p, att_s, gla_f, gla_b, z, four_p, four_s, x, mod3, gla_norm_g[l], w_branch_b, w_out_b, l, seg_len)
        j = l // 2
        if l % 2 == 0:
            x = _dense_ffn(x, mod3, norm2_g[l], ffn_w1_b, ffn_w3_b, ffn_w2_b, j, seg_len)
        elif l + 1 < depth:
            x = _moe_ffn(x, mod3, norm2_g[l], router_pad[j], moe_w1, moe_w3, moe_w2, j, seg_len)
        else:
            y_p, y_s = _moe_ffn(x, mod3, norm2_g[l], router_pad[j], moe_w1, moe_w3, moe_w2, j, seg_len, final_g=final_g)
        new_k.append(k_out[:seg_len].reshape(batch, seq, ATT_KV_HEADS, HEAD_DIM))
        v_col = lay["akv"] + KV_WIDTH
        new_v.append(z[:seg_len, v_col:v_col + KV_WIDTH].astype(F32).reshape(batch, seq, ATT_KV_HEADS, HEAD_DIM))
        cps = seq // GLA_CHUNK
        sf = s_fin_f[0, cps - 1::cps]
        sb = s_fin_b[0, 0::cps]
        st = jnp.stack([sf, sb], axis=1).reshape(batch, 2, GLA_DV, GLA_HEADS, GLA_DK)
        new_s.append(st.transpose(0, 1, 3, 4, 2))
    if depth % 2 == 1:
        y_p = _final_norm(x, final_g, row0=0, n_rows=seg_len)
        y_s = _final_norm(x, final_g, row0=seg_len, n_rows=t - seg_len)
    y_prompt = y_p.reshape(batch, seq, d)
    y_sample = y_s.reshape(dec_batch, dec_seq, d)
    return (y_prompt, y_sample, jnp.stack(new_k, axis=1), jnp.stack(new_v, axis=1), jnp.stack(new_s, axis=1))
```

```python
import functools
import math

import jax
import jax.numpy as jnp
from jax import lax
from jax.experimental import pallas as pl
from jax.experimental.pallas import tpu as pltpu

F32 = jnp.float32
BF16 = jnp.bfloat16

EPS = 1e-6
ATT_HEADS = 8
ATT_KV_HEADS = 2
HEAD_DIM = 64
ATT_WIDTH = ATT_HEADS * HEAD_DIM
KV_WIDTH = ATT_KV_HEADS * HEAD_DIM
ROPE_FREQS = HEAD_DIM // 4
ROPE_THETA = 10000.0
GRID_W = 64
GLA_HEADS = 4
GLA_DK = 64
GLA_DV = 128
GLA_K_WIDTH = GLA_HEADS * GLA_DK
GLA_V_WIDTH = GLA_HEADS * GLA_DV
GLA_LOW_RANK = 16
GLA_TAU = 16.0
FOURIER_GROUPS = 4
FOURIER_GROUP_DIM = 128
FOURIER_WIDTH = FOURIER_GROUPS * FOURIER_GROUP_DIM
N_BRANCHES = 3
N_EXPERTS = 8

LANES = 128
MXU_WIDTH = 256
VMEM_LIMIT_BYTES = 56 * 1024 * 1024

COL_AQ = 0
COL_GV = 512
COL_GR = 1024
COL_FU = 1536
COL_GATES = 2048

GLA_CHUNK = 256
GLA_SAFE_LOG = 60.0


def _cparams(sem):
    return pltpu.CompilerParams(dimension_semantics=sem, vmem_limit_bytes=VMEM_LIMIT_BYTES)


def _resident(shape, index_map):
    return pl.BlockSpec(shape, index_map, pipeline_mode=pl.Buffered(1))


def _split2(x):
    hi = x.astype(BF16)
    lo = (x - hi.astype(F32)).astype(BF16)
    return hi, lo


def _split3(x):
    hi = x.astype(BF16)
    r = x - hi.astype(F32)
    mid = r.astype(BF16)
    lo = (r - mid.astype(F32)).astype(BF16)
    return hi, mid, lo


def _dot(a, b):
    return jnp.dot(a, b, preferred_element_type=F32)


def _dot_nt(a, b):
    return lax.dot_general(a, b, (((1,), (1,)), ((), ())), preferred_element_type=F32)


def _dot_tn(a, b):
    return lax.dot_general(a, b, (((0,), (0,)), ((), ())), preferred_element_type=F32)


def _dot3(a, b):
    a_hi, a_lo = _split2(a)
    b_hi, b_lo = _split2(b)
    return _dot(a_hi, b_hi) + _dot(a_lo, b_hi) + _dot(a_hi, b_lo)


def _sigmoid(x):
    return 1.0 / (1.0 + jnp.exp(-x))


def _silu(x):
    return x * _sigmoid(x)


def _in_layout(d_model):
    gates_end = COL_GATES + N_BRANCHES * d_model
    col_gq = gates_end
    col_gk = col_gq + GLA_K_WIDTH
    col_akv = col_gk + GLA_K_WIDTH
    col_lr = col_akv + 2 * KV_WIDTH
    used = col_lr + LANES
    total = -(-used // 1024) * 1024
    return dict(gq=col_gq, gk=col_gk, akv=col_akv, lr=col_lr, total=total)


def _mod_kernel(cond_ref, w_ref, b_ref, o_ref):
    c = _silu(cond_ref[...])
    o_ref[0] = _dot3(c, w_ref[0]) + b_ref[0]


def _modulation(cond, ada_w, ada_b):
    depth, d, n6 = ada_w.shape
    tn = 1536 if n6 % 1536 == 0 else n6
    rows = cond.shape[0]
    return pl.pallas_call(
        _mod_kernel,
        grid=(depth, n6 // tn),
        in_specs=[
            pl.BlockSpec((rows, d), lambda l, j: (0, 0)),
            pl.BlockSpec((1, d, tn), lambda l, j: (l, 0, j)),
            pl.BlockSpec((1, 1, tn), lambda l, j: (l, 0, j)),
        ],
        out_specs=pl.BlockSpec((1, rows, tn), lambda l, j: (l, 0, j)),
        out_shape=jax.ShapeDtypeStruct((depth, rows, n6), F32),
        compiler_params=_cparams(("parallel", "parallel")),
        name="adaln_mod",
    )(cond, ada_w, ada_b.reshape(depth, 1, n6))


def _norm_mod(x, g, shift, scale):
    ms = jnp.mean(x * x, axis=-1, keepdims=True)
    return x * lax.rsqrt(ms + EPS) * g * (1.0 + scale) + shift


def _inproj_kernel(x_ref, mod_ref, g_ref, w_ref, z_ref, h_scr, *, d):
    @pl.when(pl.program_id(1) == 0)
    def _():
        h = _norm_mod(x_ref[...], g_ref[...], mod_ref[0, :, 0:d], mod_ref[0, :, d:2 * d])
        h_scr[...] = h.astype(BF16)

    z_ref[...] = _dot(h_scr[...], w_ref[...]).astype(z_ref.dtype)


def _in_projection(x, mod3, g, w, layer, seg_len):
    t, d = x.shape
    n = w.shape[2]
    tm = min(1024, seg_len)
    tn = 3072
    tps = seg_len // tm
    return pl.pallas_call(
        functools.partial(_inproj_kernel, d=d),
        grid=(t // tm, n // tn),
        in_specs=[
            pl.BlockSpec((tm, d), lambda i, j: (i, 0)),
            pl.BlockSpec((1, 1, 6 * d), lambda i, j: (i // tps, 0, 0)),
            pl.BlockSpec((1, d), lambda i, j: (0, 0)),
            pl.BlockSpec((None, d, tn), lambda i, j: (layer, 0, j)),
        ],
        out_specs=pl.BlockSpec((tm, tn), lambda i, j: (i, j)),
        out_shape=jax.ShapeDtypeStruct((t, n), BF16),
        scratch_shapes=[pltpu.VMEM((tm, d), BF16)],
        compiler_params=_cparams(("parallel", "arbitrary")),
        name="in_projection",
    )(x, mod3, g.reshape(1, d), w)


def _head_rms(x, g, ones_bd):
    hi, lo = _split2(x * x)
    ss = _dot(hi, ones_bd) + _dot(lo, ones_bd)
    return x * lax.rsqrt(ss * (1.0 / HEAD_DIM) + EPS) * g


def _rope(x, cos, sin_signed):
    width = x.shape[1]
    reps = width // LANES
    if reps > 1:
        cos = jnp.concatenate([cos] * reps, axis=1)
        sin_signed = jnp.concatenate([sin_signed] * reps, axis=1)
    lane = lax.broadcasted_iota(jnp.int32, x.shape, 1)
    first_half = (lane % (2 * ROPE_FREQS)) < ROPE_FREQS
    partner = jnp.where(first_half, pltpu.roll(x, width - ROPE_FREQS, 1), pltpu.roll(x, ROPE_FREQS, 1))
    return x * cos + partner * sin_signed


ATT_KEY_CHUNK = 512
ATT_QUERY_TILE = 256
VT_ROWS = 2 * HEAD_DIM
VT_USED = HEAD_DIM + 16


def _dup_heads(x):
    lane = lax.broadcasted_iota(jnp.int32, x.shape, 1)
    swapped = pltpu.roll(x, HEAD_DIM, 1)
    low = lane < HEAD_DIM
    return jnp.concatenate([jnp.where(low, x, swapped), jnp.where(low, swapped, x)], axis=1)


def _attn_prep_kernel(zq_ref, zkv_ref, cos_ref, sin_ref, qg_ref, kg_ref, bd_ref, q_ref, kout_ref, k2_ref, vt_ref):
    cos = cos_ref[0]
    sin = sin_ref[0]
    bd = bd_ref[...]
    q = _head_rms(zq_ref[...].astype(F32), qg_ref[...], bd)
    q_ref[...] = (_rope(q, cos, sin) * (HEAD_DIM ** -0.5)).astype(BF16)
    kv = zkv_ref[...].astype(F32)
    k = _head_rms(kv[:, :KV_WIDTH], kg_ref[...], bd[:KV_WIDTH, :KV_WIDTH])
    kout_ref[...] = k
    k2_ref[...] = _dup_heads(_rope(k, cos, sin)).astype(BF16)
    vt = kv[:, KV_WIDTH:].T
    ones = jnp.ones((HEAD_DIM, vt.shape[1]), F32)
    vt_ref[...] = jnp.concatenate([vt[:HEAD_DIM], ones, vt[HEAD_DIM:], ones], axis=0).astype(BF16)


def _attention_prep(z, lay, cos_tab, sin_tab, q_g, k_g, ones_bd, seg_len):
    t = z.shape[0]
    tm = min(512, seg_len)
    tps = seg_len // tm
    akv_blk = lay["akv"] // (2 * KV_WIDTH)
    tab_spec = pl.BlockSpec((1, tm, LANES), lambda i: (jnp.minimum(i // tps, 1), i % tps, 0))
    return pl.pallas_call(
        _attn_prep_kernel,
        grid=(t // tm,),
        in_specs=[
            pl.BlockSpec((tm, ATT_WIDTH), lambda i: (i, COL_AQ // ATT_WIDTH)),
            pl.BlockSpec((tm, 2 * KV_WIDTH), lambda i: (i, akv_blk)),
            tab_spec,
            tab_spec,
            pl.BlockSpec((1, ATT_WIDTH), lambda i: (0, 0)),
            pl.BlockSpec((1, KV_WIDTH), lambda i: (0, 0)),
            pl.BlockSpec((ATT_WIDTH, ATT_WIDTH), lambda i: (0, 0)),
        ],
        out_specs=[
            pl.BlockSpec((tm, ATT_WIDTH), lambda i: (i, 0)),
            pl.BlockSpec((tm, KV_WIDTH), lambda i: (i, 0)),
            pl.BlockSpec((tm, 2 * KV_WIDTH), lambda i: (i, 0)),
            pl.BlockSpec((ATT_KV_HEADS * VT_ROWS, tm), lambda i: (0, i)),
        ],
        out_shape=[
            jax.ShapeDtypeStruct((t, ATT_WIDTH), BF16),
            jax.ShapeDtypeStruct((t, KV_WIDTH), F32),
            jax.ShapeDtypeStruct((t, 2 * KV_WIDTH), BF16),
            jax.ShapeDtypeStruct((ATT_KV_HEADS * VT_ROWS, t), BF16),
        ],
        compiler_params=_cparams(("parallel",)),
        name="attention_prep",
    )(z, z, cos_tab, sin_tab, q_g, k_g, ones_bd)


def _attn_kernel(*refs, has_ctx):
    if has_ctx:
        q_ref, k_ref, v_ref, ck_ref, cv_ref, o_ref = refs
    else:
        q_ref, k_ref, v_ref, o_ref = refs
    tq = q_ref.shape[0]
    sources = [(k_ref, v_ref)] + ([(ck_ref.at[0], cv_ref.at[0])] if has_ctx else [])
    chunks = []
    for kr, vr in sources:
        n = min(ATT_KEY_CHUNK, kr.shape[0])
        chunks += [(kr, vr, c * n, n) for c in range(kr.shape[0] // n)]
    low = lax.broadcasted_iota(jnp.int32, (tq, LANES), 1) < HEAD_DIM
    zero = jnp.zeros((tq, LANES), BF16)
    n_heads = ATT_HEADS // ATT_KV_HEADS
    ws = []
    for h in range(n_heads):
        q2 = q_ref[:, (h // 2) * LANES:(h // 2 + 1) * LANES]
        ws.append(jnp.where(low, q2, zero) if h % 2 == 0 else jnp.where(low, zero, q2))
    w = jnp.concatenate(ws, axis=0)
    maxes, parts = [], []
    scores = lambda c: _dot_nt(c[0][c[2]:c[2] + c[3], :], w)
    st_next = scores(chunks[0])
    for idx, (kr, vr, r0, n) in enumerate(chunks):
        st = st_next
        if idx + 1 < len(chunks):
            st_next = scores(chunks[idx + 1])
        mc = jnp.max(st, axis=0, keepdims=True)
        p = jnp.exp(st - mc).astype(BF16)
        maxes.append(mc)
        parts.append(_dot(vr[0:VT_USED, r0:r0 + n], p))
    m = functools.reduce(jnp.maximum, maxes)
    acc = sum(jnp.exp(mc - m) * part for mc, part in zip(maxes, parts))
    o_t = acc[:HEAD_DIM] / acc[HEAD_DIM:HEAD_DIM + 1]
    o_heads = jnp.concatenate([o_t[:, h * tq:(h + 1) * tq] for h in range(n_heads)], axis=0)
    o_ref[...] = o_heads.T.astype(BF16)


def _attention(q, k2, v2, ctx_k2, ctx_v2, *, row0, n_seq, seq_len, tq):
    ntq = seq_len // tq
    gw = ATT_WIDTH // ATT_KV_HEADS
    qblk0 = row0 // tq
    kblk0 = row0 // seq_len
    has_ctx = ctx_k2 is not None
    in_specs = [
        pl.BlockSpec((tq, gw), lambda b, g, i: (qblk0 + b * ntq + i, g)),
        pl.BlockSpec((seq_len, LANES), lambda b, g, i: (kblk0 + b, g)),
        pl.BlockSpec((VT_ROWS, seq_len), lambda b, g, i: (g, kblk0 + b)),
    ]
    args = [q, k2, v2]
    if has_ctx:
        past = ctx_k2.shape[1]
        in_specs += [pl.BlockSpec((1, past, LANES), lambda b, g, i: (b, 0, g)),
                     pl.BlockSpec((1, VT_ROWS, past), lambda b, g, i: (b, g, 0))]
        args += [ctx_k2, ctx_v2]
    return pl.pallas_call(
        functools.partial(_attn_kernel, has_ctx=has_ctx),
        grid=(n_seq, ATT_KV_HEADS, ntq),
        in_specs=in_specs,
        out_specs=pl.BlockSpec((tq, gw), lambda b, g, i: (b * ntq + i, g)),
        out_shape=jax.ShapeDtypeStruct((n_seq * seq_len, ATT_WIDTH), BF16),
        compiler_params=_cparams(("parallel", "parallel", "parallel")),
        name="attention_ctx" if has_ctx else "attention_self",
    )(*args)


def _gla_prepare(lr_ref, aup, ab, fwd):
    c = lr_ref.shape[0]
    aup_hi, aup_lo = _split2(aup)
    lr = lr_ref[...]
    x = _dot(lr, aup_hi) + _dot(lr, aup_lo) + ab
    la = (jnp.minimum(x, 0.0) - jnp.log(1.0 + jnp.exp(-jnp.abs(x)))) * (1.0 / GLA_TAU)
    row = lax.broadcasted_iota(jnp.int32, (c, c), 0)
    col = lax.broadcasted_iota(jnp.int32, (c, c), 1)
    tri = (row >= col) if fwd else (row <= col)
    tri_b = jnp.where(tri, 1.0, 0.0).astype(BF16)
    la_hi, la_mid, la_lo = _split3(la)
    cum = _dot(tri_b, la_hi) + _dot(tri_b, la_mid) + _dot(tri_b, la_lo)
    bend = jnp.sum(la, axis=0, keepdims=True)
    return la, cum, bend, tri, jnp.min(cum) >= -GLA_SAFE_LOG


def _gla_fast(q_ref, k_ref, v_ref, o_ref, sfin_ref, s_ref, cum, bend, tri):
    head_of_lane = lax.broadcasted_iota(jnp.int32, (1, GLA_K_WIDTH), 1) // GLA_DK
    q = q_ref[...].astype(F32) * (GLA_DK ** -0.5)
    k = k_ref[...].astype(F32)
    vb = v_ref[...]
    qt = (q * jnp.exp(cum)).astype(BF16)
    kt = (k * jnp.exp(-cum)).astype(BF16)
    ke = (k * jnp.exp(bend - cum)).astype(BF16)
    s_old = s_ref[...]
    s_b = s_old.astype(BF16)
    s_new = jnp.exp(bend) * s_old
    zero = jnp.zeros_like(qt)
    for h in range(GLA_HEADS):
        hm = head_of_lane == h
        qh = jnp.where(hm, qt, zero)
        a = jnp.where(tri, _dot_nt(qh, kt), 0.0).astype(BF16)
        vh = vb[:, h * GLA_DV:(h + 1) * GLA_DV]
        o_ref[:, h * GLA_DV:(h + 1) * GLA_DV] = _dot(a, vh) + _dot_nt(qh, s_b)
        s_new = s_new + _dot_tn(vh, jnp.where(hm, ke, zero))
    s_ref[...] = s_new
    sfin_ref[0, 0] = s_new


def _gla_slow(q_ref, k_ref, v_ref, o_ref, sfin_ref, s_ref, la, fwd, la_scr, qf_scr, kf_scr, vt_scr, ot_scr):
    c = q_ref.shape[0]
    head_of_lane = lax.broadcasted_iota(jnp.int32, (1, GLA_K_WIDTH), 1) // GLA_DK
    la_scr[...] = la
    qf_scr[...] = q_ref[...].astype(F32) * (GLA_DK ** -0.5)
    kf_scr[...] = k_ref[...].astype(F32)
    vt_scr[...] = v_ref[...].astype(F32).T
    ot_scr[...] = jnp.zeros_like(ot_scr)
    lane_c = lax.broadcasted_iota(jnp.int32, (1, c), 1)

    def body(i, s):
        t = i if fwd else c - 1 - i
        a_t = jnp.exp(la_scr[pl.ds(t, 1), :])
        k_t = kf_scr[pl.ds(t, 1), :]
        q_t = qf_scr[pl.ds(t, 1), :]
        onehot = lane_c == t
        vmat = jnp.zeros((GLA_DV, GLA_K_WIDTH), F32)
        for h in range(GLA_HEADS):
            vcol = jnp.sum(jnp.where(onehot, vt_scr[h * GLA_DV:(h + 1) * GLA_DV, :], 0.0), axis=1, keepdims=True)
            vmat = jnp.where(head_of_lane == h, vcol, vmat)
        s = a_t * s + vmat * k_t
        prod = s * q_t
        for h in range(GLA_HEADS):
            ocol = jnp.sum(jnp.where(head_of_lane == h, prod, 0.0), axis=1, keepdims=True)
            rows = slice(h * GLA_DV, (h + 1) * GLA_DV)
            ot_scr[rows, :] = jnp.where(onehot, ocol, ot_scr[rows, :])
        return s

    s_new = lax.fori_loop(0, c, body, s_ref[...])
    o_ref[...] = ot_scr[...].T
    s_ref[...] = s_new
    sfin_ref[0, 0] = s_new


def _gla_kernel(qf_ref, kf_ref, vf_ref, lrf_ref, qb_ref, kb_ref, vb_ref, lrb_ref, aup_ref, ab_ref, s0_ref,
                of_ref, ob_ref, sff_ref, sfb_ref, s_scr, *slow_scr, chunks_per_seq):
    @pl.when(pl.program_id(1) % chunks_per_seq == 0)
    def _():
        s_scr[...] = s0_ref[0]

    dirs = ((True, qf_ref, kf_ref, vf_ref, lrf_ref, of_ref, sff_ref), (False, qb_ref, kb_ref, vb_ref, lrb_ref, ob_ref, sfb_ref))
    prep = [_gla_prepare(d[4], aup_ref[i], ab_ref[i], d[0]) for i, d in enumerate(dirs)]
    both_safe = prep[0][4] & prep[1][4]

    def fast(i):
        _, q_ref, k_ref, v_ref, _, o_ref, sfin_ref = dirs[i]
        _, cum, bend, tri, _ = prep[i]
        _gla_fast(q_ref, k_ref, v_ref, o_ref, sfin_ref, s_scr.at[i], cum, bend, tri)

    @pl.when(both_safe)
    def _():
        fast(0)
        fast(1)

    @pl.when(jnp.logical_not(both_safe))
    def _():
        for i, (fwd, q_ref, k_ref, v_ref, _, o_ref, sfin_ref) in enumerate(dirs):
            pl.when(prep[i][4])(functools.partial(fast, i))
            slow = functools.partial(_gla_slow, q_ref, k_ref, v_ref, o_ref, sfin_ref, s_scr.at[i], prep[i][0], fwd, *slow_scr)
            pl.when(jnp.logical_not(prep[i][4]))(slow)


def _gla(z, lay, aup_pad, ab, s0, *, seg_len, n_seg, prompt_len):
    t = z.shape[0]
    c = GLA_CHUNK
    nc = seg_len // c
    cps_prompt = prompt_len // c
    gq_blk = lay["gq"] // GLA_K_WIDTH
    gk_blk = lay["gk"] // GLA_K_WIDTH
    gv_blk = COL_GV // GLA_V_WIDTH
    lr_blk = lay["lr"] // LANES

    fwd_chunk = lambda s, n: s * nc + n
    bwd_chunk = lambda s, n: s * nc + nc - 1 - n

    def kernel(*refs):
        cps = jnp.where(pl.program_id(0) == 0, cps_prompt, nc)
        _gla_kernel(*refs, chunks_per_seq=cps)

    def operand_specs(chunk):
        return [
            pl.BlockSpec((c, GLA_K_WIDTH), lambda s, n: (chunk(s, n), gq_blk)),
            pl.BlockSpec((c, GLA_K_WIDTH), lambda s, n: (chunk(s, n), gk_blk)),
            pl.BlockSpec((c, GLA_V_WIDTH), lambda s, n: (chunk(s, n), gv_blk)),
            pl.BlockSpec((c, LANES), lambda s, n: (chunk(s, n), lr_blk)),
        ]

    return pl.pallas_call(
        kernel,
        grid=(n_seg, nc),
        in_specs=operand_specs(fwd_chunk) + operand_specs(bwd_chunk) + [
            pl.BlockSpec((2, LANES, GLA_K_WIDTH), lambda s, n: (0, 0, 0)),
            pl.BlockSpec((2, 1, GLA_K_WIDTH), lambda s, n: (0, 0, 0)),
            pl.BlockSpec((1, 2, GLA_DV, GLA_K_WIDTH), lambda s, n: (s, 0, 0, 0)),
        ],
        out_specs=[
            pl.BlockSpec((c, GLA_V_WIDTH), lambda s, n: (fwd_chunk(s, n), 0)),
            pl.BlockSpec((c, GLA_V_WIDTH), lambda s, n: (bwd_chunk(s, n), 0)),
            pl.BlockSpec((1, 1, GLA_DV, GLA_K_WIDTH), lambda s, n: (s, n, 0, 0)),
            pl.BlockSpec((1, 1, GLA_DV, GLA_K_WIDTH), lambda s, n: (s, nc - 1 - n, 0, 0)),
        ],
        out_shape=[
            jax.ShapeDtypeStruct((t, GLA_V_WIDTH), F32),
            jax.ShapeDtypeStruct((t, GLA_V_WIDTH), F32),
            jax.ShapeDtypeStruct((n_seg, nc, GLA_DV, GLA_K_WIDTH), F32),
            jax.ShapeDtypeStruct((n_seg, nc, GLA_DV, GLA_K_WIDTH), F32),
        ],
        scratch_shapes=[
            pltpu.VMEM((2, GLA_DV, GLA_K_WIDTH), F32),
            pltpu.VMEM((c, GLA_K_WIDTH), F32),
            pltpu.VMEM((c, GLA_K_WIDTH), F32),
            pltpu.VMEM((c, GLA_K_WIDTH), F32),
            pltpu.VMEM((GLA_V_WIDTH, c), F32),
            pltpu.VMEM((GLA_V_WIDTH, c), F32),
        ],
        compiler_params=_cparams(("parallel", "arbitrary")),
        name="gla_scan",
    )(z, z, z, z, z, z, z, z, aup_pad, ab, s0)


def _four_chan_kernel(u_ref, w_ref, o_ref):
    o_ref[...] = _dot(u_ref[...], w_ref[...]).astype(BF16)


def _fourier_channels(z, w_chan):
    t = z.shape[0]
    tm = 512
    return pl.pallas_call(
        _four_chan_kernel,
        grid=(t // tm,),
        in_specs=[
            pl.BlockSpec((tm, FOURIER_WIDTH), lambda i: (i, COL_FU // FOURIER_WIDTH)),
            _resident((FOURIER_WIDTH, 2 * FOURIER_WIDTH), lambda i: (0, 0)),
        ],
        out_specs=pl.BlockSpec((tm, 2 * FOURIER_WIDTH), lambda i: (i, 0)),
        out_shape=jax.ShapeDtypeStruct((t, 2 * FOURIER_WIDTH), BF16),
        compiler_params=_cparams(("parallel",)),
        name="fourier_channels",
    )(z, w_chan)


def _four_pos_kernel(c_ref, s_ref, xc_ref, xs_ref, o_ref):
    o_ref[...] = (_dot(c_ref[...], xc_ref[...]) + _dot(s_ref[...], xs_ref[...])).astype(BF16)


def _fourier_positions(xcs, cos_m, nsin_m, *, row0, n_seq, seq_len):
    tm = min(512, seq_len)
    nt = seq_len // tm
    blk0 = row0 // seq_len
    return pl.pallas_call(
        _four_pos_kernel,
        grid=(n_seq, nt),
        in_specs=[
            pl.BlockSpec((tm, seq_len), lambda b, i: (i, 0)),
            pl.BlockSpec((tm, seq_len), lambda b, i: (i, 0)),
            pl.BlockSpec((seq_len, FOURIER_WIDTH), lambda b, i: (blk0 + b, 0)),
            pl.BlockSpec((seq_len, FOURIER_WIDTH), lambda b, i: (blk0 + b, 1)),
        ],
        out_specs=pl.BlockSpec((tm, FOURIER_WIDTH), lambda b, i: (b * nt + i, 0)),
        out_shape=jax.ShapeDtypeStruct((n_seq * seq_len, FOURIER_WIDTH), BF16),
        compiler_params=_cparams(("parallel", "parallel")),
        name="fourier_positions",
    )(cos_m, nsin_m, xcs, xcs)


def _merge_kernel(attp_ref, atts_ref, gof_ref, gob_ref, gr_ref, fourp_ref, fours_ref, g0_ref, g1_ref, g2_ref, x_ref,
                  mod_ref, gg_ref, wb_ref, wo_ref, o_ref, *, d, prompt_tiles):
    in_prompt = pl.program_id(0) < prompt_tiles
    att = jnp.where(in_prompt, attp_ref[...], atts_ref[...])
    four = jnp.where(in_prompt, fourp_ref[...], fours_ref[...])
    o = gof_ref[...] + gob_ref[...]
    gg = gg_ref[...]
    parts = []
    for h in range(GLA_HEADS):
        oh = o[:, h * GLA_DV:(h + 1) * GLA_DV]
        ms = jnp.mean(oh * oh, axis=-1, keepdims=True)
        parts.append(oh * lax.rsqrt(ms + EPS) * gg)
    gla = (jnp.concatenate(parts, axis=1) * _silu(gr_ref[...].astype(F32))).astype(BF16)
    merged = (_sigmoid(g0_ref[...].astype(F32)) * _dot(att, wb_ref[0])
              + _sigmoid(g1_ref[...].astype(F32)) * _dot(gla, wb_ref[1])
              + _sigmoid(g2_ref[...].astype(F32)) * _dot(four, wb_ref[2]))
    mix = _dot(merged.astype(BF16), wo_ref[...])
    o_ref[...] = x_ref[...] + mod_ref[0, :, 2 * d:3 * d] * mix


def _merge(att_p, att_s, gla_f, gla_b, z, four_p, four_s, x, mod3, gla_g, w_branch, w_out, layer, seg_len):
    t, d = x.shape
    tm = min(512, seg_len)
    tps = seg_len // tm
    bw = att_p.shape[1]
    gate_blk = COL_GATES // d
    prompt_tiles = att_p.shape[0] // tm
    p_spec = pl.BlockSpec((tm, bw), lambda i: (jnp.minimum(i, prompt_tiles - 1), 0))
    s_spec = pl.BlockSpec((tm, bw), lambda i: (jnp.maximum(i - prompt_tiles, 0), 0))
    return pl.pallas_call(
        functools.partial(_merge_kernel, d=d, prompt_tiles=prompt_tiles),
        grid=(t // tm,),
        in_specs=[
            p_spec,
            s_spec,
            pl.BlockSpec((tm, GLA_V_WIDTH), lambda i: (i, 0)),
            pl.BlockSpec((tm, GLA_V_WIDTH), lambda i: (i, 0)),
            pl.BlockSpec((tm, GLA_V_WIDTH), lambda i: (i, COL_GR // GLA_V_WIDTH)),
            p_spec,
            s_spec,
            pl.BlockSpec((tm, d), lambda i: (i, gate_blk)),
            pl.BlockSpec((tm, d), lambda i: (i, gate_blk + 1)),
            pl.BlockSpec((tm, d), lambda i: (i, gate_blk + 2)),
            pl.BlockSpec((tm, d), lambda i: (i, 0)),
            pl.BlockSpec((1, 1, 6 * d), lambda i: (i // tps, 0, 0)),
            pl.BlockSpec((1, GLA_DV), lambda i: (0, 0)),
            _resident((None, N_BRANCHES, bw, d), lambda i: (layer, 0, 0, 0)),
            _resident((None, d, d), lambda i: (layer, 0, 0)),
        ],
        out_specs=pl.BlockSpec((tm, d), lambda i: (i, 0)),
        out_shape=jax.ShapeDtypeStruct((t, d), F32),
        compiler_params=_cparams(("parallel",)),
        name="branch_merge",
    )(att_p, att_s, gla_f, gla_b, z, four_p, four_s, z, z, z, x, mod3, gla_g.reshape(1, GLA_DV), w_branch, w_out)


def _ffn_kernel(x_ref, mod_ref, g_ref, w1_ref, w3_ref, w2_ref, o_ref, *, d):
    x = x_ref[...]
    h = _norm_mod(x, g_ref[...], mod_ref[0, :, 3 * d:4 * d], mod_ref[0, :, 4 * d:5 * d]).astype(BF16)
    a = _dot(h, w1_ref[...])
    u = (_silu(a) * _dot(h, w3_ref[...])).astype(BF16)
    o_ref[...] = x + mod_ref[0, :, 5 * d:6 * d] * _dot(u, w2_ref[...])


def _dense_ffn(x, mod3, g, w1, w3, w2, layer, seg_len):
    t, d = x.shape
    dff = w1.shape[2]
    tm = min(512, seg_len)
    tps = seg_len // tm
    return pl.pallas_call(
        functools.partial(_ffn_kernel, d=d),
        grid=(t // tm,),
        in_specs=[
            pl.BlockSpec((tm, d), lambda i: (i, 0)),
            pl.BlockSpec((1, 1, 6 * d), lambda i: (i // tps, 0, 0)),
            pl.BlockSpec((1, d), lambda i: (0, 0)),
            _resident((None, d, dff), lambda i: (layer, 0, 0)),
            _resident((None, d, dff), lambda i: (layer, 0, 0)),
            _resident((None, dff, d), lambda i: (layer, 0, 0)),
        ],
        out_specs=pl.BlockSpec((tm, d), lambda i: (i, 0)),
        out_shape=jax.ShapeDtypeStruct((t, d), F32),
        compiler_params=_cparams(("parallel",)),
        name="dense_ffn",
    )(x, mod3, g.reshape(1, d), w1, w3, w2)


MOE_TILE = 512
MOE_UP_CHUNK = 1792
ROUTE_E1, ROUTE_E2, ROUTE_R1, ROUTE_R2, ROUTE_W1, ROUTE_W2 = range(6)


def _route_kernel(x_ref, mod_ref, g_ref, rw_ref, h_ref, route_ref, count_ref, carry_scr, *, d):
    @pl.when(pl.program_id(0) == 0)
    def _():
        carry_scr[...] = jnp.zeros_like(carry_scr)

    h = _norm_mod(x_ref[...], g_ref[...], mod_ref[0, :, 3 * d:4 * d], mod_ref[0, :, 4 * d:5 * d])
    _store_as_tiles(h_ref, h)
    logits = _dot3(h, rw_ref[...])
    tm = logits.shape[0]
    lane = lax.broadcasted_iota(jnp.int32, logits.shape, 1).astype(F32)
    neg = jnp.float32(-jnp.inf)
    l1 = jnp.where(lane < N_EXPERTS, logits, neg)
    m1 = jnp.max(l1, axis=-1, keepdims=True)
    i1 = jnp.min(jnp.where(l1 == m1, lane, float(LANES)), axis=-1, keepdims=True)
    l2 = jnp.where(lane == i1, neg, l1)
    m2 = jnp.max(l2, axis=-1, keepdims=True)
    i2 = jnp.min(jnp.where(l2 == m2, lane, float(LANES)), axis=-1, keepdims=True)
    e2 = jnp.exp(m2 - m1)
    w1 = 1.0 / (1.0 + e2)
    w2 = e2 * w1
    onehot = jnp.where((lane == i1) | (lane == i2), 1.0, 0.0)
    row = lax.broadcasted_iota(jnp.int32, (tm, tm), 0)
    col = lax.broadcasted_iota(jnp.int32, (tm, tm), 1)
    before = jnp.where(row > col, 1.0, 0.0).astype(BF16)
    rank = _dot(before, onehot.astype(BF16)) + carry_scr[...]
    r1 = jnp.sum(jnp.where(lane == i1, rank, 0.0), axis=-1, keepdims=True)
    r2 = jnp.sum(jnp.where(lane == i2, rank, 0.0), axis=-1, keepdims=True)
    rec = jnp.zeros_like(logits)
    for idx, val in ((ROUTE_E1, i1), (ROUTE_E2, i2), (ROUTE_R1, r1), (ROUTE_R2, r2), (ROUTE_W1, w1), (ROUTE_W2, w2)):
        rec = jnp.where(lane == idx, val, rec)
    route_ref[...] = rec
    total = carry_scr[...] + jnp.sum(onehot, axis=0, keepdims=True)
    carry_scr[...] = total
    count_ref[...] = total


def _route(x, mod3, g, rw_pad, seg_len):
    t, d = x.shape
    tm = min(512, seg_len)
    tps = seg_len // tm
    return pl.pallas_call(
        functools.partial(_route_kernel, d=d),
        grid=(t // tm,),
        in_specs=[
            pl.BlockSpec((tm, d), lambda i: (i, 0)),
            pl.BlockSpec((1, 1, 6 * d), lambda i: (i // tps, 0, 0)),
            pl.BlockSpec((1, d), lambda i: (0, 0)),
            pl.BlockSpec((d, LANES), lambda i: (0, 0)),
        ],
        out_specs=[
            pl.BlockSpec((tm * TILE_ROWS, LANES), lambda i: (i, 0)),
            pl.BlockSpec((tm, LANES), lambda i: (i, 0)),
            pl.BlockSpec((1, LANES), lambda i: (0, 0)),
        ],
        out_shape=[
            jax.ShapeDtypeStruct((t * TILE_ROWS, LANES), F32),
            jax.ShapeDtypeStruct((t, LANES), F32),
            jax.ShapeDtypeStruct((1, LANES), F32),
        ],
        scratch_shapes=[pltpu.VMEM((1, LANES), F32)],
        compiler_params=_cparams(("arbitrary",)),
        name="moe_route",
    )(x, mod3, g.reshape(1, d), rw_pad)


DMA_UNROLL = 8


TILE_ROWS = 8


def _store_as_tiles(ref, x):
    rows = x.shape[0]
    for c in range(TILE_ROWS):
        ref[pl.ds(c, rows, stride=TILE_ROWS), :] = x[:, c * LANES:(c + 1) * LANES]


def _tile_lane_block(ref, rows, c):
    return ref[pl.ds(c, rows, stride=TILE_ROWS), :]


def _row_tile(ref, row):
    return ref.at[pl.ds(pl.multiple_of(row * TILE_ROWS, TILE_ROWS), TILE_ROWS), :]


def _start_row_gather(src_hbm, row_of, dst, sem, n):
    def body(i, carry):
        pltpu.make_async_copy(_row_tile(src_hbm, row_of(i)), _row_tile(dst, i), sem).start()
        return carry

    lax.fori_loop(0, n, body, 0, unroll=DMA_UNROLL)


def _wait_row_gather(src_hbm, dst, sem, n):
    def body(i, carry):
        pltpu.make_async_copy(_row_tile(src_hbm, 0), _row_tile(dst, 0), sem).wait()
        return carry

    lax.fori_loop(0, n, body, 0, unroll=DMA_UNROLL)


def _gather_kernel(p1_ref, p2_ref, cs_ref, nv_ref, h_hbm, o_ref, tok_smem, buf, sem):
    j = pl.program_id(0)
    tm = o_ref.shape[0]
    n_tok = p1_ref.shape[0]
    n_valid = nv_ref[0]

    def issue(tile):
        cs = cs_ref[tile]
        slot = tile % 2
        _start_row_gather(h_hbm, lambda i: tok_smem[jnp.minimum(cs + i, 2 * n_tok - 1)], buf.at[slot], sem.at[slot], tm)

    @pl.when(j == 0)
    def _():
        def inv(t, carry):
            tok_smem[p1_ref[t]] = t
            tok_smem[p2_ref[t]] = t
            return carry

        lax.fori_loop(0, n_tok, inv, 0, unroll=DMA_UNROLL)
        issue(0)

    @pl.when(j + 1 < n_valid)
    def _():
        issue(j + 1)

    @pl.when(j < n_valid)
    def _():
        slot = j % 2
        _wait_row_gather(h_hbm, buf.at[slot], sem.at[slot], tm)
        for c in range(TILE_ROWS):
            o_ref[:, c * LANES:(c + 1) * LANES] = _tile_lane_block(buf.at[slot], tm, c).astype(BF16)

    @pl.when(j >= n_valid)
    def _():
        o_ref[...] = jnp.zeros_like(o_ref)


def _gather_sorted(h, pos1, pos2, tile_cs, n_valid):
    t = h.shape[0] // TILE_ROWS
    d = TILE_ROWS * LANES
    nt = tile_cs.shape[0]
    return pl.pallas_call(
        _gather_kernel,
        grid_spec=pltpu.PrefetchScalarGridSpec(
            num_scalar_prefetch=4,
            grid=(nt,),
            in_specs=[pl.BlockSpec(memory_space=pl.ANY)],
            out_specs=pl.BlockSpec((MOE_TILE, d), lambda j, a, b, c, nv: (j, 0)),
            scratch_shapes=[
                pltpu.SMEM((2 * t,), jnp.int32),
                pltpu.VMEM((2, MOE_TILE * TILE_ROWS, LANES), F32),
                pltpu.SemaphoreType.DMA((2,)),
            ],
        ),
        out_shape=jax.ShapeDtypeStruct((nt * MOE_TILE, d), BF16),
        compiler_params=_cparams(("arbitrary",)),
        name="moe_gather",
    )(pos1, pos2, tile_cs, n_valid, h)


def _expert_changed(te_ref, j):
    prev = te_ref[jnp.maximum(j - 1, 0)]
    return (j == 0) | (te_ref[j] != prev)


def _expert_up_kernel(te_ref, nv_ref, xs_ref, w1_ref, w3_ref, u_ref, w1b_scr, w3b_scr):
    j = pl.program_id(1)

    @pl.when(_expert_changed(te_ref, j))
    def _():
        w1b_scr[...] = w1_ref[0].astype(BF16)
        w3b_scr[...] = w3_ref[0].astype(BF16)

    @pl.when(j < nv_ref[0])
    def _():
        xs = xs_ref[...]
        tf = u_ref.shape[1]
        col_chunks = [slice(c0, min(c0 + MOE_UP_CHUNK, tf)) for c0 in range(0, tf, MOE_UP_CHUNK)]
        both = lambda cols: (_dot(xs, w1b_scr[:, cols]), _dot(xs, w3b_scr[:, cols]))
        nxt = both(col_chunks[0])
        for idx, cols in enumerate(col_chunks):
            a, b = nxt
            if idx + 1 < len(col_chunks):
                nxt = both(col_chunks[idx + 1])
            u_ref[:, cols] = (_silu(a) * b).astype(BF16)

    @pl.when(j >= nv_ref[0])
    def _():
        u_ref[...] = jnp.zeros_like(u_ref)


def _expert_up(xs, tile_expert, n_valid, w1, w3, layer):
    s, d = xs.shape
    dff = w1.shape[3]
    tf = dff // 2 if (dff // 2) % MXU_WIDTH == 0 else dff
    nt = s // MOE_TILE

    def rows(f, j, te, nv):
        return (jnp.minimum(j, nv[0] - 1), 0)

    return pl.pallas_call(
        _expert_up_kernel,
        grid_spec=pltpu.PrefetchScalarGridSpec(
            num_scalar_prefetch=2,
            grid=(dff // tf, nt),
            in_specs=[
                pl.BlockSpec((MOE_TILE, d), rows),
                pl.BlockSpec((None, 1, d, tf), lambda f, j, te, nv: (layer, te[j], 0, f)),
                pl.BlockSpec((None, 1, d, tf), lambda f, j, te, nv: (layer, te[j], 0, f)),
            ],
            out_specs=pl.BlockSpec((MOE_TILE, tf), lambda f, j, te, nv: (j, f)),
            scratch_shapes=[pltpu.VMEM((d, tf), BF16), pltpu.VMEM((d, tf), BF16)],
        ),
        out_shape=jax.ShapeDtypeStruct((s, dff), BF16),
        compiler_params=_cparams(("arbitrary", "arbitrary")),
        name="moe_expert_up",
    )(tile_expert, n_valid, xs, w1, w3)


def _expert_down_kernel(te_ref, nv_ref, u_ref, w2_ref, y_ref, w2b_scr):
    j = pl.program_id(1)

    @pl.when(_expert_changed(te_ref, j))
    def _():
        w2b_scr[...] = w2_ref[0].astype(BF16)

    @pl.when(j < nv_ref[0])
    def _():
        _store_as_tiles(y_ref, _dot(u_ref[...], w2b_scr[...]))

    @pl.when(j >= nv_ref[0])
    def _():
        y_ref[...] = jnp.zeros_like(y_ref)


def _expert_down(u, tile_expert, n_valid, w2, layer):
    s, dff = u.shape
    d = w2.shape[3]
    nt = s // MOE_TILE
    return pl.pallas_call(
        _expert_down_kernel,
        grid_spec=pltpu.PrefetchScalarGridSpec(
            num_scalar_prefetch=2,
            grid=(1, nt),
            in_specs=[
                pl.BlockSpec((MOE_TILE, dff), lambda n, j, te, nv: (jnp.minimum(j, nv[0] - 1), 0)),
                pl.BlockSpec((None, 1, dff, d), lambda n, j, te, nv: (layer, te[j], 0, 0)),
            ],
            out_specs=pl.BlockSpec((MOE_TILE * TILE_ROWS, LANES), lambda n, j, te, nv: (j, 0)),
            scratch_shapes=[pltpu.VMEM((dff, d), BF16)],
        ),
        out_shape=jax.ShapeDtypeStruct((s * TILE_ROWS, LANES), F32),
        compiler_params=_cparams(("arbitrary", "arbitrary")),
        name="moe_expert_down",
    )(tile_expert, n_valid, u, w2)


def _combine_kernel(d1_ref, d2_ref, x_ref, mod_ref, route_ref, *rest, d, tile0, final_norm):
    if final_norm:
        fg_ref, ys_hbm, o_ref, buf1, buf2, sem = rest
    else:
        ys_hbm, o_ref, buf1, buf2, sem = rest
    tc = x_ref.shape[0]
    j = pl.program_id(0)

    def issue(step):
        slot = step % 2
        base = (tile0 + step) * tc
        _start_row_gather(ys_hbm, lambda i: d1_ref[base + i], buf1.at[slot], sem.at[0, slot], tc)
        _start_row_gather(ys_hbm, lambda i: d2_ref[base + i], buf2.at[slot], sem.at[1, slot], tc)

    @pl.when(j == 0)
    def _():
        issue(0)

    @pl.when(j + 1 < pl.num_programs(0))
    def _():
        issue(j + 1)

    slot = j % 2
    _wait_row_gather(ys_hbm, buf1.at[slot], sem.at[0, slot], tc)
    _wait_row_gather(ys_hbm, buf2.at[slot], sem.at[1, slot], tc)
    route = route_ref[...]
    lane = lax.broadcasted_iota(jnp.int32, route.shape, 1)
    w1 = jnp.sum(jnp.where(lane == ROUTE_W1, route, 0.0), axis=-1, keepdims=True)
    w2 = jnp.sum(jnp.where(lane == ROUTE_W2, route, 0.0), axis=-1, keepdims=True)
    gate = mod_ref[0, :, 5 * d:6 * d]
    blocks = []
    for c in range(TILE_ROWS):
        cols = slice(c * LANES, (c + 1) * LANES)
        e1 = _tile_lane_block(buf1.at[slot], tc, c)
        e2 = _tile_lane_block(buf2.at[slot], tc, c)
        blocks.append(x_ref[:, cols] + gate[:, cols] * (w1 * e1 + w2 * e2))
    if final_norm:
        ms = sum(jnp.sum(b * b, axis=-1, keepdims=True) for b in blocks) * (1.0 / d)
        inv = lax.rsqrt(ms + EPS)
        blocks = [b * inv * fg_ref[:, c * LANES:(c + 1) * LANES] for c, b in enumerate(blocks)]
    for c, b in enumerate(blocks):
        o_ref[:, c * LANES:(c + 1) * LANES] = b


def _combine(x, mod3, route, ys, dest1, dest2, seg_len, *, row0=0, n_rows=None, final_g=None):
    t, d = x.shape
    n_rows = t if n_rows is None else n_rows
    tc = 512
    tps = seg_len // tc
    tile0 = row0 // tc
    final_norm = final_g is not None
    in_specs = [
        pl.BlockSpec((tc, d), lambda i, a, b: (tile0 + i, 0)),
        pl.BlockSpec((1, 1, 6 * d), lambda i, a, b: ((tile0 + i) // tps, 0, 0)),
        pl.BlockSpec((tc, LANES), lambda i, a, b: (tile0 + i, 0)),
    ]
    args = [dest1, dest2, x, mod3, route]
    if final_norm:
        in_specs.append(pl.BlockSpec((1, d), lambda i, a, b: (0, 0)))
        args.append(final_g.reshape(1, d))
    in_specs.append(pl.BlockSpec(memory_space=pl.ANY))
    args.append(ys)
    return pl.pallas_call(
        functools.partial(_combine_kernel, d=d, tile0=tile0, final_norm=final_norm),
        grid_spec=pltpu.PrefetchScalarGridSpec(
            num_scalar_prefetch=2,
            grid=(n_rows // tc,),
            in_specs=in_specs,
            out_specs=pl.BlockSpec((tc, d), lambda i, a, b: (i, 0)),
            scratch_shapes=[pltpu.VMEM((2, tc * TILE_ROWS, LANES), F32), pltpu.VMEM((2, tc * TILE_ROWS, LANES), F32),
                            pltpu.SemaphoreType.DMA((2, 2))],
        ),
        out_shape=jax.ShapeDtypeStruct((n_rows, d), F32),
        compiler_params=_cparams(("arbitrary",)),
        name="moe_combine",
    )(*args)


def _moe_ffn(x, mod3, g, rw_pad, w1, w3, w2, layer, seg_len, final_g=None):
    t, d = x.shape
    h, route, counts = _route(x, mod3, g, rw_pad, seg_len)
    cnt = counts[0, :N_EXPERTS].astype(jnp.int32)
    padded = (cnt + MOE_TILE - 1) // MOE_TILE * MOE_TILE
    ends = jnp.cumsum(padded)
    offs = ends - padded
    starts = jnp.cumsum(cnt) - cnt
    e1 = route[:, ROUTE_E1].astype(jnp.int32)
    e2 = route[:, ROUTE_E2].astype(jnp.int32)
    r1 = route[:, ROUTE_R1].astype(jnp.int32)
    r2 = route[:, ROUTE_R2].astype(jnp.int32)
    pos1, pos2 = starts[e1] + r1, starts[e2] + r2
    dest1, dest2 = offs[e1] + r1, offs[e2] + r2
    n_tiles = (2 * t) // MOE_TILE + N_EXPERTS
    tile_start = jnp.arange(n_tiles, dtype=jnp.int32) * MOE_TILE
    tile_expert = jnp.minimum(jnp.sum((tile_start[:, None] >= ends[None, :]).astype(jnp.int32), axis=1), N_EXPERTS - 1)
    tile_cs = starts[tile_expert] + (tile_start - offs[tile_expert])
    n_valid = (ends[-1:] // MOE_TILE).astype(jnp.int32)
    xs = _gather_sorted(h, pos1, pos2, tile_cs, n_valid)
    u = _expert_up(xs, tile_expert, n_valid, w1, w3, layer)
    ys = _expert_down(u, tile_expert, n_valid, w2, layer)
    if final_g is None:
        return _combine(x, mod3, route, ys, dest1, dest2, seg_len)
    return (_combine(x, mod3, route, ys, dest1, dest2, seg_len, row0=0, n_rows=seg_len, final_g=final_g),
            _combine(x, mod3, route, ys, dest1, dest2, seg_len, row0=seg_len, n_rows=t - seg_len, final_g=final_g))


def _final_norm_kernel(x_ref, g_ref, o_ref):
    x = x_ref[...]
    ms = jnp.mean(x * x, axis=-1, keepdims=True)
    o_ref[...] = x * lax.rsqrt(ms + EPS) * g_ref[...]


def _final_norm(x, g, *, row0, n_rows):
    d = x.shape[1]
    tm = 512
    blk0 = row0 // tm
    return pl.pallas_call(
        _final_norm_kernel,
        grid=(n_rows // tm,),
        in_specs=[pl.BlockSpec((tm, d), lambda i: (blk0 + i, 0)), pl.BlockSpec((1, d), lambda i: (0, 0))],
        out_specs=pl.BlockSpec((tm, d), lambda i: (i, 0)),
        out_shape=jax.ShapeDtypeStruct((n_rows, d), F32),
        compiler_params=_cparams(("parallel",)),
        name="final_norm",
    )(x, g.reshape(1, d))


def _rope_tables(n_tokens):
    pos = jnp.arange(n_tokens, dtype=jnp.int32)
    rowcol = jnp.stack([pos // GRID_W, pos % GRID_W], axis=-1).astype(F32)
    inv_freq = 1.0 / (ROPE_THETA ** (jnp.arange(ROPE_FREQS, dtype=F32) / ROPE_FREQS))
    ang = rowcol[:, :, None] * inv_freq
    cos = jnp.cos(ang)
    sin = jnp.sin(ang)
    cos_h = jnp.stack([cos, cos], axis=2).reshape(n_tokens, HEAD_DIM)
    sin_h = jnp.stack([-sin, sin], axis=2).reshape(n_tokens, HEAD_DIM)
    cos_t = jnp.tile(cos_h, (1, LANES // HEAD_DIM))
    sin_t = jnp.tile(sin_h, (1, LANES // HEAD_DIM))
    return (jnp.stack([jnp.ones_like(cos_t), cos_t]), jnp.stack([jnp.zeros_like(sin_t), sin_t]))


def _dft_matrices(n):
    n1 = 1 << (int(math.log2(n)) // 2)
    assert n % n1 == 0
    n2 = n // n1
    m = jnp.arange(n, dtype=jnp.int32)[None, :]
    ang_a = ((jnp.arange(n1, dtype=jnp.int32)[:, None] * m) % n1).astype(F32) * (2.0 * math.pi / n1)
    ang_b = ((jnp.arange(n2, dtype=jnp.int32)[:, None] * m) % n).astype(F32) * (2.0 * math.pi / n)
    ca, sa = jnp.cos(ang_a)[:, None, :], jnp.sin(ang_a)[:, None, :]
    cb, sb = jnp.cos(ang_b)[None, :, :], jnp.sin(ang_b)[None, :, :]
    scale = n ** -0.5
    cos_m = ((ca * cb - sa * sb) * scale).reshape(n, n)
    nsin_m = ((sa * cb + ca * sb) * -scale).reshape(n, n)
    return cos_m.astype(BF16), nsin_m.astype(BF16)


def _channel_dft_weight():
    n = FOURIER_GROUP_DIM
    idx = jnp.arange(n, dtype=jnp.int32)
    ang = ((idx[:, None] * idx[None, :]) % n).astype(F32) * (2.0 * math.pi / n)
    eye = jnp.eye(FOURIER_GROUPS, dtype=F32)
    cos_bd = jnp.kron(eye, jnp.cos(ang) * n ** -0.5)
    sin_bd = jnp.kron(eye, jnp.sin(ang) * n ** -0.5)
    return jnp.concatenate([cos_bd, sin_bd], axis=1).astype(BF16)


def _repack_w_in(w_in, lay):
    depth, d, _ = w_in.shape
    widths = (ATT_WIDTH, KV_WIDTH, KV_WIDTH, GLA_K_WIDTH, GLA_K_WIDTH, GLA_V_WIDTH, GLA_V_WIDTH,
              GLA_LOW_RANK, GLA_LOW_RANK, FOURIER_WIDTH, N_BRANCHES * d)
    offs = [0]
    for w in widths:
        offs.append(offs[-1] + w)
    aq, ak, av, gq, gk, gv, gr, lrf, lrb, fu, gates = [w_in[:, :, offs[i]:offs[i + 1]] for i in range(len(widths))]
    used = lay["lr"] + 2 * GLA_LOW_RANK
    pad = jnp.zeros((depth, d, lay["total"] - used), w_in.dtype)
    return jnp.concatenate([aq, gv, gr, fu, gates, gq, gk, ak, av, lrf, lrb, pad], axis=-1).astype(BF16)


def kernel(x_prompt, x_sample, cache_k, cache_v, state_gla, c, c_ctx, ada_w, ada_b, norm1_g, norm2_g, w_in, q_norm_g, k_norm_g, alpha_up, alpha_b, gla_norm_g, w_branch, w_out, ffn_w1, ffn_w3, ffn_w2, router_w, moe_w1, moe_w3, moe_w2, final_g):
    batch, seq, d = x_prompt.shape
    dec_batch, dec_seq, _ = x_sample.shape
    depth = w_in.shape[0]
    past = cache_k.shape[2]
    seg_len = batch * seq
    assert seg_len == dec_seq, "context tokens and each latent request must fill equal segments"
    assert seq % GLA_CHUNK == 0 and dec_seq % GLA_CHUNK == 0
    assert COL_GATES % d == 0
    assert d == TILE_ROWS * LANES, "MoE row DMAs move one (8, 128) f32 tile per token"
    n_seg = 1 + dec_batch
    t = n_seg * seg_len
    lay = _in_layout(d)

    w_in_p = _repack_w_in(w_in, lay)
    w_branch_b = w_branch.astype(BF16)
    w_out_b = w_out.astype(BF16)
    ffn_w1_b, ffn_w3_b, ffn_w2_b = ffn_w1.astype(BF16), ffn_w3.astype(BF16), ffn_w2.astype(BF16)
    router_pad = jnp.pad(router_w, ((0, 0), (0, 0), (0, LANES - N_EXPERTS)))
    cos_tab, sin_tab = _rope_tables(dec_seq)
    dft_cos_s, dft_nsin_s = _dft_matrices(dec_seq)
    dft_cos_p, dft_nsin_p = _dft_matrices(seq)
    w_chan = _channel_dft_weight()
    ones_bd = jnp.kron(jnp.eye(ATT_HEADS, dtype=F32), jnp.ones((HEAD_DIM, HEAD_DIM), F32)).astype(BF16)
    q_g = jnp.tile(q_norm_g, (1, ATT_HEADS)).reshape(depth, 1, ATT_WIDTH)
    k_g = jnp.tile(k_norm_g, (1, ATT_KV_HEADS)).reshape(depth, 1, KV_WIDTH)
    aup_pad = jnp.zeros((depth, 2, LANES, GLA_K_WIDTH), F32)
    aup_pad = aup_pad.at[:, 0, :GLA_LOW_RANK].set(alpha_up[:, 0])
    aup_pad = aup_pad.at[:, 1, GLA_LOW_RANK:2 * GLA_LOW_RANK].set(alpha_up[:, 1])
    ab = alpha_b.reshape(depth, 2, 1, GLA_K_WIDTH)
    s0_lat = state_gla.transpose(0, 1, 2, 5, 3, 4).reshape(dec_batch, depth, 2, GLA_DV, GLA_K_WIDTH)
    s0_all = jnp.concatenate([jnp.zeros((1,) + s0_lat.shape[1:], F32), s0_lat], axis=0)

    def ctx_keys(x):
        xb = x.astype(BF16)[:, :, :, None, :]
        return jnp.broadcast_to(xb, (dec_batch, past, ATT_KV_HEADS, 2, HEAD_DIM)).reshape(dec_batch, past, 2 * KV_WIDTH)

    def ctx_values_t(x):
        xt = x.astype(BF16).transpose(0, 2, 3, 1)
        return jnp.concatenate([xt, jnp.ones_like(xt)], axis=2).reshape(dec_batch, ATT_KV_HEADS * VT_ROWS, past)

    cond = jnp.zeros((8, d), F32).at[0].set(c_ctx).at[1:1 + dec_batch].set(c)
    mod_all = _modulation(cond, ada_w, ada_b)

    x = jnp.concatenate([x_prompt.reshape(seg_len, d), x_sample.reshape(dec_batch * dec_seq, d)], axis=0)
    new_k, new_v, new_s = [], [], []
    for l in range(depth):
        mod3 = mod_all[l, :n_seg].reshape(n_seg, 1, 6 * d)
        z = _in_projection(x, mod3, norm1_g[l], w_in_p, l, seg_len)
        qn, k_out, k2, vt = _attention_prep(z, lay, cos_tab, sin_tab, q_g[l], k_g[l], ones_bd, seg_len)
        att_p = _attention(qn, k2, vt, None, None, row0=0, n_seq=batch, seq_len=seq, tq=min(ATT_QUERY_TILE, seq))
        att_s = _attention(qn, k2, vt, ctx_keys(cache_k[:, l]), ctx_values_t(cache_v[:, l]),
                           row0=seg_len, n_seq=dec_batch, seq_len=dec_seq, tq=min(ATT_QUERY_TILE, dec_seq))
        gla_f, gla_b, s_fin_f, s_fin_b = _gla(z, lay, aup_pad[l], ab[l], s0_all[:, l],
                                              seg_len=seg_len, n_seg=n_seg, prompt_len=seq)
        xcs = _fourier_channels(z, w_chan)
        four_p = _fourier_positions(xcs, dft_cos_p, dft_nsin_p, row0=0, n_seq=batch, seq_len=seq)
        four_s = _fourier_positions(xcs, dft_cos_s, dft_nsin_s, row0=seg_len, n_seq=dec_batch, seq_len=dec_seq)
        x = _merge(att_p, att_s, gla_f, gla_b, z, four_p, four_s, x, mod3, gla_norm_g[l], w_branch_b, w_out_b, l, seg_len)
        j = l // 2
        if l % 2 == 0:
            x = _dense_ffn(x, mod3, norm2_g[l], ffn_w1_b, ffn_w3_b, ffn_w2_b, j, seg_len)
        elif l + 1 < depth:
            x = _moe_ffn(x, mod3, norm2_g[l], router_pad[j], moe_w1, moe_w3, moe_w2, j, seg_len)
        else:
            y_p, y_s = _moe_ffn(x, mod3, norm2_g[l], router_pad[j], moe_w1, moe_w3, moe_w2, j, seg_len, final_g=final_g)
        new_k.append(k_out[:seg_len].reshape(batch, seq, ATT_KV_HEADS, HEAD_DIM))
        v_col = lay["akv"] + KV_WIDTH
        new_v.append(z[:seg_len, v_col:v_col + KV_WIDTH].astype(F32).reshape(batch, seq, ATT_KV_HEADS, HEAD_DIM))
        cps = seq // GLA_CHUNK
        sf = s_fin_f[0, cps - 1::cps]
        sb = s_fin_b[0, 0::cps]
        st = jnp.stack([sf, sb], axis=1).reshape(batch, 2, GLA_DV, GLA_HEADS, GLA_DK)
        new_s.append(st.transpose(0, 1, 3, 4, 2))
    if depth % 2 == 1:
        y_p = _final_norm(x, final_g, row0=0, n_rows=seg_len)
        y_s = _final_norm(x, final_g, row0=seg_len, n_rows=t - seg_len)
    y_prompt = y_p.reshape(batch, seq, d)
    y_sample = y_s.reshape(dec_batch, dec_seq, d)
    return (y_prompt, y_sample, jnp.stack(new_k, axis=1), jnp.stack(new_v, axis=1), jnp.stack(new_s, axis=1))
```

```python
import functools
import math

import jax
import jax.numpy as jnp
from jax import lax
from jax.experimental import pallas as pl
from jax.experimental.pallas import tpu as pltpu

F32 = jnp.float32
BF16 = jnp.bfloat16

EPS = 1e-6
ATT_HEADS = 8
ATT_KV_HEADS = 2
HEAD_DIM = 64
ATT_WIDTH = ATT_HEADS * HEAD_DIM
KV_WIDTH = ATT_KV_HEADS * HEAD_DIM
ROPE_FREQS = HEAD_DIM // 4
ROPE_THETA = 10000.0
GRID_W = 64
GLA_HEADS = 4
GLA_DK = 64
GLA_DV = 128
GLA_K_WIDTH = GLA_HEADS * GLA_DK
GLA_V_WIDTH = GLA_HEADS * GLA_DV
GLA_LOW_RANK = 16
GLA_TAU = 16.0
FOURIER_GROUPS = 4
FOURIER_GROUP_DIM = 128
FOURIER_WIDTH = FOURIER_GROUPS * FOURIER_GROUP_DIM
N_BRANCHES = 3
N_EXPERTS = 8

LANES = 128
MXU_WIDTH = 256
VMEM_LIMIT_BYTES = 56 * 1024 * 1024

COL_AQ = 0
COL_GV = 512
COL_GR = 1024
COL_FU = 1536
COL_GATES = 2048

GLA_CHUNK = 256
GLA_SAFE_LOG = 60.0


def _cparams(sem):
    return pltpu.CompilerParams(dimension_semantics=sem, vmem_limit_bytes=VMEM_LIMIT_BYTES)


def _resident(shape, index_map):
    return pl.BlockSpec(shape, index_map, pipeline_mode=pl.Buffered(1))


def _split2(x):
    hi = x.astype(BF16)
    lo = (x - hi.astype(F32)).astype(BF16)
    return hi, lo


def _split3(x):
    hi = x.astype(BF16)
    r = x - hi.astype(F32)
    mid = r.astype(BF16)
    lo = (r - mid.astype(F32)).astype(BF16)
    return hi, mid, lo


def _dot(a, b):
    return jnp.dot(a, b, preferred_element_type=F32)


def _dot_nt(a, b):
    return lax.dot_general(a, b, (((1,), (1,)), ((), ())), preferred_element_type=F32)


def _dot_tn(a, b):
    return lax.dot_general(a, b, (((0,), (0,)), ((), ())), preferred_element_type=F32)


def _dot3(a, b):
    a_hi, a_lo = _split2(a)
    b_hi, b_lo = _split2(b)
    return _dot(a_hi, b_hi) + _dot(a_lo, b_hi) + _dot(a_hi, b_lo)


def _sigmoid(x):
    return 1.0 / (1.0 + jnp.exp(-x))


def _silu(x):
    return x * _sigmoid(x)


def _in_layout(d_model):
    gates_end = COL_GATES + N_BRANCHES * d_model
    col_gq = gates_end
    col_gk = col_gq + GLA_K_WIDTH
    col_akv = col_gk + GLA_K_WIDTH
    col_lr = col_akv + 2 * KV_WIDTH
    used = col_lr + LANES
    total = -(-used // 1024) * 1024
    return dict(gq=col_gq, gk=col_gk, akv=col_akv, lr=col_lr, total=total)


def _mod_kernel(cond_ref, w_ref, b_ref, o_ref):
    c = _silu(cond_ref[...])
    o_ref[0] = _dot3(c, w_ref[0]) + b_ref[0]


def _modulation(cond, ada_w, ada_b):
    depth, d, n6 = ada_w.shape
    tn = 1536 if n6 % 1536 == 0 else n6
    rows = cond.shape[0]
    return pl.pallas_call(
        _mod_kernel,
        grid=(depth, n6 // tn),
        in_specs=[
            pl.BlockSpec((rows, d), lambda l, j: (0, 0)),
            pl.BlockSpec((1, d, tn), lambda l, j: (l, 0, j)),
            pl.BlockSpec((1, 1, tn), lambda l, j: (l, 0, j)),
        ],
        out_specs=pl.BlockSpec((1, rows, tn), lambda l, j: (l, 0, j)),
        out_shape=jax.ShapeDtypeStruct((depth, rows, n6), F32),
        compiler_params=_cparams(("parallel", "parallel")),
        name="adaln_mod",
    )(cond, ada_w, ada_b.reshape(depth, 1, n6))


def _norm_mod(x, g, shift, scale):
    ms = jnp.mean(x * x, axis=-1, keepdims=True)
    return x * lax.rsqrt(ms + EPS) * g * (1.0 + scale) + shift


def _inproj_kernel(x_ref, mod_ref, g_ref, w_ref, z_ref, h_scr, *, d):
    @pl.when(pl.program_id(1) == 0)
    def _():
        h = _norm_mod(x_ref[...], g_ref[...], mod_ref[0, :, 0:d], mod_ref[0, :, d:2 * d])
        h_scr[...] = h.astype(BF16)

    z_ref[...] = _dot(h_scr[...], w_ref[...]).astype(z_ref.dtype)


def _in_projection(x, mod3, g, w, layer, seg_len):
    t, d = x.shape
    n = w.shape[2]
    tm = min(1024, seg_len)
    tn = 3072
    tps = seg_len // tm
    return pl.pallas_call(
        functools.partial(_inproj_kernel, d=d),
        grid=(t // tm, n // tn),
        in_specs=[
            pl.BlockSpec((tm, d), lambda i, j: (i, 0)),
            pl.BlockSpec((1, 1, 6 * d), lambda i, j: (i // tps, 0, 0)),
            pl.BlockSpec((1, d), lambda i, j: (0, 0)),
            pl.BlockSpec((None, d, tn), lambda i, j: (layer, 0, j)),
        ],
        out_specs=pl.BlockSpec((tm, tn), lambda i, j: (i, j)),
        out_shape=jax.ShapeDtypeStruct((t, n), BF16),
        scratch_shapes=[pltpu.VMEM((tm, d), BF16)],
        compiler_params=_cparams(("parallel", "arbitrary")),
        name="in_projection",
    )(x, mod3, g.reshape(1, d), w)


def _head_rms(x, g, ones_bd):
    hi, lo = _split2(x * x)
    ss = _dot(hi, ones_bd) + _dot(lo, ones_bd)
    return x * lax.rsqrt(ss * (1.0 / HEAD_DIM) + EPS) * g


def _rope(x, cos, sin_signed):
    width = x.shape[1]
    reps = width // LANES
    if reps > 1:
        cos = jnp.concatenate([cos] * reps, axis=1)
        sin_signed = jnp.concatenate([sin_signed] * reps, axis=1)
    lane = lax.broadcasted_iota(jnp.int32, x.shape, 1)
    first_half = (lane % (2 * ROPE_FREQS)) < ROPE_FREQS
    partner = jnp.where(first_half, pltpu.roll(x, width - ROPE_FREQS, 1), pltpu.roll(x, ROPE_FREQS, 1))
    return x * cos + partner * sin_signed


ATT_KEY_CHUNK = 512
ATT_QUERY_TILE = 512
VT_ROWS = 2 * HEAD_DIM
VT_USED = HEAD_DIM + 16


def _dup_heads(x):
    lane = lax.broadcasted_iota(jnp.int32, x.shape, 1)
    swapped = pltpu.roll(x, HEAD_DIM, 1)
    low = lane < HEAD_DIM
    return jnp.concatenate([jnp.where(low, x, swapped), jnp.where(low, swapped, x)], axis=1)


def _attn_prep_kernel(zq_ref, zkv_ref, cos_ref, sin_ref, qg_ref, kg_ref, bd_ref, q_ref, kout_ref, k2_ref, vt_ref):
    cos = cos_ref[0]
    sin = sin_ref[0]
    bd = bd_ref[...]
    q = _head_rms(zq_ref[...].astype(F32), qg_ref[...], bd)
    q_ref[...] = (_rope(q, cos, sin) * (HEAD_DIM ** -0.5)).astype(BF16)
    kv = zkv_ref[...].astype(F32)
    k = _head_rms(kv[:, :KV_WIDTH], kg_ref[...], bd[:KV_WIDTH, :KV_WIDTH])
    kout_ref[...] = k
    k2_ref[...] = _dup_heads(_rope(k, cos, sin)).astype(BF16)
    vt = kv[:, KV_WIDTH:].T
    ones = jnp.ones((HEAD_DIM, vt.shape[1]), F32)
    vt_ref[...] = jnp.concatenate([vt[:HEAD_DIM], ones, vt[HEAD_DIM:], ones], axis=0).astype(BF16)


def _attention_prep(z, lay, cos_tab, sin_tab, q_g, k_g, ones_bd, seg_len):
    t = z.shape[0]
    tm = min(512, seg_len)
    tps = seg_len // tm
    akv_blk = lay["akv"] // (2 * KV_WIDTH)
    tab_spec = pl.BlockSpec((1, tm, LANES), lambda i: (jnp.minimum(i // tps, 1), i % tps, 0))
    return pl.pallas_call(
        _attn_prep_kernel,
        grid=(t // tm,),
        in_specs=[
            pl.BlockSpec((tm, ATT_WIDTH), lambda i: (i, COL_AQ // ATT_WIDTH)),
            pl.BlockSpec((tm, 2 * KV_WIDTH), lambda i: (i, akv_blk)),
            tab_spec,
            tab_spec,
            pl.BlockSpec((1, ATT_WIDTH), lambda i: (0, 0)),
            pl.BlockSpec((1, KV_WIDTH), lambda i: (0, 0)),
            pl.BlockSpec((ATT_WIDTH, ATT_WIDTH), lambda i: (0, 0)),
        ],
        out_specs=[
            pl.BlockSpec((tm, ATT_WIDTH), lambda i: (i, 0)),
            pl.BlockSpec((tm, KV_WIDTH), lambda i: (i, 0)),
            pl.BlockSpec((tm, 2 * KV_WIDTH), lambda i: (i, 0)),
            pl.BlockSpec((ATT_KV_HEADS * VT_ROWS, tm), lambda i: (0, i)),
        ],
        out_shape=[
            jax.ShapeDtypeStruct((t, ATT_WIDTH), BF16),
            jax.ShapeDtypeStruct((t, KV_WIDTH), F32),
            jax.ShapeDtypeStruct((t, 2 * KV_WIDTH), BF16),
            jax.ShapeDtypeStruct((ATT_KV_HEADS * VT_ROWS, t), BF16),
        ],
        compiler_params=_cparams(("parallel",)),
        name="attention_prep",
    )(z, z, cos_tab, sin_tab, q_g, k_g, ones_bd)


def _attn_kernel(*refs, has_ctx):
    if has_ctx:
        q_ref, k_ref, v_ref, ck_ref, cv_ref, o_ref = refs
    else:
        q_ref, k_ref, v_ref, o_ref = refs
    tq = q_ref.shape[0]
    sources = [(k_ref, v_ref)] + ([(ck_ref.at[0], cv_ref.at[0])] if has_ctx else [])
    chunks = []
    for kr, vr in sources:
        n = min(ATT_KEY_CHUNK, kr.shape[0])
        chunks += [(kr, vr, c * n, n) for c in range(kr.shape[0] // n)]
    low = lax.broadcasted_iota(jnp.int32, (tq, LANES), 1) < HEAD_DIM
    zero = jnp.zeros((tq, LANES), BF16)
    n_heads = ATT_HEADS // ATT_KV_HEADS
    ws = []
    for h in range(n_heads):
        q2 = q_ref[:, (h // 2) * LANES:(h // 2 + 1) * LANES]
        ws.append(jnp.where(low, q2, zero) if h % 2 == 0 else jnp.where(low, zero, q2))
    w = jnp.concatenate(ws, axis=0)
    maxes, parts = [], []
    scores = lambda c: _dot_nt(c[0][c[2]:c[2] + c[3], :], w)
    st_next = scores(chunks[0])
    for idx, (kr, vr, r0, n) in enumerate(chunks):
        st = st_next
        if idx + 1 < len(chunks):
            st_next = scores(chunks[idx + 1])
        mc = jnp.max(st, axis=0, keepdims=True)
        p = jnp.exp(st - mc).astype(BF16)
        maxes.append(mc)
        parts.append(_dot(vr[0:VT_USED, r0:r0 + n], p))
    m = functools.reduce(jnp.maximum, maxes)
    acc = sum(jnp.exp(mc - m) * part for mc, part in zip(maxes, parts))
    o_t = acc[:HEAD_DIM] / acc[HEAD_DIM:HEAD_DIM + 1]
    o_heads = jnp.concatenate([o_t[:, h * tq:(h + 1) * tq] for h in range(n_heads)], axis=0)
    o_ref[...] = o_heads.T.astype(BF16)


def _attention(q, k2, v2, ctx_k2, ctx_v2, *, row0, n_seq, seq_len, tq):
    ntq = seq_len // tq
    gw = ATT_WIDTH // ATT_KV_HEADS
    qblk0 = row0 // tq
    kblk0 = row0 // seq_len
    has_ctx = ctx_k2 is not None
    in_specs = [
        pl.BlockSpec((tq, gw), lambda b, g, i: (qblk0 + b * ntq + i, g)),
        pl.BlockSpec((seq_len, LANES), lambda b, g, i: (kblk0 + b, g)),
        pl.BlockSpec((VT_ROWS, seq_len), lambda b, g, i: (g, kblk0 + b)),
    ]
    args = [q, k2, v2]
    if has_ctx:
        past = ctx_k2.shape[1]
        in_specs += [pl.BlockSpec((1, past, LANES), lambda b, g, i: (b, 0, g)),
                     pl.BlockSpec((1, VT_ROWS, past), lambda b, g, i: (b, g, 0))]
        args += [ctx_k2, ctx_v2]
    return pl.pallas_call(
        functools.partial(_attn_kernel, has_ctx=has_ctx),
        grid=(n_seq, ATT_KV_HEADS, ntq),
        in_specs=in_specs,
        out_specs=pl.BlockSpec((tq, gw), lambda b, g, i: (b * ntq + i, g)),
        out_shape=jax.ShapeDtypeStruct((n_seq * seq_len, ATT_WIDTH), BF16),
        compiler_params=_cparams(("parallel", "parallel", "parallel")),
        name="attention_ctx" if has_ctx else "attention_self",
    )(*args)


def _gla_prepare(lr_ref, aup, ab, fwd):
    c = lr_ref.shape[0]
    aup_hi, aup_lo = _split2(aup)
    lr = lr_ref[...]
    x = _dot(lr, aup_hi) + _dot(lr, aup_lo) + ab
    la = (jnp.minimum(x, 0.0) - jnp.log(1.0 + jnp.exp(-jnp.abs(x)))) * (1.0 / GLA_TAU)
    row = lax.broadcasted_iota(jnp.int32, (c, c), 0)
    col = lax.broadcasted_iota(jnp.int32, (c, c), 1)
    tri = (row >= col) if fwd else (row <= col)
    tri_b = jnp.where(tri, 1.0, 0.0).astype(BF16)
    la_hi, la_lo = _split2(la)
    cum = _dot(tri_b, la_hi) + _dot(tri_b, la_lo)
    bend = jnp.sum(la, axis=0, keepdims=True)
    return la, cum, bend, tri, jnp.min(cum) >= -GLA_SAFE_LOG


def _gla_fast(q_ref, k_ref, v_ref, o_ref, sfin_ref, s_ref, cum, bend, tri):
    head_of_lane = lax.broadcasted_iota(jnp.int32, (1, GLA_K_WIDTH), 1) // GLA_DK
    q = q_ref[...].astype(F32) * (GLA_DK ** -0.5)
    k = k_ref[...].astype(F32)
    vb = v_ref[...]
    qt = (q * jnp.exp(cum)).astype(BF16)
    kt = (k * jnp.exp(-cum)).astype(BF16)
    ke = (k * jnp.exp(bend - cum)).astype(BF16)
    s_old = s_ref[...]
    s_b = s_old.astype(BF16)
    s_new = jnp.exp(bend) * s_old
    zero = jnp.zeros_like(qt)
    for h in range(GLA_HEADS):
        hm = head_of_lane == h
        qh = jnp.where(hm, qt, zero)
        a = jnp.where(tri, _dot_nt(qh, kt), 0.0).astype(BF16)
        vh = vb[:, h * GLA_DV:(h + 1) * GLA_DV]
        o_ref[:, h * GLA_DV:(h + 1) * GLA_DV] = _dot(a, vh) + _dot_nt(qh, s_b)
        s_new = s_new + _dot_tn(vh, jnp.where(hm, ke, zero))
    s_ref[...] = s_new
    sfin_ref[0, 0] = s_new


def _gla_slow(q_ref, k_ref, v_ref, o_ref, sfin_ref, s_ref, la, fwd, la_scr, qf_scr, kf_scr, vt_scr, ot_scr):
    c = q_ref.shape[0]
    head_of_lane = lax.broadcasted_iota(jnp.int32, (1, GLA_K_WIDTH), 1) // GLA_DK
    la_scr[...] = la
    qf_scr[...] = q_ref[...].astype(F32) * (GLA_DK ** -0.5)
    kf_scr[...] = k_ref[...].astype(F32)
    vt_scr[...] = v_ref[...].astype(F32).T
    ot_scr[...] = jnp.zeros_like(ot_scr)
    lane_c = lax.broadcasted_iota(jnp.int32, (1, c), 1)

    def body(i, s):
        t = i if fwd else c - 1 - i
        a_t = jnp.exp(la_scr[pl.ds(t, 1), :])
        k_t = kf_scr[pl.ds(t, 1), :]
        q_t = qf_scr[pl.ds(t, 1), :]
        onehot = lane_c == t
        vmat = jnp.zeros((GLA_DV, GLA_K_WIDTH), F32)
        for h in range(GLA_HEADS):
            vcol = jnp.sum(jnp.where(onehot, vt_scr[h * GLA_DV:(h + 1) * GLA_DV, :], 0.0), axis=1, keepdims=True)
            vmat = jnp.where(head_of_lane == h, vcol, vmat)
        s = a_t * s + vmat * k_t
        prod = s * q_t
        for h in range(GLA_HEADS):
            ocol = jnp.sum(jnp.where(head_of_lane == h, prod, 0.0), axis=1, keepdims=True)
            rows = slice(h * GLA_DV, (h + 1) * GLA_DV)
            ot_scr[rows, :] = jnp.where(onehot, ocol, ot_scr[rows, :])
        return s

    s_new = lax.fori_loop(0, c, body, s_ref[...])
    o_ref[...] = ot_scr[...].T
    s_ref[...] = s_new
    sfin_ref[0, 0] = s_new


def _gla_kernel(qf_ref, kf_ref, vf_ref, lrf_ref, qb_ref, kb_ref, vb_ref, lrb_ref, aup_ref, ab_ref, s0_ref,
                of_ref, ob_ref, sff_ref, sfb_ref, s_scr, *slow_scr, chunks_per_seq):
    @pl.when(pl.program_id(1) % chunks_per_seq == 0)
    def _():
        s_scr[...] = s0_ref[0]

    dirs = ((True, qf_ref, kf_ref, vf_ref, lrf_ref, of_ref, sff_ref), (False, qb_ref, kb_ref, vb_ref, lrb_ref, ob_ref, sfb_ref))
    prep = [_gla_prepare(d[4], aup_ref[i], ab_ref[i], d[0]) for i, d in enumerate(dirs)]
    both_safe = prep[0][4] & prep[1][4]

    def fast(i):
        _, q_ref, k_ref, v_ref, _, o_ref, sfin_ref = dirs[i]
        _, cum, bend, tri, _ = prep[i]
        _gla_fast(q_ref, k_ref, v_ref, o_ref, sfin_ref, s_scr.at[i], cum, bend, tri)

    @pl.when(both_safe)
    def _():
        fast(0)
        fast(1)

    @pl.when(jnp.logical_not(both_safe))
    def _():
        for i, (fwd, q_ref, k_ref, v_ref, _, o_ref, sfin_ref) in enumerate(dirs):
            pl.when(prep[i][4])(functools.partial(fast, i))
            slow = functools.partial(_gla_slow, q_ref, k_ref, v_ref, o_ref, sfin_ref, s_scr.at[i], prep[i][0], fwd, *slow_scr)
            pl.when(jnp.logical_not(prep[i][4]))(slow)


def _gla(z, lay, aup_pad, ab, s0, *, seg_len, n_seg, prompt_len):
    t = z.shape[0]
    c = GLA_CHUNK
    nc = seg_len // c
    cps_prompt = prompt_len // c
    gq_blk = lay["gq"] // GLA_K_WIDTH
    gk_blk = lay["gk"] // GLA_K_WIDTH
    gv_blk = COL_GV // GLA_V_WIDTH
    lr_blk = lay["lr"] // LANES

    fwd_chunk = lambda s, n: s * nc + n
    bwd_chunk = lambda s, n: s * nc + nc - 1 - n

    def kernel(*refs):
        cps = jnp.where(pl.program_id(0) == 0, cps_prompt, nc)
        _gla_kernel(*refs, chunks_per_seq=cps)

    def operand_specs(chunk):
        return [
            pl.BlockSpec((c, GLA_K_WIDTH), lambda s, n: (chunk(s, n), gq_blk)),
            pl.BlockSpec((c, GLA_K_WIDTH), lambda s, n: (chunk(s, n), gk_blk)),
            pl.BlockSpec((c, GLA_V_WIDTH), lambda s, n: (chunk(s, n), gv_blk)),
            pl.BlockSpec((c, LANES), lambda s, n: (chunk(s, n), lr_blk)),
        ]

    return pl.pallas_call(
        kernel,
        grid=(n_seg, nc),
        in_specs=operand_specs(fwd_chunk) + operand_specs(bwd_chunk) + [
            pl.BlockSpec((2, LANES, GLA_K_WIDTH), lambda s, n: (0, 0, 0)),
            pl.BlockSpec((2, 1, GLA_K_WIDTH), lambda s, n: (0, 0, 0)),
            pl.BlockSpec((1, 2, GLA_DV, GLA_K_WIDTH), lambda s, n: (s, 0, 0, 0)),
        ],
        out_specs=[
            pl.BlockSpec((c, GLA_V_WIDTH), lambda s, n: (fwd_chunk(s, n), 0)),
            pl.BlockSpec((c, GLA_V_WIDTH), lambda s, n: (bwd_chunk(s, n), 0)),
            pl.BlockSpec((1, 1, GLA_DV, GLA_K_WIDTH), lambda s, n: (s, n, 0, 0)),
            pl.BlockSpec((1, 1, GLA_DV, GLA_K_WIDTH), lambda s, n: (s, nc - 1 - n, 0, 0)),
        ],
        out_shape=[
            jax.ShapeDtypeStruct((t, GLA_V_WIDTH), F32),
            jax.ShapeDtypeStruct((t, GLA_V_WIDTH), F32),
            jax.ShapeDtypeStruct((n_seg, nc, GLA_DV, GLA_K_WIDTH), F32),
            jax.ShapeDtypeStruct((n_seg, nc, GLA_DV, GLA_K_WIDTH), F32),
        ],
        scratch_shapes=[
            pltpu.VMEM((2, GLA_DV, GLA_K_WIDTH), F32),
            pltpu.VMEM((c, GLA_K_WIDTH), F32),
            pltpu.VMEM((c, GLA_K_WIDTH), F32),
            pltpu.VMEM((c, GLA_K_WIDTH), F32),
            pltpu.VMEM((GLA_V_WIDTH, c), F32),
            pltpu.VMEM((GLA_V_WIDTH, c), F32),
        ],
        compiler_params=_cparams(("parallel", "arbitrary")),
        name="gla_scan",
    )(z, z, z, z, z, z, z, z, aup_pad, ab, s0)


def _four_chan_kernel(u_ref, w_ref, o_ref):
    o_ref[...] = _dot(u_ref[...], w_ref[...]).astype(BF16)


def _fourier_channels(z, w_chan):
    t = z.shape[0]
    tm = 512
    return pl.pallas_call(
        _four_chan_kernel,
        grid=(t // tm,),
        in_specs=[
            pl.BlockSpec((tm, FOURIER_WIDTH), lambda i: (i, COL_FU // FOURIER_WIDTH)),
            _resident((FOURIER_WIDTH, 2 * FOURIER_WIDTH), lambda i: (0, 0)),
        ],
        out_specs=pl.BlockSpec((tm, 2 * FOURIER_WIDTH), lambda i: (i, 0)),
        out_shape=jax.ShapeDtypeStruct((t, 2 * FOURIER_WIDTH), BF16),
        compiler_params=_cparams(("parallel",)),
        name="fourier_channels",
    )(z, w_chan)


def _four_pos_kernel(c_ref, s_ref, xc_ref, xs_ref, o_ref):
    o_ref[...] = (_dot(c_ref[...], xc_ref[...]) + _dot(s_ref[...], xs_ref[...])).astype(BF16)


def _fourier_positions(xcs, cos_m, nsin_m, *, row0, n_seq, seq_len):
    tm = min(512, seq_len)
    nt = seq_len // tm
    blk0 = row0 // seq_len
    return pl.pallas_call(
        _four_pos_kernel,
        grid=(n_seq, nt),
        in_specs=[
            pl.BlockSpec((tm, seq_len), lambda b, i: (i, 0)),
            pl.BlockSpec((tm, seq_len), lambda b, i: (i, 0)),
            pl.BlockSpec((seq_len, FOURIER_WIDTH), lambda b, i: (blk0 + b, 0)),
            pl.BlockSpec((seq_len, FOURIER_WIDTH), lambda b, i: (blk0 + b, 1)),
        ],
        out_specs=pl.BlockSpec((tm, FOURIER_WIDTH), lambda b, i: (b * nt + i, 0)),
        out_shape=jax.ShapeDtypeStruct((n_seq * seq_len, FOURIER_WIDTH), BF16),
        compiler_params=_cparams(("parallel", "parallel")),
        name="fourier_positions",
    )(cos_m, nsin_m, xcs, xcs)


def _merge_kernel(attp_ref, atts_ref, gof_ref, gob_ref, gr_ref, fourp_ref, fours_ref, g0_ref, g1_ref, g2_ref, x_ref,
                  mod_ref, gg_ref, wb_ref, wo_ref, o_ref, *, d, prompt_tiles):
    in_prompt = pl.program_id(0) < prompt_tiles
    att = jnp.where(in_prompt, attp_ref[...], atts_ref[...])
    four = jnp.where(in_prompt, fourp_ref[...], fours_ref[...])
    o = gof_ref[...] + gob_ref[...]
    gg = gg_ref[...]
    parts = []
    for h in range(GLA_HEADS):
        oh = o[:, h * GLA_DV:(h + 1) * GLA_DV]
        ms = jnp.mean(oh * oh, axis=-1, keepdims=True)
        parts.append(oh * lax.rsqrt(ms + EPS) * gg)
    gla = (jnp.concatenate(parts, axis=1) * _silu(gr_ref[...].astype(F32))).astype(BF16)
    merged = (_sigmoid(g0_ref[...].astype(F32)) * _dot(att, wb_ref[0])
              + _sigmoid(g1_ref[...].astype(F32)) * _dot(gla, wb_ref[1])
              + _sigmoid(g2_ref[...].astype(F32)) * _dot(four, wb_ref[2]))
    mix = _dot(merged.astype(BF16), wo_ref[...])
    o_ref[...] = x_ref[...] + mod_ref[0, :, 2 * d:3 * d] * mix


def _merge(att_p, att_s, gla_f, gla_b, z, four_p, four_s, x, mod3, gla_g, w_branch, w_out, layer, seg_len):
    t, d = x.shape
    tm = min(512, seg_len)
    tps = seg_len // tm
    bw = att_p.shape[1]
    gate_blk = COL_GATES // d
    prompt_tiles = att_p.shape[0] // tm
    p_spec = pl.BlockSpec((tm, bw), lambda i: (jnp.minimum(i, prompt_tiles - 1), 0))
    s_spec = pl.BlockSpec((tm, bw), lambda i: (jnp.maximum(i - prompt_tiles, 0), 0))
    return pl.pallas_call(
        functools.partial(_merge_kernel, d=d, prompt_tiles=prompt_tiles),
        grid=(t // tm,),
        in_specs=[
            p_spec,
            s_spec,
            pl.BlockSpec((tm, GLA_V_WIDTH), lambda i: (i, 0)),
            pl.BlockSpec((tm, GLA_V_WIDTH), lambda i: (i, 0)),
            pl.BlockSpec((tm, GLA_V_WIDTH), lambda i: (i, COL_GR // GLA_V_WIDTH)),
            p_spec,
            s_spec,
            pl.BlockSpec((tm, d), lambda i: (i, gate_blk)),
            pl.BlockSpec((tm, d), lambda i: (i, gate_blk + 1)),
            pl.BlockSpec((tm, d), lambda i: (i, gate_blk + 2)),
            pl.BlockSpec((tm, d), lambda i: (i, 0)),
            pl.BlockSpec((1, 1, 6 * d), lambda i: (i // tps, 0, 0)),
            pl.BlockSpec((1, GLA_DV), lambda i: (0, 0)),
            _resident((None, N_BRANCHES, bw, d), lambda i: (layer, 0, 0, 0)),
            _resident((None, d, d), lambda i: (layer, 0, 0)),
        ],
        out_specs=pl.BlockSpec((tm, d), lambda i: (i, 0)),
        out_shape=jax.ShapeDtypeStruct((t, d), F32),
        compiler_params=_cparams(("parallel",)),
        name="branch_merge",
    )(att_p, att_s, gla_f, gla_b, z, four_p, four_s, z, z, z, x, mod3, gla_g.reshape(1, GLA_DV), w_branch, w_out)


def _ffn_kernel(x_ref, mod_ref, g_ref, w1_ref, w3_ref, w2_ref, o_ref, *, d):
    x = x_ref[...]
    h = _norm_mod(x, g_ref[...], mod_ref[0, :, 3 * d:4 * d], mod_ref[0, :, 4 * d:5 * d]).astype(BF16)
    a = _dot(h, w1_ref[...])
    u = (_silu(a) * _dot(h, w3_ref[...])).astype(BF16)
    o_ref[...] = x + mod_ref[0, :, 5 * d:6 * d] * _dot(u, w2_ref[...])


def _dense_ffn(x, mod3, g, w1, w3, w2, layer, seg_len):
    t, d = x.shape
    dff = w1.shape[2]
    tm = min(512, seg_len)
    tps = seg_len // tm
    return pl.pallas_call(
        functools.partial(_ffn_kernel, d=d),
        grid=(t // tm,),
        in_specs=[
            pl.BlockSpec((tm, d), lambda i: (i, 0)),
            pl.BlockSpec((1, 1, 6 * d), lambda i: (i // tps, 0, 0)),
            pl.BlockSpec((1, d), lambda i: (0, 0)),
            _resident((None, d, dff), lambda i: (layer, 0, 0)),
            _resident((None, d, dff), lambda i: (layer, 0, 0)),
            _resident((None, dff, d), lambda i: (layer, 0, 0)),
        ],
        out_specs=pl.BlockSpec((tm, d), lambda i: (i, 0)),
        out_shape=jax.ShapeDtypeStruct((t, d), F32),
        compiler_params=_cparams(("parallel",)),
        name="dense_ffn",
    )(x, mod3, g.reshape(1, d), w1, w3, w2)


MOE_TILE = 512
MOE_UP_CHUNK = 1792
ROUTE_E1, ROUTE_E2, ROUTE_R1, ROUTE_R2, ROUTE_W1, ROUTE_W2 = range(6)


def _route_kernel(x_ref, mod_ref, g_ref, rw_ref, h_ref, route_ref, count_ref, carry_scr, *, d):
    @pl.when(pl.program_id(0) == 0)
    def _():
        carry_scr[...] = jnp.zeros_like(carry_scr)

    h = _norm_mod(x_ref[...], g_ref[...], mod_ref[0, :, 3 * d:4 * d], mod_ref[0, :, 4 * d:5 * d])
    _store_as_tiles(h_ref, h)
    logits = _dot3(h, rw_ref[...])
    tm = logits.shape[0]
    lane = lax.broadcasted_iota(jnp.int32, logits.shape, 1).astype(F32)
    neg = jnp.float32(-jnp.inf)
    l1 = jnp.where(lane < N_EXPERTS, logits, neg)
    m1 = jnp.max(l1, axis=-1, keepdims=True)
    i1 = jnp.min(jnp.where(l1 == m1, lane, float(LANES)), axis=-1, keepdims=True)
    l2 = jnp.where(lane == i1, neg, l1)
    m2 = jnp.max(l2, axis=-1, keepdims=True)
    i2 = jnp.min(jnp.where(l2 == m2, lane, float(LANES)), axis=-1, keepdims=True)
    e2 = jnp.exp(m2 - m1)
    w1 = 1.0 / (1.0 + e2)
    w2 = e2 * w1
    onehot = jnp.where((lane == i1) | (lane == i2), 1.0, 0.0)
    row = lax.broadcasted_iota(jnp.int32, (tm, tm), 0)
    col = lax.broadcasted_iota(jnp.int32, (tm, tm), 1)
    before = jnp.where(row > col, 1.0, 0.0).astype(BF16)
    rank = _dot(before, onehot.astype(BF16)) + carry_scr[...]
    r1 = jnp.sum(jnp.where(lane == i1, rank, 0.0), axis=-1, keepdims=True)
    r2 = jnp.sum(jnp.where(lane == i2, rank, 0.0), axis=-1, keepdims=True)
    rec = jnp.zeros_like(logits)
    for idx, val in ((ROUTE_E1, i1), (ROUTE_E2, i2), (ROUTE_R1, r1), (ROUTE_R2, r2), (ROUTE_W1, w1), (ROUTE_W2, w2)):
        rec = jnp.where(lane == idx, val, rec)
    route_ref[...] = rec
    total = carry_scr[...] + jnp.sum(onehot, axis=0, keepdims=True)
    carry_scr[...] = total
    count_ref[...] = total


def _route(x, mod3, g, rw_pad, seg_len):
    t, d = x.shape
    tm = min(512, seg_len)
    tps = seg_len // tm
    return pl.pallas_call(
        functools.partial(_route_kernel, d=d),
        grid=(t // tm,),
        in_specs=[
            pl.BlockSpec((tm, d), lambda i: (i, 0)),
            pl.BlockSpec((1, 1, 6 * d), lambda i: (i // tps, 0, 0)),
            pl.BlockSpec((1, d), lambda i: (0, 0)),
            pl.BlockSpec((d, LANES), lambda i: (0, 0)),
        ],
        out_specs=[
            pl.BlockSpec((tm * TILE_ROWS, LANES), lambda i: (i, 0)),
            pl.BlockSpec((tm, LANES), lambda i: (i, 0)),
            pl.BlockSpec((1, LANES), lambda i: (0, 0)),
        ],
        out_shape=[
            jax.ShapeDtypeStruct((t * TILE_ROWS, LANES), F32),
            jax.ShapeDtypeStruct((t, LANES), F32),
            jax.ShapeDtypeStruct((1, LANES), F32),
        ],
        scratch_shapes=[pltpu.VMEM((1, LANES), F32)],
        compiler_params=_cparams(("arbitrary",)),
        name="moe_route",
    )(x, mod3, g.reshape(1, d), rw_pad)


DMA_UNROLL = 8


TILE_ROWS = 8


def _store_as_tiles(ref, x):
    rows = x.shape[0]
    for c in range(TILE_ROWS):
        ref[pl.ds(c, rows, stride=TILE_ROWS), :] = x[:, c * LANES:(c + 1) * LANES]


def _tile_lane_block(ref, rows, c):
    return ref[pl.ds(c, rows, stride=TILE_ROWS), :]


def _row_tile(ref, row):
    return ref.at[pl.ds(pl.multiple_of(row * TILE_ROWS, TILE_ROWS), TILE_ROWS), :]


def _start_row_gather(src_hbm, row_of, dst, sem, n):
    def body(i, carry):
        pltpu.make_async_copy(_row_tile(src_hbm, row_of(i)), _row_tile(dst, i), sem).start()
        return carry

    lax.fori_loop(0, n, body, 0, unroll=DMA_UNROLL)


def _wait_row_gather(src_hbm, dst, sem, n):
    def body(i, carry):
        pltpu.make_async_copy(_row_tile(src_hbm, 0), _row_tile(dst, 0), sem).wait()
        return carry

    lax.fori_loop(0, n, body, 0, unroll=DMA_UNROLL)


def _gather_kernel(p1_ref, p2_ref, cs_ref, nv_ref, h_hbm, o_ref, tok_smem, buf, sem):
    j = pl.program_id(0)
    tm = o_ref.shape[0]
    n_tok = p1_ref.shape[0]
    n_valid = nv_ref[0]

    def issue(tile):
        cs = cs_ref[tile]
        slot = tile % 2
        _start_row_gather(h_hbm, lambda i: tok_smem[jnp.minimum(cs + i, 2 * n_tok - 1)], buf.at[slot], sem.at[slot], tm)

    @pl.when(j == 0)
    def _():
        def inv(t, carry):
            tok_smem[p1_ref[t]] = t
            tok_smem[p2_ref[t]] = t
            return carry

        lax.fori_loop(0, n_tok, inv, 0, unroll=DMA_UNROLL)
        issue(0)

    @pl.when(j + 1 < n_valid)
    def _():
        issue(j + 1)

    @pl.when(j < n_valid)
    def _():
        slot = j % 2
        _wait_row_gather(h_hbm, buf.at[slot], sem.at[slot], tm)
        for c in range(TILE_ROWS):
            o_ref[:, c * LANES:(c + 1) * LANES] = _tile_lane_block(buf.at[slot], tm, c).astype(BF16)

    @pl.when(j >= n_valid)
    def _():
        o_ref[...] = jnp.zeros_like(o_ref)


def _gather_sorted(h, pos1, pos2, tile_cs, n_valid):
    t = h.shape[0] // TILE_ROWS
    d = TILE_ROWS * LANES
    nt = tile_cs.shape[0]
    return pl.pallas_call(
        _gather_kernel,
        grid_spec=pltpu.PrefetchScalarGridSpec(
            num_scalar_prefetch=4,
            grid=(nt,),
            in_specs=[pl.BlockSpec(memory_space=pl.ANY)],
            out_specs=pl.BlockSpec((MOE_TILE, d), lambda j, a, b, c, nv: (j, 0)),
            scratch_shapes=[
                pltpu.SMEM((2 * t,), jnp.int32),
                pltpu.VMEM((2, MOE_TILE * TILE_ROWS, LANES), F32),
                pltpu.SemaphoreType.DMA((2,)),
            ],
        ),
        out_shape=jax.ShapeDtypeStruct((nt * MOE_TILE, d), BF16),
        compiler_params=_cparams(("arbitrary",)),
        name="moe_gather",
    )(pos1, pos2, tile_cs, n_valid, h)


def _expert_changed(te_ref, j):
    prev = te_ref[jnp.maximum(j - 1, 0)]
    return (j == 0) | (te_ref[j] != prev)


def _expert_up_kernel(te_ref, nv_ref, xs_ref, w1_ref, w3_ref, u_ref, w1b_scr, w3b_scr):
    j = pl.program_id(1)

    @pl.when(_expert_changed(te_ref, j))
    def _():
        w1b_scr[...] = w1_ref[0].astype(BF16)
        w3b_scr[...] = w3_ref[0].astype(BF16)

    @pl.when(j < nv_ref[0])
    def _():
        xs = xs_ref[...]
        tf = u_ref.shape[1]
        col_chunks = [slice(c0, min(c0 + MOE_UP_CHUNK, tf)) for c0 in range(0, tf, MOE_UP_CHUNK)]
        both = lambda cols: (_dot(xs, w1b_scr[:, cols]), _dot(xs, w3b_scr[:, cols]))
        nxt = both(col_chunks[0])
        for idx, cols in enumerate(col_chunks):
            a, b = nxt
            if idx + 1 < len(col_chunks):
                nxt = both(col_chunks[idx + 1])
            u_ref[:, cols] = (_silu(a) * b).astype(BF16)

    @pl.when(j >= nv_ref[0])
    def _():
        u_ref[...] = jnp.zeros_like(u_ref)


def _expert_up(xs, tile_expert, n_valid, w1, w3, layer):
    s, d = xs.shape
    dff = w1.shape[3]
    tf = dff // 2 if (dff // 2) % MXU_WIDTH == 0 else dff
    nt = s // MOE_TILE

    def rows(f, j, te, nv):
        return (jnp.minimum(j, nv[0] - 1), 0)

    return pl.pallas_call(
        _expert_up_kernel,
        grid_spec=pltpu.PrefetchScalarGridSpec(
            num_scalar_prefetch=2,
            grid=(dff // tf, nt),
            in_specs=[
                pl.BlockSpec((MOE_TILE, d), rows),
                pl.BlockSpec((None, 1, d, tf), lambda f, j, te, nv: (layer, te[j], 0, f)),
                pl.BlockSpec((None, 1, d, tf), lambda f, j, te, nv: (layer, te[j], 0, f)),
            ],
            out_specs=pl.BlockSpec((MOE_TILE, tf), lambda f, j, te, nv: (j, f)),
            scratch_shapes=[pltpu.VMEM((d, tf), BF16), pltpu.VMEM((d, tf), BF16)],
        ),
        out_shape=jax.ShapeDtypeStruct((s, dff), BF16),
        compiler_params=_cparams(("arbitrary", "arbitrary")),
        name="moe_expert_up",
    )(tile_expert, n_valid, xs, w1, w3)


def _expert_down_kernel(te_ref, nv_ref, u_ref, w2_ref, y_ref, w2b_scr):
    j = pl.program_id(1)

    @pl.when(_expert_changed(te_ref, j))
    def _():
        w2b_scr[...] = w2_ref[0].astype(BF16)

    @pl.when(j < nv_ref[0])
    def _():
        _store_as_tiles(y_ref, _dot(u_ref[...], w2b_scr[...]))

    @pl.when(j >= nv_ref[0])
    def _():
        y_ref[...] = jnp.zeros_like(y_ref)


def _expert_down(u, tile_expert, n_valid, w2, layer):
    s, dff = u.shape
    d = w2.shape[3]
    nt = s // MOE_TILE
    return pl.pallas_call(
        _expert_down_kernel,
        grid_spec=pltpu.PrefetchScalarGridSpec(
            num_scalar_prefetch=2,
            grid=(1, nt),
            in_specs=[
                pl.BlockSpec((MOE_TILE, dff), lambda n, j, te, nv: (jnp.minimum(j, nv[0] - 1), 0)),
                pl.BlockSpec((None, 1, dff, d), lambda n, j, te, nv: (layer, te[j], 0, 0)),
            ],
            out_specs=pl.BlockSpec((MOE_TILE * TILE_ROWS, LANES), lambda n, j, te, nv: (j, 0)),
            scratch_shapes=[pltpu.VMEM((dff, d), BF16)],
        ),
        out_shape=jax.ShapeDtypeStruct((s * TILE_ROWS, LANES), F32),
        compiler_params=_cparams(("arbitrary", "arbitrary")),
        name="moe_expert_down",
    )(tile_expert, n_valid, u, w2)


def _combine_kernel(d1_ref, d2_ref, x_ref, mod_ref, route_ref, *rest, d, tile0, final_norm):
    if final_norm:
        fg_ref, ys_hbm, o_ref, buf1, buf2, sem = rest
    else:
        ys_hbm, o_ref, buf1, buf2, sem = rest
    tc = x_ref.shape[0]
    j = pl.program_id(0)

    def issue(step):
        slot = step % 2
        base = (tile0 + step) * tc
        _start_row_gather(ys_hbm, lambda i: d1_ref[base + i], buf1.at[slot], sem.at[0, slot], tc)
        _start_row_gather(ys_hbm, lambda i: d2_ref[base + i], buf2.at[slot], sem.at[1, slot], tc)

    @pl.when(j == 0)
    def _():
        issue(0)

    @pl.when(j + 1 < pl.num_programs(0))
    def _():
        issue(j + 1)

    slot = j % 2
    _wait_row_gather(ys_hbm, buf1.at[slot], sem.at[0, slot], tc)
    _wait_row_gather(ys_hbm, buf2.at[slot], sem.at[1, slot], tc)
    route = route_ref[...]
    lane = lax.broadcasted_iota(jnp.int32, route.shape, 1)
    w1 = jnp.sum(jnp.where(lane == ROUTE_W1, route, 0.0), axis=-1, keepdims=True)
    w2 = jnp.sum(jnp.where(lane == ROUTE_W2, route, 0.0), axis=-1, keepdims=True)
    gate = mod_ref[0, :, 5 * d:6 * d]
    blocks = []
    for c in range(TILE_ROWS):
        cols = slice(c * LANES, (c + 1) * LANES)
        e1 = _tile_lane_block(buf1.at[slot], tc, c)
        e2 = _tile_lane_block(buf2.at[slot], tc, c)
        blocks.append(x_ref[:, cols] + gate[:, cols] * (w1 * e1 + w2 * e2))
    if final_norm:
        ms = sum(jnp.sum(b * b, axis=-1, keepdims=True) for b in blocks) * (1.0 / d)
        inv = lax.rsqrt(ms + EPS)
        blocks = [b * inv * fg_ref[:, c * LANES:(c + 1) * LANES] for c, b in enumerate(blocks)]
    for c, b in enumerate(blocks):
        o_ref[:, c * LANES:(c + 1) * LANES] = b


def _combine(x, mod3, route, ys, dest1, dest2, seg_len, *, row0=0, n_rows=None, final_g=None):
    t, d = x.shape
    n_rows = t if n_rows is None else n_rows
    tc = 512
    tps = seg_len // tc
    tile0 = row0 // tc
    final_norm = final_g is not None
    in_specs = [
        pl.BlockSpec((tc, d), lambda i, a, b: (tile0 + i, 0)),
        pl.BlockSpec((1, 1, 6 * d), lambda i, a, b: ((tile0 + i) // tps, 0, 0)),
        pl.BlockSpec((tc, LANES), lambda i, a, b: (tile0 + i, 0)),
    ]
    args = [dest1, dest2, x, mod3, route]
    if final_norm:
        in_specs.append(pl.BlockSpec((1, d), lambda i, a, b: (0, 0)))
        args.append(final_g.reshape(1, d))
    in_specs.append(pl.BlockSpec(memory_space=pl.ANY))
    args.append(ys)
    return pl.pallas_call(
        functools.partial(_combine_kernel, d=d, tile0=tile0, final_norm=final_norm),
        grid_spec=pltpu.PrefetchScalarGridSpec(
            num_scalar_prefetch=2,
            grid=(n_rows // tc,),
            in_specs=in_specs,
            out_specs=pl.BlockSpec((tc, d), lambda i, a, b: (i, 0)),
            scratch_shapes=[pltpu.VMEM((2, tc * TILE_ROWS, LANES), F32), pltpu.VMEM((2, tc * TILE_ROWS, LANES), F32),
                            pltpu.SemaphoreType.DMA((2, 2))],
        ),
        out_shape=jax.ShapeDtypeStruct((n_rows, d), F32),
        compiler_params=_cparams(("arbitrary",)),
        name="moe_combine",
    )(*args)


def _moe_ffn(x, mod3, g, rw_pad, w1, w3, w2, layer, seg_len, final_g=None):
    t, d = x.shape
    h, route, counts = _route(x, mod3, g, rw_pad, seg_len)
    cnt = counts[0, :N_EXPERTS].astype(jnp.int32)
    padded = (cnt + MOE_TILE - 1) // MOE_TILE * MOE_TILE
    ends = jnp.cumsum(padded)
    offs = ends - padded
    starts = jnp.cumsum(cnt) - cnt
    e1 = route[:, ROUTE_E1].astype(jnp.int32)
    e2 = route[:, ROUTE_E2].astype(jnp.int32)
    r1 = route[:, ROUTE_R1].astype(jnp.int32)
    r2 = route[:, ROUTE_R2].astype(jnp.int32)
    pos1, pos2 = starts[e1] + r1, starts[e2] + r2
    dest1, dest2 = offs[e1] + r1, offs[e2] + r2
    n_tiles = (2 * t) // MOE_TILE + N_EXPERTS
    tile_start = jnp.arange(n_tiles, dtype=jnp.int32) * MOE_TILE
    tile_expert = jnp.minimum(jnp.sum((tile_start[:, None] >= ends[None, :]).astype(jnp.int32), axis=1), N_EXPERTS - 1)
    tile_cs = starts[tile_expert] + (tile_start - offs[tile_expert])
    n_valid = (ends[-1:] // MOE_TILE).astype(jnp.int32)
    xs = _gather_sorted(h, pos1, pos2, tile_cs, n_valid)
    u = _expert_up(xs, tile_expert, n_valid, w1, w3, layer)
    ys = _expert_down(u, tile_expert, n_valid, w2, layer)
    if final_g is None:
        return _combine(x, mod3, route, ys, dest1, dest2, seg_len)
    return (_combine(x, mod3, route, ys, dest1, dest2, seg_len, row0=0, n_rows=seg_len, final_g=final_g),
            _combine(x, mod3, route, ys, dest1, dest2, seg_len, row0=seg_len, n_rows=t - seg_len, final_g=final_g))


def _final_norm_kernel(x_ref, g_ref, o_ref):
    x = x_ref[...]
    ms = jnp.mean(x * x, axis=-1, keepdims=True)
    o_ref[...] = x * lax.rsqrt(ms + EPS) * g_ref[...]


def _final_norm(x, g, *, row0, n_rows):
    d = x.shape[1]
    tm = 512
    blk0 = row0 // tm
    return pl.pallas_call(
        _final_norm_kernel,
        grid=(n_rows // tm,),
        in_specs=[pl.BlockSpec((tm, d), lambda i: (blk0 + i, 0)), pl.BlockSpec((1, d), lambda i: (0, 0))],
        out_specs=pl.BlockSpec((tm, d), lambda i: (i, 0)),
        out_shape=jax.ShapeDtypeStruct((n_rows, d), F32),
        compiler_params=_cparams(("parallel",)),
        name="final_norm",
    )(x, g.reshape(1, d))


def _rope_tables(n_tokens):
    pos = jnp.arange(n_tokens, dtype=jnp.int32)
    rowcol = jnp.stack([pos // GRID_W, pos % GRID_W], axis=-1).astype(F32)
    inv_freq = 1.0 / (ROPE_THETA ** (jnp.arange(ROPE_FREQS, dtype=F32) / ROPE_FREQS))
    ang = rowcol[:, :, None] * inv_freq
    cos = jnp.cos(ang)
    sin = jnp.sin(ang)
    cos_h = jnp.stack([cos, cos], axis=2).reshape(n_tokens, HEAD_DIM)
    sin_h = jnp.stack([-sin, sin], axis=2).reshape(n_tokens, HEAD_DIM)
    cos_t = jnp.tile(cos_h, (1, LANES // HEAD_DIM))
    sin_t = jnp.tile(sin_h, (1, LANES // HEAD_DIM))
    return (jnp.stack([jnp.ones_like(cos_t), cos_t]), jnp.stack([jnp.zeros_like(sin_t), sin_t]))


def _dft_matrices(n):
    n1 = 1 << (int(math.log2(n)) // 2)
    assert n % n1 == 0
    n2 = n // n1
    m = jnp.arange(n, dtype=jnp.int32)[None, :]
    ang_a = ((jnp.arange(n1, dtype=jnp.int32)[:, None] * m) % n1).astype(F32) * (2.0 * math.pi / n1)
    ang_b = ((jnp.arange(n2, dtype=jnp.int32)[:, None] * m) % n).astype(F32) * (2.0 * math.pi / n)
    ca, sa = jnp.cos(ang_a)[:, None, :], jnp.sin(ang_a)[:, None, :]
    cb, sb = jnp.cos(ang_b)[None, :, :], jnp.sin(ang_b)[None, :, :]
    scale = n ** -0.5
    cos_m = ((ca * cb - sa * sb) * scale).reshape(n, n)
    nsin_m = ((sa * cb + ca * sb) * -scale).reshape(n, n)
    return cos_m.astype(BF16), nsin_m.astype(BF16)


def _channel_dft_weight():
    n = FOURIER_GROUP_DIM
    idx = jnp.arange(n, dtype=jnp.int32)
    ang = ((idx[:, None] * idx[None, :]) % n).astype(F32) * (2.0 * math.pi / n)
    eye = jnp.eye(FOURIER_GROUPS, dtype=F32)
    cos_bd = jnp.kron(eye, jnp.cos(ang) * n ** -0.5)
    sin_bd = jnp.kron(eye, jnp.sin(ang) * n ** -0.5)
    return jnp.concatenate([cos_bd, sin_bd], axis=1).astype(BF16)


def _repack_w_in(w_in, lay):
    depth, d, _ = w_in.shape
    widths = (ATT_WIDTH, KV_WIDTH, KV_WIDTH, GLA_K_WIDTH, GLA_K_WIDTH, GLA_V_WIDTH, GLA_V_WIDTH,
              GLA_LOW_RANK, GLA_LOW_RANK, FOURIER_WIDTH, N_BRANCHES * d)
    offs = [0]
    for w in widths:
        offs.append(offs[-1] + w)
    aq, ak, av, gq, gk, gv, gr, lrf, lrb, fu, gates = [w_in[:, :, offs[i]:offs[i + 1]] for i in range(len(widths))]
    used = lay["lr"] + 2 * GLA_LOW_RANK
    pad = jnp.zeros((depth, d, lay["total"] - used), w_in.dtype)
    return jnp.concatenate([aq, gv, gr, fu, gates, gq, gk, ak, av, lrf, lrb, pad], axis=-1).astype(BF16)


def kernel(x_prompt, x_sample, cache_k, cache_v, state_gla, c, c_ctx, ada_w, ada_b, norm1_g, norm2_g, w_in, q_norm_g, k_norm_g, alpha_up, alpha_b, gla_norm_g, w_branch, w_out, ffn_w1, ffn_w3, ffn_w2, router_w, moe_w1, moe_w3, moe_w2, final_g):
    batch, seq, d = x_prompt.shape
    dec_batch, dec_seq, _ = x_sample.shape
    depth = w_in.shape[0]
    past = cache_k.shape[2]
    seg_len = batch * seq
    assert seg_len == dec_seq, "context tokens and each latent request must fill equal segments"
    assert seq % GLA_CHUNK == 0 and dec_seq % GLA_CHUNK == 0
    assert COL_GATES % d == 0
    assert d == TILE_ROWS * LANES, "MoE row DMAs move one (8, 128) f32 tile per token"
    n_seg = 1 + dec_batch
    t = n_seg * seg_len
    lay = _in_layout(d)

    w_in_p = _repack_w_in(w_in, lay)
    w_branch_b = w_branch.astype(BF16)
    w_out_b = w_out.astype(BF16)
    ffn_w1_b, ffn_w3_b, ffn_w2_b = ffn_w1.astype(BF16), ffn_w3.astype(BF16), ffn_w2.astype(BF16)
    router_pad = jnp.pad(router_w, ((0, 0), (0, 0), (0, LANES - N_EXPERTS)))
    cos_tab, sin_tab = _rope_tables(dec_seq)
    dft_cos_s, dft_nsin_s = _dft_matrices(dec_seq)
    dft_cos_p, dft_nsin_p = _dft_matrices(seq)
    w_chan = _channel_dft_weight()
    ones_bd = jnp.kron(jnp.eye(ATT_HEADS, dtype=F32), jnp.ones((HEAD_DIM, HEAD_DIM), F32)).astype(BF16)
    q_g = jnp.tile(q_norm_g, (1, ATT_HEADS)).reshape(depth, 1, ATT_WIDTH)
    k_g = jnp.tile(k_norm_g, (1, ATT_KV_HEADS)).reshape(depth, 1, KV_WIDTH)
    aup_pad = jnp.zeros((depth, 2, LANES, GLA_K_WIDTH), F32)
    aup_pad = aup_pad.at[:, 0, :GLA_LOW_RANK].set(alpha_up[:, 0])
    aup_pad = aup_pad.at[:, 1, GLA_LOW_RANK:2 * GLA_LOW_RANK].set(alpha_up[:, 1])
    ab = alpha_b.reshape(depth, 2, 1, GLA_K_WIDTH)
    s0_lat = state_gla.transpose(0, 1, 2, 5, 3, 4).reshape(dec_batch, depth, 2, GLA_DV, GLA_K_WIDTH)
    s0_all = jnp.concatenate([jnp.zeros((1,) + s0_lat.shape[1:], F32), s0_lat], axis=0)

    def ctx_keys(x):
        xb = x.astype(BF16)[:, :, :, None, :]
        return jnp.broadcast_to(xb, (dec_batch, past, ATT_KV_HEADS, 2, HEAD_DIM)).reshape(dec_batch, past, 2 * KV_WIDTH)

    def ctx_values_t(x):
        xt = x.astype(BF16).transpose(0, 2, 3, 1)
        return jnp.concatenate([xt, jnp.ones_like(xt)], axis=2).reshape(dec_batch, ATT_KV_HEADS * VT_ROWS, past)

    cond = jnp.zeros((8, d), F32).at[0].set(c_ctx).at[1:1 + dec_batch].set(c)
    mod_all = _modulation(cond, ada_w, ada_b)

    x = jnp.concatenate([x_prompt.reshape(seg_len, d), x_sample.reshape(dec_batch * dec_seq, d)], axis=0)
    new_k, new_v, new_s = [], [], []
    for l in range(depth):
        mod3 = mod_all[l, :n_seg].reshape(n_seg, 1, 6 * d)
        z = _in_projection(x, mod3, norm1_g[l], w_in_p, l, seg_len)
        qn, k_out, k2, vt = _attention_prep(z, lay, cos_tab, sin_tab, q_g[l], k_g[l], ones_bd, seg_len)
        att_p = _attention(qn, k2, vt, None, None, row0=0, n_seq=batch, seq_len=seq, tq=min(ATT_QUERY_TILE, seq))
        att_s = _attention(qn, k2, vt, ctx_keys(cache_k[:, l]), ctx_values_t(cache_v[:, l]),
                           row0=seg_len, n_seq=dec_batch, seq_len=dec_seq, tq=min(ATT_QUERY_TILE, dec_seq))
        gla_f, gla_b, s_fin_f, s_fin_b = _gla(z, lay, aup_pad[l], ab[l], s0_all[:, l],
                                              seg_len=seg_len, n_seg=n_seg, prompt_len=seq)
        xcs = _fourier_channels(z, w_chan)
        four_p = _fourier_positions(xcs, dft_cos_p, dft_nsin_p, row0=0, n_seq=batch, seq_len=seq)
        four_s = _fourier_positions(xcs, dft_cos_s, dft_nsin_s, row0=seg_len, n_seq=dec_batch, seq_len=dec_seq)
        x = _merge(att_p, att_s, gla_f, gla_b, z, four_p, four_s, x, mod3, gla_norm_g[l], w_branch_b, w_out_b, l, seg_len)
        j = l // 2
        if l % 2 == 0:
            x = _dense_ffn(x, mod3, norm2_g[l], ffn_w1_b, ffn_w3_b, ffn_w2_b, j, seg_len)
        elif l + 1 < depth:
            x = _moe_ffn(x, mod3, norm2_g[l], router_pad[j], moe_w1, moe_w3, moe_w2, j, seg_len)
        else:
            y_p, y_s = _moe_ffn(x, mod3, norm2_g[l], router_pad[j], moe_w1, moe_w3, moe_w2, j, seg_len, final_g=final_g)
        new_k.append(k_out[:seg_len].reshape(batch, seq, ATT_KV_HEADS, HEAD_DIM))
        v_col = lay["akv"] + KV_WIDTH
        new_v.append(z[:seg_len, v_col:v_col + KV_WIDTH].astype(F32).reshape(batch, seq, ATT_KV_HEADS, HEAD_DIM))
        cps = seq // GLA_CHUNK
        sf = s_fin_f[0, cps - 1::cps]
        sb = s_fin_b[0, 0::cps]
        st = jnp.stack([sf, sb], axis=1).reshape(batch, 2, GLA_DV, GLA_HEADS, GLA_DK)
        new_s.append(st.transpose(0, 1, 3, 4, 2))
    if depth % 2 == 1:
        y_p = _final_norm(x, final_g, row0=0, n_rows=seg_len)
        y_s = _final_norm(x, final_g, row0=seg_len, n_rows=t - seg_len)
    y_prompt = y_p.reshape(batch, seq, d)
    y_sample = y_s.reshape(dec_batch, dec_seq, d)
    return (y_prompt, y_sample, jnp.stack(new_k, axis=1), jnp.stack(new_v, axis=1), jnp.stack(new_s, axis=1))
```
